```python
import math
import jax, jax.numpy as jnp
from jax import lax
import numpy as np

D_MODEL = 1024
BATCH = 8
SEQ = 4096
DEPTH = 1

N_META = 16
D_SSM = D_MODEL // 2
D_POOL = D_MODEL - D_SSM
SSM_GROUP = 16
SSM_GROUPS = D_SSM // SSM_GROUP
SSM_STATE = 64
POOL_WINDOWS = (2, 4, 8, 16)
POOL_GROUPS = len(POOL_WINDOWS)
POOL_GROUP_DIM = D_POOL // POOL_GROUPS
D_FF = ((8 * D_MODEL // 3 + 255) // 256) * 256
STEP_MIN = 1e-3
STEP_MAX = 1e-1
EPS = 1e-6

kernel_name = "hymba_s5_multiscale_pool_hybrid"


def rmsnorm(x, g):
    xf = x.astype(jnp.float32)
    return xf * lax.rsqrt(jnp.mean(xf * xf, axis=-1, keepdims=True) + EPS) * g.astype(jnp.float32)


def _complex_affine_combine(e1, e2):
    a1r, a1i, b1r, b1i = e1
    a2r, a2i, b2r, b2i = e2
    ar = a2r * a1r - a2i * a1i
    ai = a2r * a1i + a2i * a1r
    br = a2r * b1r - a2i * b1i + b2r
    bi = a2r * b1i + a2i * b1r + b2i
    return (ar, ai, br, bi)


def s5_mixer(u, lam_re, lam_im, log_step, b_re, b_im, c_re, c_im, d, glu_w, glu_b):
    L = u.shape[1]
    lr = jnp.minimum(lam_re.astype(jnp.float32), -1e-4)
    li = lam_im.astype(jnp.float32)
    step = jnp.exp(log_step.astype(jnp.float32))[:, None]
    mag = jnp.exp(lr * step)
    ang = li * step
    abr = mag * jnp.cos(ang)
    abi = mag * jnp.sin(ang)
    nr = abr - 1.0
    ni = abi
    den = lr * lr + li * li
    cr = ((nr * lr + ni * li) / den)[..., None]
    ci = ((ni * lr - nr * li) / den)[..., None]
    br = b_re.astype(jnp.float32)
    bi = b_im.astype(jnp.float32)
    bbr = cr * br - ci * bi
    bbi = cr * bi + ci * br
    uf = u.astype(jnp.float32)
    bu_r = jnp.einsum('blgh,gph->blgp', uf, bbr)
    bu_i = jnp.einsum('blgh,gph->blgp', uf, bbi)
    a_r = jnp.broadcast_to(abr[None, None], (1, L) + abr.shape)
    a_i = jnp.broadcast_to(abi[None, None], (1, L) + abi.shape)
    _, _, sr, si = lax.associative_scan(_complex_affine_combine, (a_r, a_i, bu_r, bu_i), axis=1)
    y = (jnp.einsum('blgp,ghp->blgh', sr, c_re.astype(jnp.float32))
         - jnp.einsum('blgp,ghp->blgh', si, c_im.astype(jnp.float32))
         + d.astype(jnp.float32) * uf)
    g = jax.nn.gelu(y)
    gate = jnp.einsum('blgh,ghk->blgk', g, glu_w.astype(jnp.float32)) + glu_b.astype(jnp.float32)
    return g * jax.nn.sigmoid(gate)


def pool_mixer(v, pool_w, pool_scale):
    L = v.shape[1]
    vf = v.astype(jnp.float32)
    cs = jnp.cumsum(vf, axis=1)
    t = jnp.arange(1, L + 1, dtype=jnp.float32)
    outs = []
    for k, w in enumerate(POOL_WINDOWS):
        ck = cs[:, :, k]
        lower = jnp.pad(ck, ((0, 0), (w, 0), (0, 0)))[:, :L]
        cnt = jnp.minimum(t, float(w))[None, :, None]
        outs.append((ck - lower) / cnt - vf[:, :, k])
    p = jnp.stack(outs, axis=2)
    p = jnp.einsum('blkc,kcd->blkd', p, pool_w.astype(jnp.float32))
    return p * pool_scale.astype(jnp.float32)


def setup_inputs(seed: int = 0) -> dict:
    key = jax.random.key(seed)
    ks = jax.random.split(key, 24)
    f32 = jnp.float32
    G, H, P = SSM_GROUPS, SSM_GROUP, SSM_STATE
    n = jnp.arange(P, dtype=f32)
    x = jax.random.normal(ks[0], (BATCH, SEQ, D_MODEL), f32)
    meta_tokens = jax.random.normal(ks[1], (N_META, D_MODEL), f32)
    norm1_g = 1.0 + 0.02 * jax.random.normal(ks[2], (DEPTH, D_MODEL), f32)
    w_in = jax.random.normal(ks[3], (DEPTH, D_MODEL, D_MODEL), f32) * D_MODEL ** -0.5
    ssm_lambda_re = -0.5 + 0.01 * jax.random.normal(ks[4], (DEPTH, G, P), f32)
    ssm_lambda_im = math.pi * n + 0.01 * jax.random.normal(ks[5], (DEPTH, G, P), f32)
    ssm_log_step = jax.random.uniform(ks[6], (DEPTH, G), f32, math.log(STEP_MIN), math.log(STEP_MAX))
    ssm_b_re = jax.random.normal(ks[7], (DEPTH, G, P, H), f32) * (2.0 * H) ** -0.5
    ssm_b_im = jax.random.normal(ks[8], (DEPTH, G, P, H), f32) * (2.0 * H) ** -0.5
    ssm_c_re = jax.random.normal(ks[9], (DEPTH, G, H, P), f32) * (2.0 * P) ** -0.5 * 4.0
    ssm_c_im = jax.random.normal(ks[10], (DEPTH, G, H, P), f32) * (2.0 * P) ** -0.5 * 4.0
    ssm_d = jax.random.normal(ks[11], (DEPTH, G, H), f32)
    ssm_glu_w = jax.random.normal(ks[12], (DEPTH, G, H, H), f32) * H ** -0.5
    ssm_glu_b = 0.02 * jax.random.normal(ks[13], (DEPTH, G, H), f32)
    ssm_norm_g = 1.0 + 0.02 * jax.random.normal(ks[14], (DEPTH, D_SSM), f32)
    pool_w = jax.random.normal(ks[15], (DEPTH, POOL_GROUPS, POOL_GROUP_DIM, POOL_GROUP_DIM), f32) * POOL_GROUP_DIM ** -0.5
    pool_scale = 1.0 + 0.1 * jax.random.normal(ks[16], (DEPTH, POOL_GROUPS, POOL_GROUP_DIM), f32)
    pool_norm_g = 1.0 + 0.02 * jax.random.normal(ks[17], (DEPTH, D_POOL), f32)
    w_out = jax.random.normal(ks[18], (DEPTH, D_MODEL, D_MODEL), f32) * D_MODEL ** -0.5
    norm2_g = 1.0 + 0.02 * jax.random.normal(ks[19], (DEPTH, D_MODEL), f32)
    w_gate = jax.random.normal(ks[20], (DEPTH, D_MODEL, D_FF), f32) * D_MODEL ** -0.5
    w_up = jax.random.normal(ks[21], (DEPTH, D_MODEL, D_FF), f32) * D_MODEL ** -0.5
    w_down = jax.random.normal(ks[22], (DEPTH, D_FF, D_MODEL), f32) * D_FF ** -0.5
    final_norm_g = 1.0 + 0.02 * jax.random.normal(ks[23], (D_MODEL,), f32)
    return {"x": x, "meta_tokens": meta_tokens, "norm1_g": norm1_g, "w_in": w_in,
            "ssm_lambda_re": ssm_lambda_re, "ssm_lambda_im": ssm_lambda_im,
            "ssm_log_step": ssm_log_step, "ssm_b_re": ssm_b_re, "ssm_b_im": ssm_b_im,
            "ssm_c_re": ssm_c_re, "ssm_c_im": ssm_c_im, "ssm_d": ssm_d,
            "ssm_glu_w": ssm_glu_w, "ssm_glu_b": ssm_glu_b, "ssm_norm_g": ssm_norm_g,
            "pool_w": pool_w, "pool_scale": pool_scale, "pool_norm_g": pool_norm_g,
            "w_out": w_out, "norm2_g": norm2_g, "w_gate": w_gate, "w_up": w_up,
            "w_down": w_down, "final_norm_g": final_norm_g}


def reference(x, meta_tokens, norm1_g, w_in, ssm_lambda_re, ssm_lambda_im, ssm_log_step,
              ssm_b_re, ssm_b_im, ssm_c_re, ssm_c_im, ssm_d, ssm_glu_w, ssm_glu_b,
              ssm_norm_g, pool_w, pool_scale, pool_norm_g, w_out, norm2_g, w_gate, w_up,
              w_down, final_norm_g):
    B = x.shape[0]
    meta = jnp.broadcast_to(meta_tokens.astype(jnp.float32)[None], (B, N_META, D_MODEL))
    h = jnp.concatenate([meta, x.astype(jnp.float32)], axis=1)
    L = h.shape[1]
    for i in range(DEPTH):
        n1 = rmsnorm(h, norm1_g[i])
        proj = n1 @ w_in[i].astype(jnp.float32)
        u = proj[..., :D_SSM].reshape(B, L, SSM_GROUPS, SSM_GROUP)
        v = proj[..., D_SSM:].reshape(B, L, POOL_GROUPS, POOL_GROUP_DIM)
        ys = s5_mixer(u, ssm_lambda_re[i], ssm_lambda_im[i], ssm_log_step[i], ssm_b_re[i],
                      ssm_b_im[i], ssm_c_re[i], ssm_c_im[i], ssm_d[i], ssm_glu_w[i],
                      ssm_glu_b[i]).reshape(B, L, D_SSM)
        yp = pool_mixer(v, pool_w[i], pool_scale[i]).reshape(B, L, D_POOL)
        mixed = jnp.concatenate([rmsnorm(ys, ssm_norm_g[i]), rmsnorm(yp, pool_norm_g[i])], axis=-1)
        h = h + mixed @ w_out[i].astype(jnp.float32)
        n2 = rmsnorm(h, norm2_g[i])
        ff = jax.nn.silu(n2 @ w_gate[i].astype(jnp.float32)) * (n2 @ w_up[i].astype(jnp.float32))
        h = h + ff @ w_down[i].astype(jnp.float32)
    out = rmsnorm(h, final_norm_g)[:, N_META:]
    return out.astype(x.dtype)
```

```python
import functools
import math

import jax
import jax.numpy as jnp
from jax import lax
from jax.experimental import pallas as pl
from jax.experimental.pallas import tpu as pltpu

N_META = 16
SSM_GROUP = 16
SSM_STATE = 64
POOL_WINDOWS = (2, 4, 8, 16)
EPS = 1e-6

BATCH_ROWS = 8
HALF_GROUPS = 16
SCAN_LANES = 512
T_RELAYOUT = 64
T_MIX = 64
T_FFN = 128
FF_CHUNK = 256
VMEM_LIMIT = 56 * 1024 * 1024

_BF16 = jnp.bfloat16
_F32 = jnp.float32


def _rms(x, g):
    return x * lax.rsqrt(jnp.mean(x * x, axis=-1, keepdims=True) + EPS) * g


def _dot(a, b):
    return jnp.dot(a, b, preferred_element_type=_F32)


def _to_time_major_kernel(x_hbm, o_ref, sem):
    t0 = pl.program_id(0) * T_RELAYOUT
    copies = [
        pltpu.make_async_copy(x_hbm.at[b, pl.ds(t0, T_RELAYOUT), :], o_ref.at[:, b, :], sem.at[b])
        for b in range(BATCH_ROWS)
    ]
    for c in copies:
        c.start()
    for c in copies:
        c.wait()


def _to_batch_major_kernel(x_ref, o_hbm, sem):
    t0 = pl.program_id(0) * T_RELAYOUT
    copies = [
        pltpu.make_async_copy(x_ref.at[:, b, :], o_hbm.at[b, pl.ds(t0, T_RELAYOUT), :], sem.at[b])
        for b in range(BATCH_ROWS)
    ]
    for c in copies:
        c.start()
    for c in copies:
        c.wait()


def _to_time_major(x):
    B, S, D = x.shape
    return pl.pallas_call(
        _to_time_major_kernel,
        grid=(S // T_RELAYOUT,),
        in_specs=[pl.BlockSpec(memory_space=pl.ANY)],
        out_specs=pl.BlockSpec((T_RELAYOUT, B, D), lambda i: (i, 0, 0)),
        out_shape=jax.ShapeDtypeStruct((S, B, D), x.dtype),
        scratch_shapes=[pltpu.SemaphoreType.DMA((BATCH_ROWS,))],
        compiler_params=pltpu.CompilerParams(dimension_semantics=("arbitrary",)),
        name="to_time_major",
    )(x)


def _to_batch_major(xt):
    S, B, D = xt.shape
    return pl.pallas_call(
        _to_batch_major_kernel,
        grid=(S // T_RELAYOUT,),
        in_specs=[pl.BlockSpec((T_RELAYOUT, B, D), lambda i: (i, 0, 0))],
        out_specs=pl.BlockSpec(memory_space=pl.ANY),
        out_shape=jax.ShapeDtypeStruct((B, S, D), xt.dtype),
        scratch_shapes=[pltpu.SemaphoreType.DMA((BATCH_ROWS,))],
        compiler_params=pltpu.CompilerParams(dimension_semantics=("arbitrary",)),
        name="to_batch_major",
    )(xt)


def _mixer_kernel(xt_ref, meta_ref, n1g_ref, win_ref, ar_ref, ai_ref, bdr_ref, bdi_ref,
                  cbr_ref, cbi_ref, d_ref, gw_ref, gb_ref, sng_ref, pw_ref, ps_ref, png_ref,
                  wout_ref, h1_ref, bur, bui, st_r, st_i, halo):
    d_ssm = d_ref.shape[-1]
    n_state = ar_ref.shape[-1]
    half_in = HALF_GROUPS * SSM_GROUP
    half_state = HALF_GROUPS * SSM_STATE
    halo_rows = halo.shape[0]

    def project(rows):
        n1 = _rms(rows, n1g_ref[...]).astype(_BF16)
        return _dot(n1, win_ref[...])

    def scan(u, nt):
        rows = nt * BATCH_ROWS
        ub = u.astype(_BF16)
        for hh in range(d_ssm // half_in):
            uh = ub[:, hh * half_in:(hh + 1) * half_in]
            cols = slice(hh * half_state, (hh + 1) * half_state)
            bur[0:rows, cols] = _dot(uh, bdr_ref[hh])
            bui[0:rows, cols] = _dot(uh, bdi_ref[hh])
        for cb in range(n_state // SCAN_LANES):
            cs = slice(cb * SCAN_LANES, (cb + 1) * SCAN_LANES)
            ar = jnp.broadcast_to(ar_ref[:, cs], (BATCH_ROWS, SCAN_LANES))
            ai = jnp.broadcast_to(ai_ref[:, cs], (BATCH_ROWS, SCAN_LANES))

            def body(t, carry):
                sr, si = carry
                r0 = pl.multiple_of(t * BATCH_ROWS, BATCH_ROWS)
                nr = ar * sr - ai * si + bur[pl.ds(r0, BATCH_ROWS), cs]
                ni = ar * si + ai * sr + bui[pl.ds(r0, BATCH_ROWS), cs]
                bur[pl.ds(r0, BATCH_ROWS), cs] = nr
                bui[pl.ds(r0, BATCH_ROWS), cs] = ni
                return nr, ni

            sr, si = lax.fori_loop(0, nt, body, (st_r[:, cs], st_i[:, cs]), unroll=4)
            st_r[:, cs] = sr
            st_i[:, cs] = si

    @pl.when(pl.program_id(0) == 0)
    def _():
        st_r[...] = jnp.zeros_like(st_r)
        st_i[...] = jnp.zeros_like(st_i)
        pm = project(meta_ref[...])
        scan(pm[:, :d_ssm], N_META)
        halo[...] = pm[:, d_ssm:]

    x = xt_ref[...].reshape(T_MIX * BATCH_ROWS, xt_ref.shape[-1])
    rows = x.shape[0]
    proj = project(x)
    u = proj[:, :d_ssm]
    v = proj[:, d_ssm:]

    scan(u, T_MIX)
    ys = []
    for hh in range(d_ssm // half_in):
        cols = slice(hh * half_state, (hh + 1) * half_state)
        ys.append(_dot(bur[0:rows, cols].astype(_BF16), cbr_ref[hh])
                  + _dot(bui[0:rows, cols].astype(_BF16), cbi_ref[hh]))
    y = jnp.concatenate(ys, axis=-1) + d_ref[...] * u
    g = jax.nn.gelu(y)
    gate = _dot(g.astype(_BF16), gw_ref[...]) + gb_ref[...]
    y_ssm = _rms(g * jax.nn.sigmoid(gate), sng_ref[...])

    vext = jnp.concatenate([halo[...], v], axis=0)
    halo[...] = v[rows - halo_rows:, :]
    gd = v.shape[-1] // len(POOL_WINDOWS)
    yps = []
    for k, w in enumerate(POOL_WINDOWS):
        acc = vext[:, k * gd:(k + 1) * gd]
        span = 1
        while span < w:
            shift = span * BATCH_ROWS
            acc = acc[shift:, :] + acc[:-shift, :]
            span *= 2
        pk = acc[acc.shape[0] - rows:, :] * (1.0 / w) - v[:, k * gd:(k + 1) * gd]
        yps.append(_dot(pk.astype(_BF16), pw_ref[k]))
    y_pool = _rms(jnp.concatenate(yps, axis=-1) * ps_ref[...], png_ref[...])

    mixed = jnp.concatenate([y_ssm, y_pool], axis=-1).astype(_BF16)
    h1 = x + _dot(mixed, wout_ref[...])
    h1_ref[...] = h1.reshape(h1_ref.shape)


def _const_spec(a):
    nd = a.ndim
    return pl.BlockSpec(a.shape, lambda i: (0,) * nd, pipeline_mode=pl.Buffered(1))


def _mixer(xt, consts):
    S, B, D = xt.shape
    n_state = consts[3].shape[-1]
    d_pool = consts[14].shape[-1]
    rows = T_MIX * B
    return pl.pallas_call(
        _mixer_kernel,
        grid=(S // T_MIX,),
        in_specs=[pl.BlockSpec((T_MIX, B, D), lambda i: (i, 0, 0))] + [_const_spec(c) for c in consts],
        out_specs=pl.BlockSpec((T_MIX, B, D), lambda i: (i, 0, 0)),
        out_shape=jax.ShapeDtypeStruct((S, B, D), _F32),
        scratch_shapes=[
            pltpu.VMEM((rows, n_state), _F32),
            pltpu.VMEM((rows, n_state), _F32),
            pltpu.VMEM((B, n_state), _F32),
            pltpu.VMEM((B, n_state), _F32),
            pltpu.VMEM((N_META * B, d_pool), _F32),
        ],
        compiler_params=pltpu.CompilerParams(dimension_semantics=("arbitrary",),
                                             vmem_limit_bytes=VMEM_LIMIT),
        name="mixer",
    )(xt, *consts)


def _ffn_kernel(h_ref, n2g_ref, wg_ref, wu_ref, wd_ref, fg_ref, o_ref, act):
    h = h_ref[...].reshape(T_FFN * BATCH_ROWS, h_ref.shape[-1])
    n2 = _rms(h, n2g_ref[...]).astype(_BF16)
    d_ff = wg_ref.shape[-1]
    for c in range(d_ff // FF_CHUNK):
        cs = slice(c * FF_CHUNK, (c + 1) * FF_CHUNK)
        gate = _dot(n2, wg_ref[:, cs])
        up = _dot(n2, wu_ref[:, cs])
        act[:, cs] = (gate * jax.nn.sigmoid(gate) * up).astype(_BF16)
    h2 = h + _dot(act[...], wd_ref[...])
    o_ref[...] = _rms(h2, fg_ref[...]).reshape(o_ref.shape)


def _ffn(h1t, n2g, wg, wu, wd, fg):
    S, B, D = h1t.shape
    consts = (n2g, wg, wu, wd, fg)
    return pl.pallas_call(
        _ffn_kernel,
        grid=(S // T_FFN,),
        in_specs=[pl.BlockSpec((T_FFN, B, D), lambda i: (i, 0, 0))] + [_const_spec(c) for c in consts],
        out_specs=pl.BlockSpec((T_FFN, B, D), lambda i: (i, 0, 0)),
        out_shape=jax.ShapeDtypeStruct((S, B, D), _F32),
        scratch_shapes=[pltpu.VMEM((T_FFN * B, wg.shape[-1]), _BF16)],
        compiler_params=pltpu.CompilerParams(dimension_semantics=("arbitrary",),
                                             vmem_limit_bytes=VMEM_LIMIT),
        name="ffn",
    )(h1t, *consts)


def _block_diag(blocks):
    n, r, c = blocks.shape
    eye = jnp.eye(n, dtype=blocks.dtype)
    return jnp.einsum('grc,gk->grkc', blocks, eye).reshape(n * r, n * c)


def _halves(blocks):
    G = blocks.shape[0]
    return jnp.stack([_block_diag(blocks[h * HALF_GROUPS:(h + 1) * HALF_GROUPS])
                      for h in range(G // HALF_GROUPS)])


def _s5_params(lam_re, lam_im, log_step, b_re, b_im):
    lr = jnp.minimum(lam_re, -1e-4)
    li = lam_im
    step = jnp.exp(log_step)[:, None]
    mag = jnp.exp(lr * step)
    ang = li * step
    abr = mag * jnp.cos(ang)
    abi = mag * jnp.sin(ang)
    nr = abr - 1.0
    ni = abi
    den = lr * lr + li * li
    cr = ((nr * lr + ni * li) / den)[..., None]
    ci = ((ni * lr - nr * li) / den)[..., None]
    bbr = cr * b_re - ci * b_im
    bbi = cr * b_im + ci * b_re
    return abr, abi, bbr, bbi


def kernel(x, meta_tokens, norm1_g, w_in, ssm_lambda_re, ssm_lambda_im, ssm_log_step, ssm_b_re, ssm_b_im, ssm_c_re, ssm_c_im, ssm_d, ssm_glu_w, ssm_glu_b, ssm_norm_g, pool_w, pool_scale, pool_norm_g, w_out, norm2_g, w_gate, w_up, w_down, final_norm_g):
    B, S, D = x.shape
    assert B == BATCH_ROWS and norm1_g.shape[0] == 1
    assert S % T_MIX == 0 and S % T_FFN == 0 and S % T_RELAYOUT == 0
    f32 = _F32
    abr, abi, bbr, bbi = _s5_params(ssm_lambda_re[0].astype(f32), ssm_lambda_im[0].astype(f32),
                                    ssm_log_step[0].astype(f32), ssm_b_re[0].astype(f32),
                                    ssm_b_im[0].astype(f32))
    row = lambda a: a.astype(f32).reshape(1, -1)
    mixer_consts = (
        jnp.repeat(meta_tokens.astype(f32), B, axis=0),
        row(norm1_g[0]),
        w_in[0].astype(_BF16),
        row(abr), row(abi),
        _halves(jnp.swapaxes(bbr, 1, 2)).astype(_BF16),
        _halves(jnp.swapaxes(bbi, 1, 2)).astype(_BF16),
        _halves(jnp.swapaxes(ssm_c_re[0].astype(f32), 1, 2)).astype(_BF16),
        _halves(-jnp.swapaxes(ssm_c_im[0].astype(f32), 1, 2)).astype(_BF16),
        row(ssm_d[0]),
        _block_diag(ssm_glu_w[0].astype(f32)).astype(_BF16),
        row(ssm_glu_b[0]),
        row(ssm_norm_g[0]),
        pool_w[0].astype(_BF16),
        row(pool_scale[0]),
        row(pool_norm_g[0]),
        w_out[0].astype(_BF16),
    )
    xt = _to_time_major(x.astype(f32))
    h1t = _mixer(xt, mixer_consts)
    outt = _ffn(h1t, row(norm2_g[0]), w_gate[0].astype(_BF16), w_up[0].astype(_BF16),
                w_down[0].astype(_BF16), row(final_norm_g))
    return _to_batch_major(outt).astype(x.dtype)
```

```python
import functools
import math

import jax
import jax.numpy as jnp
from jax import lax
from jax.experimental import pallas as pl
from jax.experimental.pallas import tpu as pltpu

N_META = 16
SSM_GROUP = 16
SSM_STATE = 64
POOL_WINDOWS = (2, 4, 8, 16)
EPS = 1e-6

BATCH_ROWS = 8
HALF_GROUPS = 16
SCAN_LANES = 512
T_MIX = 64
T_FFN = 128
FF_CHUNK = 256
VMEM_LIMIT = 56 * 1024 * 1024

_BF16 = jnp.bfloat16
_F32 = jnp.float32


def _rms(x, g):
    return x * lax.rsqrt(jnp.mean(x * x, axis=-1, keepdims=True) + EPS) * g


def _dot(a, b):
    return jnp.dot(a, b, preferred_element_type=_F32)


def _tile_copies(hbm, vmem, sems, slot, tile, nt, to_vmem):
    copies = []
    for b in range(BATCH_ROWS):
        h = hbm.at[b, pl.ds(tile * nt, nt), :]
        v = vmem.at[slot, :, b, :]
        src, dst = (h, v) if to_vmem else (v, h)
        copies.append(pltpu.make_async_copy(src, dst, sems.at[slot, b]))
    return copies


def _mixer_kernel(x_hbm, meta_ref, n1g_ref, win_ref, ar_ref, ai_ref, bdr_ref, bdi_ref,
                  cbr_ref, cbi_ref, d_ref, gw_ref, gb_ref, sng_ref, pw_ref, ps_ref, png_ref,
                  wout_ref, h1_ref, xbuf, xsem, bur, bui, st_r, st_i, halo):
    i = pl.program_id(0)
    n_tiles = pl.num_programs(0)
    slot = lax.rem(i, 2)

    def fetch(tile, s):
        return _tile_copies(x_hbm, xbuf, xsem, s, tile, T_MIX, to_vmem=True)

    @pl.when(i == 0)
    def _():
        for c in fetch(0, 0):
            c.start()

    @pl.when(i + 1 < n_tiles)
    def _():
        for c in fetch(i + 1, 1 - slot):
            c.start()

    d_ssm = d_ref.shape[-1]
    n_state = ar_ref.shape[-1]
    half_in = HALF_GROUPS * SSM_GROUP
    half_state = HALF_GROUPS * SSM_STATE
    halo_rows = halo.shape[0]

    def project(rows):
        n1 = _rms(rows, n1g_ref[...]).astype(_BF16)
        return _dot(n1, win_ref[...])

    def scan(u, nt):
        rows = nt * BATCH_ROWS
        ub = u.astype(_BF16)
        for hh in range(d_ssm // half_in):
            uh = ub[:, hh * half_in:(hh + 1) * half_in]
            cols = slice(hh * half_state, (hh + 1) * half_state)
            bur[0:rows, cols] = _dot(uh, bdr_ref[hh])
            bui[0:rows, cols] = _dot(uh, bdi_ref[hh])
        for cb in range(n_state // SCAN_LANES):
            cs = slice(cb * SCAN_LANES, (cb + 1) * SCAN_LANES)
            ar = jnp.broadcast_to(ar_ref[:, cs], (BATCH_ROWS, SCAN_LANES))
            ai = jnp.broadcast_to(ai_ref[:, cs], (BATCH_ROWS, SCAN_LANES))

            def body(t, carry):
                sr, si = carry
                r0 = pl.multiple_of(t * BATCH_ROWS, BATCH_ROWS)
                nr = ar * sr - ai * si + bur[pl.ds(r0, BATCH_ROWS), cs]
                ni = ar * si + ai * sr + bui[pl.ds(r0, BATCH_ROWS), cs]
                bur[pl.ds(r0, BATCH_ROWS), cs] = nr
                bui[pl.ds(r0, BATCH_ROWS), cs] = ni
                return nr, ni

            sr, si = lax.fori_loop(0, nt, body, (st_r[:, cs], st_i[:, cs]), unroll=4)
            st_r[:, cs] = sr
            st_i[:, cs] = si

    @pl.when(i == 0)
    def _():
        st_r[...] = jnp.zeros_like(st_r)
        st_i[...] = jnp.zeros_like(st_i)
        pm = project(meta_ref[...])
        scan(pm[:, :d_ssm], N_META)
        halo[...] = pm[:, d_ssm:]

    for c in fetch(i, slot):
        c.wait()
    x = xbuf[slot].reshape(T_MIX * BATCH_ROWS, xbuf.shape[-1])
    rows = x.shape[0]
    proj = project(x)
    u = proj[:, :d_ssm]
    v = proj[:, d_ssm:]

    scan(u, T_MIX)
    ys = []
    for hh in range(d_ssm // half_in):
        cols = slice(hh * half_state, (hh + 1) * half_state)
        ys.append(_dot(bur[0:rows, cols].astype(_BF16), cbr_ref[hh])
                  + _dot(bui[0:rows, cols].astype(_BF16), cbi_ref[hh]))
    y = jnp.concatenate(ys, axis=-1) + d_ref[...] * u
    g = jax.nn.gelu(y)
    gate = _dot(g.astype(_BF16), gw_ref[...]) + gb_ref[...]
    y_ssm = _rms(g * jax.nn.sigmoid(gate), sng_ref[...])

    vext = jnp.concatenate([halo[...], v], axis=0)
    halo[...] = v[rows - halo_rows:, :]
    gd = v.shape[-1] // len(POOL_WINDOWS)
    yps = []
    for k, w in enumerate(POOL_WINDOWS):
        acc = vext[:, k * gd:(k + 1) * gd]
        span = 1
        while span < w:
            shift = span * BATCH_ROWS
            acc = acc[shift:, :] + acc[:-shift, :]
            span *= 2
        pk = acc[acc.shape[0] - rows:, :] * (1.0 / w) - v[:, k * gd:(k + 1) * gd]
        yps.append(_dot(pk.astype(_BF16), pw_ref[k]))
    y_pool = _rms(jnp.concatenate(yps, axis=-1) * ps_ref[...], png_ref[...])

    mixed = jnp.concatenate([y_ssm, y_pool], axis=-1).astype(_BF16)
    h1 = x + _dot(mixed, wout_ref[...])
    h1_ref[...] = h1.reshape(h1_ref.shape)


def _const_spec(a):
    nd = a.ndim
    return pl.BlockSpec(a.shape, lambda i: (0,) * nd, pipeline_mode=pl.Buffered(1))


def _mixer(x, consts):
    B, S, D = x.shape
    n_state = consts[3].shape[-1]
    d_pool = consts[14].shape[-1]
    rows = T_MIX * B
    return pl.pallas_call(
        _mixer_kernel,
        grid=(S // T_MIX,),
        in_specs=[pl.BlockSpec(memory_space=pl.ANY)] + [_const_spec(c) for c in consts],
        out_specs=pl.BlockSpec((T_MIX, B, D), lambda i: (i, 0, 0)),
        out_shape=jax.ShapeDtypeStruct((S, B, D), _F32),
        scratch_shapes=[
            pltpu.VMEM((2, T_MIX, B, D), _F32),
            pltpu.SemaphoreType.DMA((2, BATCH_ROWS)),
            pltpu.VMEM((rows, n_state), _F32),
            pltpu.VMEM((rows, n_state), _F32),
            pltpu.VMEM((B, n_state), _F32),
            pltpu.VMEM((B, n_state), _F32),
            pltpu.VMEM((N_META * B, d_pool), _F32),
        ],
        compiler_params=pltpu.CompilerParams(dimension_semantics=("arbitrary",),
                                             vmem_limit_bytes=VMEM_LIMIT),
        name="mixer",
    )(x, *consts)


def _ffn_kernel(h_ref, n2g_ref, wg_ref, wu_ref, wd_ref, fg_ref, o_hbm, obuf, osem, act):
    i = pl.program_id(0)
    n_tiles = pl.num_programs(0)
    slot = lax.rem(i, 2)

    def writeback(tile, s):
        return _tile_copies(o_hbm, obuf, osem, s, tile, T_FFN, to_vmem=False)

    h = h_ref[...].reshape(T_FFN * BATCH_ROWS, h_ref.shape[-1])
    n2 = _rms(h, n2g_ref[...]).astype(_BF16)
    d_ff = wg_ref.shape[-1]
    for c in range(d_ff // FF_CHUNK):
        cs = slice(c * FF_CHUNK, (c + 1) * FF_CHUNK)
        gate = _dot(n2, wg_ref[:, cs])
        up = _dot(n2, wu_ref[:, cs])
        act[:, cs] = (gate * jax.nn.sigmoid(gate) * up).astype(_BF16)
    h2 = h + _dot(act[...], wd_ref[...])
    out = _rms(h2, fg_ref[...])

    @pl.when(i >= 2)
    def _():
        for c in writeback(i - 2, slot):
            c.wait()

    obuf[slot] = out.reshape(obuf.shape[1:])
    for c in writeback(i, slot):
        c.start()

    @pl.when(i == n_tiles - 1)
    def _():
        @pl.when(n_tiles >= 2)
        def _():
            for c in writeback(i - 1, 1 - slot):
                c.wait()
        for c in writeback(i, slot):
            c.wait()


def _ffn(h1t, n2g, wg, wu, wd, fg):
    S, B, D = h1t.shape
    consts = (n2g, wg, wu, wd, fg)
    return pl.pallas_call(
        _ffn_kernel,
        grid=(S // T_FFN,),
        in_specs=[pl.BlockSpec((T_FFN, B, D), lambda i: (i, 0, 0))] + [_const_spec(c) for c in consts],
        out_specs=pl.BlockSpec(memory_space=pl.ANY),
        out_shape=jax.ShapeDtypeStruct((B, S, D), _F32),
        scratch_shapes=[pltpu.VMEM((2, T_FFN, B, D), _F32),
                        pltpu.SemaphoreType.DMA((2, BATCH_ROWS)),
                        pltpu.VMEM((T_FFN * B, wg.shape[-1]), _BF16)],
        compiler_params=pltpu.CompilerParams(dimension_semantics=("arbitrary",),
                                             vmem_limit_bytes=VMEM_LIMIT),
        name="ffn",
    )(h1t, *consts)


def _block_diag(blocks):
    n, r, c = blocks.shape
    eye = jnp.eye(n, dtype=blocks.dtype)
    return jnp.einsum('grc,gk->grkc', blocks, eye).reshape(n * r, n * c)


def _halves(blocks):
    G = blocks.shape[0]
    return jnp.stack([_block_diag(blocks[h * HALF_GROUPS:(h + 1) * HALF_GROUPS])
                      for h in range(G // HALF_GROUPS)])


def _s5_params(lam_re, lam_im, log_step, b_re, b_im):
    lr = jnp.minimum(lam_re, -1e-4)
    li = lam_im
    step = jnp.exp(log_step)[:, None]
    mag = jnp.exp(lr * step)
    ang = li * step
    abr = mag * jnp.cos(ang)
    abi = mag * jnp.sin(ang)
    nr = abr - 1.0
    ni = abi
    den = lr * lr + li * li
    cr = ((nr * lr + ni * li) / den)[..., None]
    ci = ((ni * lr - nr * li) / den)[..., None]
    bbr = cr * b_re - ci * b_im
    bbi = cr * b_im + ci * b_re
    return abr, abi, bbr, bbi


def kernel(x, meta_tokens, norm1_g, w_in, ssm_lambda_re, ssm_lambda_im, ssm_log_step, ssm_b_re, ssm_b_im, ssm_c_re, ssm_c_im, ssm_d, ssm_glu_w, ssm_glu_b, ssm_norm_g, pool_w, pool_scale, pool_norm_g, w_out, norm2_g, w_gate, w_up, w_down, final_norm_g):
    B, S, D = x.shape
    assert B == BATCH_ROWS and norm1_g.shape[0] == 1
    assert S % T_MIX == 0 and S % T_FFN == 0
    f32 = _F32
    abr, abi, bbr, bbi = _s5_params(ssm_lambda_re[0].astype(f32), ssm_lambda_im[0].astype(f32),
                                    ssm_log_step[0].astype(f32), ssm_b_re[0].astype(f32),
                                    ssm_b_im[0].astype(f32))
    row = lambda a: a.astype(f32).reshape(1, -1)
    mixer_consts = (
        jnp.repeat(meta_tokens.astype(f32), B, axis=0),
        row(norm1_g[0]),
        w_in[0].astype(_BF16),
        row(abr), row(abi),
        _halves(jnp.swapaxes(bbr, 1, 2)).astype(_BF16),
        _halves(jnp.swapaxes(bbi, 1, 2)).astype(_BF16),
        _halves(jnp.swapaxes(ssm_c_re[0].astype(f32), 1, 2)).astype(_BF16),
        _halves(-jnp.swapaxes(ssm_c_im[0].astype(f32), 1, 2)).astype(_BF16),
        row(ssm_d[0]),
        _block_diag(ssm_glu_w[0].astype(f32)).astype(_BF16),
        row(ssm_glu_b[0]),
        row(ssm_norm_g[0]),
        pool_w[0].astype(_BF16),
        row(pool_scale[0]),
        row(pool_norm_g[0]),
        w_out[0].astype(_BF16),
    )
    h1t = _mixer(x.astype(f32), mixer_consts)
    out = _ffn(h1t, row(norm2_g[0]), w_gate[0].astype(_BF16), w_up[0].astype(_BF16),
               w_down[0].astype(_BF16), row(final_norm_g))
    return out.astype(x.dtype)
```

```python
import functools
import math

import jax
import jax.numpy as jnp
from jax import lax
from jax.experimental import pallas as pl
from jax.experimental.pallas import tpu as pltpu

N_META = 16
SSM_GROUP = 16
SSM_STATE = 64
POOL_WINDOWS = (2, 4, 8, 16)
EPS = 1e-6

BATCH_ROWS = 8
HALF_GROUPS = 16
SCAN_LANES = 512
T_MIX = 64
T_FFN = 128
FF_CHUNK = 256
VMEM_LIMIT = 56 * 1024 * 1024

_BF16 = jnp.bfloat16
_F32 = jnp.float32


def _rms(x, g):
    return x * lax.rsqrt(jnp.mean(x * x, axis=-1, keepdims=True) + EPS) * g


def _dot(a, b):
    return jnp.dot(a, b, preferred_element_type=_F32)


def _tile_copies(hbm, vmem, sems, slot, tile, nt, to_vmem):
    copies = []
    for b in range(BATCH_ROWS):
        h = hbm.at[b, pl.ds(tile * nt, nt), :]
        v = vmem.at[slot, :, b, :]
        src, dst = (h, v) if to_vmem else (v, h)
        copies.append(pltpu.make_async_copy(src, dst, sems.at[slot, b]))
    return copies


def _mixer_kernel(x_hbm, meta_ref, n1g_ref, win_ref, ar_ref, ai_ref, bdr_ref, bdi_ref,
                  cbr_ref, cbi_ref, d_ref, gw_ref, gb_ref, sng_ref, pw_ref, ps_ref, png_ref,
                  wout_ref, h1_ref, xbuf, xsem, bur, bui, st_r, st_i, halo):
    i = pl.program_id(0)
    n_tiles = pl.num_programs(0)
    slot = lax.rem(i, 2)

    def fetch(tile, s):
        return _tile_copies(x_hbm, xbuf, xsem, s, tile, T_MIX, to_vmem=True)

    @pl.when(i == 0)
    def _():
        for c in fetch(0, 0):
            c.start()

    @pl.when(i + 1 < n_tiles)
    def _():
        for c in fetch(i + 1, 1 - slot):
            c.start()

    d_ssm = d_ref.shape[-1]
    n_state = ar_ref.shape[-1]
    half_in = HALF_GROUPS * SSM_GROUP
    half_state = HALF_GROUPS * SSM_STATE
    halo_rows = halo.shape[0]

    def project(rows):
        n1 = _rms(rows, n1g_ref[...]).astype(_BF16)
        return _dot(n1, win_ref[...])

    def scan(u, nt):
        rows = nt * BATCH_ROWS
        ub = u.astype(_BF16)
        for cb in range(n_state // SCAN_LANES):
            cs = slice(cb * SCAN_LANES, (cb + 1) * SCAN_LANES)
            hh = (cb * SCAN_LANES) // half_state
            ws = slice(cb * SCAN_LANES - hh * half_state, (cb + 1) * SCAN_LANES - hh * half_state)
            uh = ub[:, hh * half_in:(hh + 1) * half_in]
            bur[0:rows, cs] = _dot(uh, bdr_ref[hh, :, ws])
            bui[0:rows, cs] = _dot(uh, bdi_ref[hh, :, ws])
            ar = jnp.broadcast_to(ar_ref[:, cs], (BATCH_ROWS, SCAN_LANES))
            ai = jnp.broadcast_to(ai_ref[:, cs], (BATCH_ROWS, SCAN_LANES))
            sr = st_r[:, cs]
            si = st_i[:, cs]
            for t in range(nt):
                rs = slice(t * BATCH_ROWS, (t + 1) * BATCH_ROWS)
                sr, si = (ar * sr - ai * si + bur[rs, cs], ar * si + ai * sr + bui[rs, cs])
                bur[rs, cs] = sr
                bui[rs, cs] = si
            st_r[:, cs] = sr
            st_i[:, cs] = si

    @pl.when(i == 0)
    def _():
        st_r[...] = jnp.zeros_like(st_r)
        st_i[...] = jnp.zeros_like(st_i)
        pm = project(meta_ref[...])
        scan(pm[:, :d_ssm], N_META)
        halo[...] = pm[:, d_ssm:]

    for c in fetch(i, slot):
        c.wait()
    x = xbuf[slot].reshape(T_MIX * BATCH_ROWS, xbuf.shape[-1])
    rows = x.shape[0]
    proj = project(x)
    u = proj[:, :d_ssm]
    v = proj[:, d_ssm:]

    scan(u, T_MIX)
    ys = []
    for hh in range(d_ssm // half_in):
        cols = slice(hh * half_state, (hh + 1) * half_state)
        ys.append(_dot(bur[0:rows, cols].astype(_BF16), cbr_ref[hh])
                  + _dot(bui[0:rows, cols].astype(_BF16), cbi_ref[hh]))
    y = jnp.concatenate(ys, axis=-1) + d_ref[...] * u
    g = jax.nn.gelu(y)
    gate = _dot(g.astype(_BF16), gw_ref[...]) + gb_ref[...]
    y_ssm = _rms(g * jax.nn.sigmoid(gate), sng_ref[...])

    vext = jnp.concatenate([halo[...], v], axis=0)
    halo[...] = v[rows - halo_rows:, :]
    gd = v.shape[-1] // len(POOL_WINDOWS)
    yps = []
    for k, w in enumerate(POOL_WINDOWS):
        acc = vext[:, k * gd:(k + 1) * gd]
        span = 1
        while span < w:
            shift = span * BATCH_ROWS
            acc = acc[shift:, :] + acc[:-shift, :]
            span *= 2
        pk = acc[acc.shape[0] - rows:, :] * (1.0 / w) - v[:, k * gd:(k + 1) * gd]
        yps.append(_dot(pk.astype(_BF16), pw_ref[k]))
    y_pool = _rms(jnp.concatenate(yps, axis=-1) * ps_ref[...], png_ref[...])

    mixed = jnp.concatenate([y_ssm, y_pool], axis=-1).astype(_BF16)
    h1 = x + _dot(mixed, wout_ref[...])
    h1_ref[...] = h1.reshape(h1_ref.shape)


def _const_spec(a):
    nd = a.ndim
    return pl.BlockSpec(a.shape, lambda i: (0,) * nd, pipeline_mode=pl.Buffered(1))


def _mixer(x, consts):
    B, S, D = x.shape
    n_state = consts[3].shape[-1]
    d_pool = consts[14].shape[-1]
    rows = T_MIX * B
    return pl.pallas_call(
        _mixer_kernel,
        grid=(S // T_MIX,),
        in_specs=[pl.BlockSpec(memory_space=pl.ANY)] + [_const_spec(c) for c in consts],
        out_specs=pl.BlockSpec((T_MIX, B, D), lambda i: (i, 0, 0)),
        out_shape=jax.ShapeDtypeStruct((S, B, D), _F32),
        scratch_shapes=[
            pltpu.VMEM((2, T_MIX, B, D), _F32),
            pltpu.SemaphoreType.DMA((2, BATCH_ROWS)),
            pltpu.VMEM((rows, n_state), _F32),
            pltpu.VMEM((rows, n_state), _F32),
            pltpu.VMEM((B, n_state), _F32),
            pltpu.VMEM((B, n_state), _F32),
            pltpu.VMEM((N_META * B, d_pool), _F32),
        ],
        compiler_params=pltpu.CompilerParams(dimension_semantics=("arbitrary",),
                                             vmem_limit_bytes=VMEM_LIMIT),
        name="mixer",
    )(x, *consts)


def _ffn_kernel(h_ref, n2g_ref, wg_ref, wu_ref, wd_ref, fg_ref, o_hbm, obuf, osem, act):
    i = pl.program_id(0)
    n_tiles = pl.num_programs(0)
    slot = lax.rem(i, 2)

    def writeback(tile, s):
        return _tile_copies(o_hbm, obuf, osem, s, tile, T_FFN, to_vmem=False)

    h = h_ref[...].reshape(T_FFN * BATCH_ROWS, h_ref.shape[-1])
    n2 = _rms(h, n2g_ref[...]).astype(_BF16)
    d_ff = wg_ref.shape[-1]
    for c in range(d_ff // FF_CHUNK):
        cs = slice(c * FF_CHUNK, (c + 1) * FF_CHUNK)
        gate = _dot(n2, wg_ref[:, cs])
        up = _dot(n2, wu_ref[:, cs])
        act[:, cs] = (gate * jax.nn.sigmoid(gate) * up).astype(_BF16)
    h2 = h + _dot(act[...], wd_ref[...])
    out = _rms(h2, fg_ref[...])

    @pl.when(i >= 2)
    def _():
        for c in writeback(i - 2, slot):
            c.wait()

    obuf[slot] = out.reshape(obuf.shape[1:])
    for c in writeback(i, slot):
        c.start()

    @pl.when(i == n_tiles - 1)
    def _():
        @pl.when(n_tiles >= 2)
        def _():
            for c in writeback(i - 1, 1 - slot):
                c.wait()
        for c in writeback(i, slot):
            c.wait()


def _ffn(h1t, n2g, wg, wu, wd, fg):
    S, B, D = h1t.shape
    consts = (n2g, wg, wu, wd, fg)
    return pl.pallas_call(
        _ffn_kernel,
        grid=(S // T_FFN,),
        in_specs=[pl.BlockSpec((T_FFN, B, D), lambda i: (i, 0, 0))] + [_const_spec(c) for c in consts],
        out_specs=pl.BlockSpec(memory_space=pl.ANY),
        out_shape=jax.ShapeDtypeStruct((B, S, D), _F32),
        scratch_shapes=[pltpu.VMEM((2, T_FFN, B, D), _F32),
                        pltpu.SemaphoreType.DMA((2, BATCH_ROWS)),
                        pltpu.VMEM((T_FFN * B, wg.shape[-1]), _BF16)],
        compiler_params=pltpu.CompilerParams(dimension_semantics=("arbitrary",),
                                             vmem_limit_bytes=VMEM_LIMIT),
        name="ffn",
    )(h1t, *consts)


def _block_diag(blocks):
    n, r, c = blocks.shape
    eye = jnp.eye(n, dtype=blocks.dtype)
    return jnp.einsum('grc,gk->grkc', blocks, eye).reshape(n * r, n * c)


def _halves(blocks):
    G = blocks.shape[0]
    return jnp.stack([_block_diag(blocks[h * HALF_GROUPS:(h + 1) * HALF_GROUPS])
                      for h in range(G // HALF_GROUPS)])


def _s5_params(lam_re, lam_im, log_step, b_re, b_im):
    lr = jnp.minimum(lam_re, -1e-4)
    li = lam_im
    step = jnp.exp(log_step)[:, None]
    mag = jnp.exp(lr * step)
    ang = li * step
    abr = mag * jnp.cos(ang)
    abi = mag * jnp.sin(ang)
    nr = abr - 1.0
    ni = abi
    den = lr * lr + li * li
    cr = ((nr * lr + ni * li) / den)[..., None]
    ci = ((ni * lr - nr * li) / den)[..., None]
    bbr = cr * b_re - ci * b_im
    bbi = cr * b_im + ci * b_re
    return abr, abi, bbr, bbi


def kernel(x, meta_tokens, norm1_g, w_in, ssm_lambda_re, ssm_lambda_im, ssm_log_step, ssm_b_re, ssm_b_im, ssm_c_re, ssm_c_im, ssm_d, ssm_glu_w, ssm_glu_b, ssm_norm_g, pool_w, pool_scale, pool_norm_g, w_out, norm2_g, w_gate, w_up, w_down, final_norm_g):
    B, S, D = x.shape
    assert B == BATCH_ROWS and norm1_g.shape[0] == 1
    assert S % T_MIX == 0 and S % T_FFN == 0
    f32 = _F32
    abr, abi, bbr, bbi = _s5_params(ssm_lambda_re[0].astype(f32), ssm_lambda_im[0].astype(f32),
                                    ssm_log_step[0].astype(f32), ssm_b_re[0].astype(f32),
                                    ssm_b_im[0].astype(f32))
    row = lambda a: a.astype(f32).reshape(1, -1)
    mixer_consts = (
        jnp.repeat(meta_tokens.astype(f32), B, axis=0),
        row(norm1_g[0]),
        w_in[0].astype(_BF16),
        row(abr), row(abi),
        _halves(jnp.swapaxes(bbr, 1, 2)).astype(_BF16),
        _halves(jnp.swapaxes(bbi, 1, 2)).astype(_BF16),
        _halves(jnp.swapaxes(ssm_c_re[0].astype(f32), 1, 2)).astype(_BF16),
        _halves(-jnp.swapaxes(ssm_c_im[0].astype(f32), 1, 2)).astype(_BF16),
        row(ssm_d[0]),
        _block_diag(ssm_glu_w[0].astype(f32)).astype(_BF16),
        row(ssm_glu_b[0]),
        row(ssm_norm_g[0]),
        pool_w[0].astype(_BF16),
        row(pool_scale[0]),
        row(pool_norm_g[0]),
        w_out[0].astype(_BF16),
    )
    h1t = _mixer(x.astype(f32), mixer_consts)
    out = _ffn(h1t, row(norm2_g[0]), w_gate[0].astype(_BF16), w_up[0].astype(_BF16),
               w_down[0].astype(_BF16), row(final_norm_g))
    return out.astype(x.dtype)
```

```python
import jax
import jax.numpy as jnp
from jax import lax
from jax.experimental import pallas as pl
from jax.experimental.pallas import tpu as pltpu

N_META = 16
SSM_GROUP = 16
SSM_STATE = 64
POOL_WINDOWS = (2, 4, 8, 16)
EPS = 1e-6

BATCH_ROWS = 8
HALF_GROUPS = 16
SCAN_LANES = 512
T_TILE = 64
FF_CHUNK = 256
VMEM_LIMIT = 60 * 1024 * 1024

_BF16 = jnp.bfloat16
_F32 = jnp.float32


def _rms(x, g):
    return x * lax.rsqrt(jnp.mean(x * x, axis=-1, keepdims=True) + EPS) * g


def _dot(a, b):
    return jnp.dot(a, b, preferred_element_type=_F32)


def _tile_copies(hbm, vmem, sems, slot, tile, nt, to_vmem):
    copies = []
    for b in range(BATCH_ROWS):
        h = hbm.at[b, pl.ds(tile * nt, nt), :]
        v = vmem.at[slot, :, b, :]
        src, dst = (h, v) if to_vmem else (v, h)
        copies.append(pltpu.make_async_copy(src, dst, sems.at[slot, b]))
    return copies


def _block_kernel(x_hbm, meta_ref, n1g_ref, win_ref, ar_ref, ai_ref, bdr_ref, bdi_ref,
                  cbr_ref, cbi_ref, d_ref, gw_ref, gb_ref, sng_ref, pw_ref, ps_ref, png_ref,
                  wout_ref, n2g_ref, wg_ref, wu_ref, wd_ref, fg_ref, o_hbm,
                  xbuf, xsem, obuf, osem, hbuf, bur, bui, st_r, st_i, halo, act):
    i = pl.program_id(0)
    n_tiles = pl.num_programs(0) - 1
    slot = lax.rem(i, 2)
    oslot = 1 - slot
    last_tile = n_tiles - 1

    def fetch(tile, s):
        return _tile_copies(x_hbm, xbuf, xsem, s, tile, T_TILE, to_vmem=True)

    def writeback(tile, s):
        return _tile_copies(o_hbm, obuf, osem, s, tile, T_TILE, to_vmem=False)

    d_ssm = d_ref.shape[-1]
    n_state = ar_ref.shape[-1]
    half_in = HALF_GROUPS * SSM_GROUP
    half_state = HALF_GROUPS * SSM_STATE
    halo_rows = halo.shape[0]

    def project(rows):
        n1 = _rms(rows, n1g_ref[...]).astype(_BF16)
        return _dot(n1, win_ref[...])

    n_scan_blocks = n_state // SCAN_LANES
    n_halves = d_ssm // half_in
    n_ff_chunks = wg_ref.shape[-1] // FF_CHUNK

    def scan_block(ub, nt, cb):
        rows = nt * BATCH_ROWS
        cs = slice(cb * SCAN_LANES, (cb + 1) * SCAN_LANES)
        hh = (cb * SCAN_LANES) // half_state
        ws = slice(cb * SCAN_LANES - hh * half_state, (cb + 1) * SCAN_LANES - hh * half_state)
        uh = ub[:, hh * half_in:(hh + 1) * half_in]
        bur[0:rows, cs] = _dot(uh, bdr_ref[hh, :, ws])
        bui[0:rows, cs] = _dot(uh, bdi_ref[hh, :, ws])
        ar = jnp.broadcast_to(ar_ref[:, cs], (BATCH_ROWS, SCAN_LANES))
        ai = jnp.broadcast_to(ai_ref[:, cs], (BATCH_ROWS, SCAN_LANES))
        sr = st_r[:, cs]
        si = st_i[:, cs]
        for t in range(nt):
            rs = slice(t * BATCH_ROWS, (t + 1) * BATCH_ROWS)
            sr, si = (ar * sr - ai * si + bur[rs, cs], ar * si + ai * sr + bui[rs, cs])
            bur[rs, cs] = sr
            bui[rs, cs] = si
        st_r[:, cs] = sr
        st_i[:, cs] = si

    def ffn_chunk(n2, c):
        cs = slice(c * FF_CHUNK, (c + 1) * FF_CHUNK)
        gate = _dot(n2, wg_ref[:, cs])
        up = _dot(n2, wu_ref[:, cs])
        act[:, cs] = (gate * jax.nn.sigmoid(gate) * up).astype(_BF16)

    def step(x, h):
        rows = x.shape[0]
        ff = iter(range(n_ff_chunks))

        def ffn_chunks(n):
            for _ in range(n):
                c = next(ff, None)
                if c is not None:
                    ffn_chunk(n2, c)

        n1 = _rms(x, n1g_ref[...]).astype(_BF16)
        n2 = _rms(h, n2g_ref[...]).astype(_BF16)
        proj = _dot(n1, win_ref[...])
        u = proj[:, :d_ssm]
        v = proj[:, d_ssm:]

        ub = u.astype(_BF16)
        for cb in range(n_scan_blocks):
            scan_block(ub, T_TILE, cb)
            ffn_chunks(1)
        ys = []
        for hh in range(n_halves):
            cols = slice(hh * half_state, (hh + 1) * half_state)
            ys.append(_dot(bur[0:rows, cols].astype(_BF16), cbr_ref[hh])
                      + _dot(bui[0:rows, cols].astype(_BF16), cbi_ref[hh]))
            ffn_chunks(1)
        y = jnp.concatenate(ys, axis=-1) + d_ref[...] * u
        g = jax.nn.gelu(y)
        gb = g.astype(_BF16)
        ffn_chunks(1)
        gate = jnp.concatenate([_dot(gb[:, hh * half_in:(hh + 1) * half_in], gw_ref[hh])
                                for hh in range(n_halves)], axis=-1) + gb_ref[...]
        y_ssm = _rms(g * jax.nn.sigmoid(gate), sng_ref[...])
        ffn_chunks(1)

        vext = jnp.concatenate([halo[...], v], axis=0)
        halo[...] = v[rows - halo_rows:, :]
        gd = v.shape[-1] // len(POOL_WINDOWS)
        yps = []
        for k, w in enumerate(POOL_WINDOWS):
            acc = vext[:, k * gd:(k + 1) * gd]
            span = 1
            while span < w:
                shift = span * BATCH_ROWS
                acc = acc[shift:, :] + acc[:-shift, :]
                span *= 2
            pk = acc[acc.shape[0] - rows:, :] * (1.0 / w) - v[:, k * gd:(k + 1) * gd]
            yps.append(_dot(pk.astype(_BF16), pw_ref[k]))
        ffn_chunks(1)
        y_pool = _rms(jnp.concatenate(yps, axis=-1) * ps_ref[...], png_ref[...])
        ffn_chunks(n_ff_chunks)

        mixed = jnp.concatenate([y_ssm, y_pool], axis=-1).astype(_BF16)
        h_next = x + _dot(mixed, wout_ref[...])
        h2 = h + _dot(act[...], wd_ref[...])
        return h_next, _rms(h2, fg_ref[...])

    @pl.when(i == 0)
    def _():
        for c in fetch(0, 0):
            c.start()

    @pl.when(i < n_tiles)
    def _():
        for c in fetch(jnp.minimum(i + 1, last_tile), oslot):
            c.start()

    @pl.when(i >= 3)
    def _():
        for c in writeback(i - 3, oslot):
            c.wait()

    @pl.when(i == 0)
    def _():
        st_r[...] = jnp.zeros_like(st_r)
        st_i[...] = jnp.zeros_like(st_i)
        hbuf[...] = jnp.zeros_like(hbuf)
        pm = project(meta_ref[...])
        mb = pm[:, :d_ssm].astype(_BF16)
        for cb in range(n_scan_blocks):
            scan_block(mb, N_META, cb)
        halo[...] = pm[:, d_ssm:]

    for c in fetch(jnp.minimum(i, last_tile), slot):
        c.wait()

    x = xbuf[slot].reshape(T_TILE * BATCH_ROWS, xbuf.shape[-1])
    h_next, out = step(x, hbuf[...])
    hbuf[...] = h_next
    obuf[oslot] = out.reshape(obuf.shape[1:])

    @pl.when(i >= 1)
    def _():
        for c in writeback(i - 1, oslot):
            c.start()

    @pl.when(i == n_tiles)
    def _():
        @pl.when(n_tiles >= 2)
        def _():
            for c in writeback(i - 2, slot):
                c.wait()
        for c in writeback(i - 1, oslot):
            c.wait()


def _const_spec(a):
    nd = a.ndim
    return pl.BlockSpec(a.shape, lambda i: (0,) * nd, pipeline_mode=pl.Buffered(1))


def _block(x, consts, n_state, d_pool, d_ff):
    B, S, D = x.shape
    rows = T_TILE * B
    return pl.pallas_call(
        _block_kernel,
        grid=(S // T_TILE + 1,),
        in_specs=[pl.BlockSpec(memory_space=pl.ANY)] + [_const_spec(c) for c in consts],
        out_specs=pl.BlockSpec(memory_space=pl.ANY),
        out_shape=jax.ShapeDtypeStruct((B, S, D), _F32),
        scratch_shapes=[
            pltpu.VMEM((2, T_TILE, B, D), _F32),
            pltpu.SemaphoreType.DMA((2, BATCH_ROWS)),
            pltpu.VMEM((2, T_TILE, B, D), _F32),
            pltpu.SemaphoreType.DMA((2, BATCH_ROWS)),
            pltpu.VMEM((rows, D), _F32),
            pltpu.VMEM((rows, n_state), _F32),
            pltpu.VMEM((rows, n_state), _F32),
            pltpu.VMEM((B, n_state), _F32),
            pltpu.VMEM((B, n_state), _F32),
            pltpu.VMEM((N_META * B, d_pool), _F32),
            pltpu.VMEM((rows, d_ff), _BF16),
        ],
        compiler_params=pltpu.CompilerParams(dimension_semantics=("arbitrary",),
                                             vmem_limit_bytes=VMEM_LIMIT),
        name="block",
    )(x, *consts)


def _block_diag(blocks):
    n, r, c = blocks.shape
    eye = jnp.eye(n, dtype=blocks.dtype)
    return jnp.einsum('grc,gk->grkc', blocks, eye).reshape(n * r, n * c)


def _halves(blocks):
    G = blocks.shape[0]
    return jnp.stack([_block_diag(blocks[h * HALF_GROUPS:(h + 1) * HALF_GROUPS])
                      for h in range(G // HALF_GROUPS)])


def _s5_params(lam_re, lam_im, log_step, b_re, b_im):
    lr = jnp.minimum(lam_re, -1e-4)
    li = lam_im
    step = jnp.exp(log_step)[:, None]
    mag = jnp.exp(lr * step)
    ang = li * step
    abr = mag * jnp.cos(ang)
    abi = mag * jnp.sin(ang)
    nr = abr - 1.0
    ni = abi
    den = lr * lr + li * li
    cr = ((nr * lr + ni * li) / den)[..., None]
    ci = ((ni * lr - nr * li) / den)[..., None]
    bbr = cr * b_re - ci * b_im
    bbi = cr * b_im + ci * b_re
    return abr, abi, bbr, bbi


def kernel(x, meta_tokens, norm1_g, w_in, ssm_lambda_re, ssm_lambda_im, ssm_log_step, ssm_b_re, ssm_b_im, ssm_c_re, ssm_c_im, ssm_d, ssm_glu_w, ssm_glu_b, ssm_norm_g, pool_w, pool_scale, pool_norm_g, w_out, norm2_g, w_gate, w_up, w_down, final_norm_g):
    B, S, D = x.shape
    assert B == BATCH_ROWS and norm1_g.shape[0] == 1
    assert S % T_TILE == 0
    f32 = _F32
    abr, abi, bbr, bbi = _s5_params(ssm_lambda_re[0].astype(f32), ssm_lambda_im[0].astype(f32),
                                    ssm_log_step[0].astype(f32), ssm_b_re[0].astype(f32),
                                    ssm_b_im[0].astype(f32))
    row = lambda a: a.astype(f32).reshape(1, -1)
    consts = (
        jnp.repeat(meta_tokens.astype(f32), B, axis=0),
        row(norm1_g[0]),
        w_in[0].astype(_BF16),
        row(abr), row(abi),
        _halves(jnp.swapaxes(bbr, 1, 2)).astype(_BF16),
        _halves(jnp.swapaxes(bbi, 1, 2)).astype(_BF16),
        _halves(jnp.swapaxes(ssm_c_re[0].astype(f32), 1, 2)).astype(_BF16),
        _halves(-jnp.swapaxes(ssm_c_im[0].astype(f32), 1, 2)).astype(_BF16),
        row(ssm_d[0]),
        _halves(ssm_glu_w[0].astype(f32)).astype(_BF16),
        row(ssm_glu_b[0]),
        row(ssm_norm_g[0]),
        pool_w[0].astype(_BF16),
        row(pool_scale[0]),
        row(pool_norm_g[0]),
        w_out[0].astype(_BF16),
        row(norm2_g[0]),
        w_gate[0].astype(_BF16),
        w_up[0].astype(_BF16),
        w_down[0].astype(_BF16),
        row(final_norm_g),
    )
    out = _block(x.astype(f32), consts, n_state=abr.size, d_pool=pool_scale[0].size,
                 d_ff=w_gate.shape[-1])
    return out.astype(x.dtype)
```

```python
import jax
import jax.numpy as jnp
from jax import lax
from jax.experimental import pallas as pl
from jax.experimental.pallas import tpu as pltpu

N_META = 16
SSM_GROUP = 16
SSM_STATE = 64
POOL_WINDOWS = (2, 4, 8, 16)
EPS = 1e-6

BATCH_ROWS = 8
HALF_GROUPS = 16
OCT_GROUPS = 8
T_TILE = 64
FF_CHUNK = 256
VMEM_LIMIT = 60 * 1024 * 1024

_BF16 = jnp.bfloat16
_F32 = jnp.float32


def _rms(x, g):
    return x * lax.rsqrt(jnp.mean(x * x, axis=-1, keepdims=True) + EPS) * g


def _dot(a, b):
    return jnp.dot(a, b, preferred_element_type=_F32)


def _split_phases(a, n_pairs):
    blk = lambda t: a[t * BATCH_ROWS:(t + 1) * BATCH_ROWS]
    even = jnp.concatenate([blk(2 * k) for k in range(n_pairs)], axis=0)
    odd = jnp.concatenate([blk(2 * k + 1) for k in range(n_pairs)], axis=0)
    return even, odd


def _merge_phases(even, odd, n_pairs):
    blk = lambda a, k: a[k * BATCH_ROWS:(k + 1) * BATCH_ROWS]
    return jnp.concatenate([blk(p, k) for k in range(n_pairs) for p in (even, odd)], axis=0)


def _tile_copies(hbm, vmem, sems, slot, tile, nt, to_vmem):
    copies = []
    for b in range(BATCH_ROWS):
        h = hbm.at[b, pl.ds(tile * nt, nt), :]
        v = vmem.at[slot, :, b, :]
        src, dst = (h, v) if to_vmem else (v, h)
        copies.append(pltpu.make_async_copy(src, dst, sems.at[slot, b]))
    return copies


def _block_kernel(x_hbm, meta_ref, n1g_ref, win_ref, a2r_ref, a2i_ref, wzr_ref, wzi_ref,
                  vr_ref, vi_ref, loc_ref, d_ref, gw_ref, gb_ref, sng_ref, pw_ref, ps_ref, png_ref,
                  wout_ref, n2g_ref, wg_ref, wu_ref, wd_ref, fg_ref, o_hbm,
                  xbuf, xsem, obuf, osem, hbuf, zr, zi, st_r, st_i, u_last, halo, act):
    i = pl.program_id(0)
    n_tiles = pl.num_programs(0) - 1
    slot = lax.rem(i, 2)
    oslot = 1 - slot
    last_tile = n_tiles - 1

    def fetch(tile, s):
        return _tile_copies(x_hbm, xbuf, xsem, s, tile, T_TILE, to_vmem=True)

    def writeback(tile, s):
        return _tile_copies(o_hbm, obuf, osem, s, tile, T_TILE, to_vmem=False)

    d_ssm = d_ref.shape[-1]
    half_in = HALF_GROUPS * SSM_GROUP
    oct_in = OCT_GROUPS * SSM_GROUP
    oct_state = OCT_GROUPS * SSM_STATE
    n_octets = d_ssm // oct_in
    n_halves = d_ssm // half_in
    halo_rows = halo.shape[0]
    n_ff_chunks = wg_ref.shape[-1] // FF_CHUNK

    def project(rows):
        n1 = _rms(rows, n1g_ref[...]).astype(_BF16)
        return _dot(n1, win_ref[...])

    def s5_states(u_prev_odd, u_even, n_pairs, m):
        rows = n_pairs * BATCH_ROWS
        cs = slice(m * oct_state, (m + 1) * oct_state)
        ic = slice(m * oct_in, (m + 1) * oct_in)
        lhs = jnp.concatenate([u_prev_odd[:, ic], u_even[:, ic]], axis=1)
        zr[0:rows, cs] = _dot(lhs, wzr_ref[m])
        zi[0:rows, cs] = _dot(lhs, wzi_ref[m])
        ar = jnp.broadcast_to(a2r_ref[:, cs], (BATCH_ROWS, oct_state))
        ai = jnp.broadcast_to(a2i_ref[:, cs], (BATCH_ROWS, oct_state))
        sr = st_r[:, cs]
        si = st_i[:, cs]
        for k in range(n_pairs):
            rs = slice(k * BATCH_ROWS, (k + 1) * BATCH_ROWS)
            sr, si = (ar * sr - ai * si + zr[rs, cs], ar * si + ai * sr + zi[rs, cs])
            zr[rs, cs] = sr
            zi[rs, cs] = si
        st_r[:, cs] = sr
        st_i[:, cs] = si

    def s5_inputs(u, n_pairs):
        u_even, u_odd = _split_phases(u, n_pairs)
        u_prev_odd = jnp.concatenate([u_last[...], u_odd[:-BATCH_ROWS]], axis=0)
        u_last[...] = u_odd[-BATCH_ROWS:]
        return u_prev_odd.astype(_BF16), u_even.astype(_BF16), u_odd.astype(_BF16)

    def ffn_chunk(n2, c):
        cs = slice(c * FF_CHUNK, (c + 1) * FF_CHUNK)
        gate = _dot(n2, wg_ref[:, cs])
        up = _dot(n2, wu_ref[:, cs])
        act[:, cs] = (gate * jax.nn.sigmoid(gate) * up).astype(_BF16)

    def step(x, h):
        rows = x.shape[0]
        n_pairs = T_TILE // 2
        prows = n_pairs * BATCH_ROWS
        ff = iter(range(n_ff_chunks))

        def ffn_chunks(n):
            for _ in range(n):
                c = next(ff, None)
                if c is not None:
                    ffn_chunk(n2, c)

        n1 = _rms(x, n1g_ref[...]).astype(_BF16)
        n2 = _rms(h, n2g_ref[...]).astype(_BF16)
        proj = _dot(n1, win_ref[...])
        u = proj[:, :d_ssm]
        v = proj[:, d_ssm:]

        u_prev_odd, u_even, u_odd = s5_inputs(u, n_pairs)
        pair_out = []
        for m in range(n_octets):
            s5_states(u_prev_odd, u_even, n_pairs, m)
            ffn_chunks(1)
        for m in range(n_octets):
            cs = slice(m * oct_state, (m + 1) * oct_state)
            pair_out.append(_dot(zr[0:prows, cs].astype(_BF16), vr_ref[m])
                            + _dot(zi[0:prows, cs].astype(_BF16), vi_ref[m]))
            if m % 2 == 1:
                ffn_chunks(1)
        y_even = jnp.concatenate([p[:, :oct_in] for p in pair_out], axis=-1)
        y_odd = jnp.concatenate([p[:, oct_in:] for p in pair_out], axis=-1)
        y_odd = y_odd + jnp.concatenate(
            [_dot(u_odd[:, hh * half_in:(hh + 1) * half_in], loc_ref[hh]) for hh in range(n_halves)], axis=-1)
        y = _merge_phases(y_even, y_odd, n_pairs) + d_ref[...] * u
        g = jax.nn.gelu(y)
        gb = g.astype(_BF16)
        ffn_chunks(1)
        gate = jnp.concatenate([_dot(gb[:, hh * half_in:(hh + 1) * half_in], gw_ref[hh])
                                for hh in range(n_halves)], axis=-1) + gb_ref[...]
        y_ssm = _rms(g * jax.nn.sigmoid(gate), sng_ref[...])
        ffn_chunks(1)

        vext = jnp.concatenate([halo[...], v], axis=0)
        halo[...] = v[rows - halo_rows:, :]
        gd = v.shape[-1] // len(POOL_WINDOWS)
        yps = []
        for k, w in enumerate(POOL_WINDOWS):
            acc = vext[:, k * gd:(k + 1) * gd]
            span = 1
            while span < w:
                shift = span * BATCH_ROWS
                acc = acc[shift:, :] + acc[:-shift, :]
                span *= 2
            pk = acc[acc.shape[0] - rows:, :] * (1.0 / w) - v[:, k * gd:(k + 1) * gd]
            yps.append(_dot(pk.astype(_BF16), pw_ref[k]))
        ffn_chunks(1)
        y_pool = _rms(jnp.concatenate(yps, axis=-1) * ps_ref[...], png_ref[...])
        ffn_chunks(n_ff_chunks)

        mixed = jnp.concatenate([y_ssm, y_pool], axis=-1).astype(_BF16)
        h_next = x + _dot(mixed, wout_ref[...])
        h2 = h + _dot(act[...], wd_ref[...])
        return h_next, _rms(h2, fg_ref[...])

    @pl.when(i == 0)
    def _():
        for c in fetch(0, 0):
            c.start()

    @pl.when(i < n_tiles)
    def _():
        for c in fetch(jnp.minimum(i + 1, last_tile), oslot):
            c.start()

    @pl.when(i >= 3)
    def _():
        for c in writeback(i - 3, oslot):
            c.wait()

    @pl.when(i == 0)
    def _():
        st_r[...] = jnp.zeros_like(st_r)
        st_i[...] = jnp.zeros_like(st_i)
        u_last[...] = jnp.zeros_like(u_last)
        hbuf[...] = jnp.zeros_like(hbuf)
        pm = project(meta_ref[...])
        u_prev_odd, u_even, _ = s5_inputs(pm[:, :d_ssm], N_META // 2)
        for m in range(n_octets):
            s5_states(u_prev_odd, u_even, N_META // 2, m)
        halo[...] = pm[:, d_ssm:]

    for c in fetch(jnp.minimum(i, last_tile), slot):
        c.wait()

    x = xbuf[slot].reshape(T_TILE * BATCH_ROWS, xbuf.shape[-1])
    h_next, out = step(x, hbuf[...])
    hbuf[...] = h_next
    obuf[oslot] = out.reshape(obuf.shape[1:])

    @pl.when(i >= 1)
    def _():
        for c in writeback(i - 1, oslot):
            c.start()

    @pl.when(i == n_tiles)
    def _():
        @pl.when(n_tiles >= 2)
        def _():
            for c in writeback(i - 2, slot):
                c.wait()
        for c in writeback(i - 1, oslot):
            c.wait()


def _const_spec(a):
    nd = a.ndim
    return pl.BlockSpec(a.shape, lambda i: (0,) * nd, pipeline_mode=pl.Buffered(1))


def _block(x, consts, n_state, d_ssm, d_pool, d_ff):
    B, S, D = x.shape
    rows = T_TILE * B
    return pl.pallas_call(
        _block_kernel,
        grid=(S // T_TILE + 1,),
        in_specs=[pl.BlockSpec(memory_space=pl.ANY)] + [_const_spec(c) for c in consts],
        out_specs=pl.BlockSpec(memory_space=pl.ANY),
        out_shape=jax.ShapeDtypeStruct((B, S, D), _F32),
        scratch_shapes=[
            pltpu.VMEM((2, T_TILE, B, D), _F32),
            pltpu.SemaphoreType.DMA((2, BATCH_ROWS)),
            pltpu.VMEM((2, T_TILE, B, D), _F32),
            pltpu.SemaphoreType.DMA((2, BATCH_ROWS)),
            pltpu.VMEM((rows, D), _F32),
            pltpu.VMEM((rows // 2, n_state), _F32),
            pltpu.VMEM((rows // 2, n_state), _F32),
            pltpu.VMEM((B, n_state), _F32),
            pltpu.VMEM((B, n_state), _F32),
            pltpu.VMEM((B, d_ssm), _F32),
            pltpu.VMEM((N_META * B, d_pool), _F32),
            pltpu.VMEM((rows, d_ff), _BF16),
        ],
        compiler_params=pltpu.CompilerParams(dimension_semantics=("arbitrary",),
                                             vmem_limit_bytes=VMEM_LIMIT),
        name="block",
    )(x, *consts)


def _block_diag(blocks):
    n, r, c = blocks.shape
    eye = jnp.eye(n, dtype=blocks.dtype)
    return jnp.einsum('grc,gk->grkc', blocks, eye).reshape(n * r, n * c)


def _grouped(blocks, per):
    G = blocks.shape[0]
    return jnp.stack([_block_diag(blocks[k * per:(k + 1) * per]) for k in range(G // per)])


def _s5_params(lam_re, lam_im, log_step, b_re, b_im, c_re, c_im):
    lr = jnp.minimum(lam_re, -1e-4)
    li = lam_im
    step = jnp.exp(log_step)[:, None]
    mag = jnp.exp(lr * step)
    ang = li * step
    abr = mag * jnp.cos(ang)
    abi = mag * jnp.sin(ang)
    nr = abr - 1.0
    ni = abi
    den = lr * lr + li * li
    cr = ((nr * lr + ni * li) / den)[..., None]
    ci = ((ni * lr - nr * li) / den)[..., None]
    bbr = cr * b_re - ci * b_im
    bbi = cr * b_im + ci * b_re
    a_r, a_i = abr[..., None], abi[..., None]
    abbr = a_r * bbr - a_i * bbi
    abbi = a_r * bbi + a_i * bbr
    a2r = abr * abr - abi * abi
    a2i = 2.0 * abr * abi
    car = c_re * abr[:, None, :] - c_im * abi[:, None, :]
    cai = c_re * abi[:, None, :] + c_im * abr[:, None, :]
    loc = jnp.einsum('ghp,gpk->ghk', c_re, bbr) - jnp.einsum('ghp,gpk->ghk', c_im, bbi)
    t = lambda a: jnp.swapaxes(a, 1, 2)
    per = OCT_GROUPS
    wzr = jnp.concatenate([_grouped(t(abbr), per), _grouped(t(bbr), per)], axis=1)
    wzi = jnp.concatenate([_grouped(t(abbi), per), _grouped(t(bbi), per)], axis=1)
    vr = jnp.concatenate([_grouped(t(c_re), per), _grouped(t(car), per)], axis=2)
    vi = jnp.concatenate([_grouped(-t(c_im), per), _grouped(-t(cai), per)], axis=2)
    return a2r, a2i, wzr, wzi, vr, vi, _grouped(t(loc), HALF_GROUPS)


def kernel(x, meta_tokens, norm1_g, w_in, ssm_lambda_re, ssm_lambda_im, ssm_log_step, ssm_b_re, ssm_b_im, ssm_c_re, ssm_c_im, ssm_d, ssm_glu_w, ssm_glu_b, ssm_norm_g, pool_w, pool_scale, pool_norm_g, w_out, norm2_g, w_gate, w_up, w_down, final_norm_g):
    B, S, D = x.shape
    assert B == BATCH_ROWS and norm1_g.shape[0] == 1
    assert S % T_TILE == 0 and T_TILE % 2 == 0 and N_META % 2 == 0
    f32 = _F32
    a2r, a2i, wzr, wzi, vr, vi, loc = _s5_params(
        ssm_lambda_re[0].astype(f32), ssm_lambda_im[0].astype(f32), ssm_log_step[0].astype(f32),
        ssm_b_re[0].astype(f32), ssm_b_im[0].astype(f32), ssm_c_re[0].astype(f32), ssm_c_im[0].astype(f32))
    row = lambda a: a.astype(f32).reshape(1, -1)
    consts = (
        jnp.repeat(meta_tokens.astype(f32), B, axis=0),
        row(norm1_g[0]),
        w_in[0].astype(_BF16),
        row(a2r), row(a2i),
        wzr.astype(_BF16), wzi.astype(_BF16),
        vr.astype(_BF16), vi.astype(_BF16),
        loc.astype(_BF16),
        row(ssm_d[0]),
        _grouped(ssm_glu_w[0].astype(f32), HALF_GROUPS).astype(_BF16),
        row(ssm_glu_b[0]),
        row(ssm_norm_g[0]),
        pool_w[0].astype(_BF16),
        row(pool_scale[0]),
        row(pool_norm_g[0]),
        w_out[0].astype(_BF16),
        row(norm2_g[0]),
        w_gate[0].astype(_BF16),
        w_up[0].astype(_BF16),
        w_down[0].astype(_BF16),
        row(final_norm_g),
    )
    out = _block(x.astype(f32), consts, n_state=a2r.size, d_ssm=ssm_d[0].size,
                 d_pool=pool_scale[0].size, d_ff=w_gate.shape[-1])
    return out.astype(x.dtype)
```

```python
import jax
import jax.numpy as jnp
from jax import lax
from jax.experimental import pallas as pl
from jax.experimental.pallas import tpu as pltpu

N_META = 16
SSM_GROUP = 16
SSM_STATE = 64
POOL_WINDOWS = (2, 4, 8, 16)
EPS = 1e-6

BATCH_ROWS = 8
HALF_GROUPS = 16
OCT_GROUPS = 8
T_TILE = 64
FF_CHUNK = 256
VMEM_LIMIT = 60 * 1024 * 1024

_BF16 = jnp.bfloat16
_F32 = jnp.float32


def _rms(x, g):
    return x * lax.rsqrt(jnp.mean(x * x, axis=-1, keepdims=True) + EPS) * g


def _dot(a, b):
    return jnp.dot(a, b, preferred_element_type=_F32)


def _split_phases(a, n_pairs):
    blk = lambda t: a[t * BATCH_ROWS:(t + 1) * BATCH_ROWS]
    even = jnp.concatenate([blk(2 * k) for k in range(n_pairs)], axis=0)
    odd = jnp.concatenate([blk(2 * k + 1) for k in range(n_pairs)], axis=0)
    return even, odd


def _merge_phases(even, odd, n_pairs):
    blk = lambda a, k: a[k * BATCH_ROWS:(k + 1) * BATCH_ROWS]
    return jnp.concatenate([blk(p, k) for k in range(n_pairs) for p in (even, odd)], axis=0)


def _tile_copies(hbm, vmem, sems, slot, tile, nt, to_vmem):
    copies = []
    for b in range(BATCH_ROWS):
        h = hbm.at[b, pl.ds(tile * nt, nt), :]
        v = vmem.at[slot, :, b, :]
        src, dst = (h, v) if to_vmem else (v, h)
        copies.append(pltpu.make_async_copy(src, dst, sems.at[slot, b]))
    return copies


def _block_kernel(x_hbm, meta_ref, n1g_ref, win_ref, a2r_ref, a2i_ref, wzr_ref, wzi_ref,
                  vr_ref, vi_ref, loc_ref, d_ref, gw_ref, gb_ref, sng_ref, pw_ref, ps_ref, png_ref,
                  wout_ref, n2g_ref, wg_ref, wu_ref, wd_ref, fg_ref, o_hbm,
                  xbuf, xsem, obuf, osem, hbuf, zr, zi, st_r, st_i, u_last, halo, act):
    i = pl.program_id(0)
    n_tiles = pl.num_programs(0) - 1
    slot = lax.rem(i, 2)
    oslot = 1 - slot
    last_tile = n_tiles - 1

    def fetch(tile, s):
        return _tile_copies(x_hbm, xbuf, xsem, s, tile, T_TILE, to_vmem=True)

    def writeback(tile, s):
        return _tile_copies(o_hbm, obuf, osem, s, tile, T_TILE, to_vmem=False)

    d_ssm = d_ref.shape[-1]
    half_in = HALF_GROUPS * SSM_GROUP
    oct_in = OCT_GROUPS * SSM_GROUP
    oct_state = OCT_GROUPS * SSM_STATE
    n_octets = d_ssm // oct_in
    n_halves = d_ssm // half_in
    halo_rows = halo.shape[0]
    n_ff_chunks = wg_ref.shape[-1] // FF_CHUNK

    def project(rows):
        n1 = _rms(rows, n1g_ref[...]).astype(_BF16)
        return _dot(n1, win_ref[...])

    def s5_states(u_prev_odd, u_even, n_pairs, m):
        rows = n_pairs * BATCH_ROWS
        cs = slice(m * oct_state, (m + 1) * oct_state)
        ic = slice(m * oct_in, (m + 1) * oct_in)
        lhs = jnp.concatenate([u_prev_odd[:, ic], u_even[:, ic]], axis=1)
        zr[0:rows, cs] = _dot(lhs, wzr_ref[m])
        zi[0:rows, cs] = _dot(lhs, wzi_ref[m])
        ar = jnp.broadcast_to(a2r_ref[:, cs], (BATCH_ROWS, oct_state))
        ai = jnp.broadcast_to(a2i_ref[:, cs], (BATCH_ROWS, oct_state))
        sr = st_r[:, cs]
        si = st_i[:, cs]
        for k in range(n_pairs):
            rs = slice(k * BATCH_ROWS, (k + 1) * BATCH_ROWS)
            sr, si = (ar * sr - ai * si + zr[rs, cs], ar * si + ai * sr + zi[rs, cs])
            zr[rs, cs] = sr
            zi[rs, cs] = si
        st_r[:, cs] = sr
        st_i[:, cs] = si

    def s5_inputs(u, n_pairs):
        u_even, u_odd = _split_phases(u, n_pairs)
        u_prev_odd = jnp.concatenate([u_last[...], u_odd[:-BATCH_ROWS]], axis=0)
        u_last[...] = u_odd[-BATCH_ROWS:]
        return u_prev_odd.astype(_BF16), u_even.astype(_BF16), u_odd.astype(_BF16)

    def ffn_chunk(n2, c):
        cs = slice(c * FF_CHUNK, (c + 1) * FF_CHUNK)
        gate = _dot(n2, wg_ref[:, cs])
        up = _dot(n2, wu_ref[:, cs])
        act[:, cs] = (gate * jax.nn.sigmoid(gate) * up).astype(_BF16)

    def step(x, h):
        rows = x.shape[0]
        n_pairs = T_TILE // 2
        prows = n_pairs * BATCH_ROWS
        ff = iter(range(n_ff_chunks))

        def ffn_chunks(n):
            for _ in range(n):
                c = next(ff, None)
                if c is not None:
                    ffn_chunk(n2, c)

        n1 = _rms(x, n1g_ref[...]).astype(_BF16)
        n2 = _rms(h, n2g_ref[...]).astype(_BF16)
        proj = _dot(n1, win_ref[...])
        u = proj[:, :d_ssm]
        v = proj[:, d_ssm:]

        u_prev_odd, u_even, u_odd = s5_inputs(u, n_pairs)
        pair_out = []
        for m in range(n_octets):
            s5_states(u_prev_odd, u_even, n_pairs, m)
            ffn_chunks(1)
        for m in range(n_octets):
            cs = slice(m * oct_state, (m + 1) * oct_state)
            pair_out.append(_dot(zr[0:prows, cs].astype(_BF16), vr_ref[m])
                            + _dot(zi[0:prows, cs].astype(_BF16), vi_ref[m]))
            if m % 2 == 1:
                ffn_chunks(1)
        y_even = jnp.concatenate([p[:, :oct_in] for p in pair_out], axis=-1)
        y_odd = jnp.concatenate([p[:, oct_in:] for p in pair_out], axis=-1)
        y_odd = y_odd + jnp.concatenate(
            [_dot(u_odd[:, hh * half_in:(hh + 1) * half_in], loc_ref[hh]) for hh in range(n_halves)], axis=-1)
        y = _merge_phases(y_even, y_odd, n_pairs) + d_ref[...] * u
        g = jax.nn.gelu(y)
        gb = g.astype(_BF16)
        ffn_chunks(1)
        gate = jnp.concatenate([_dot(gb[:, hh * half_in:(hh + 1) * half_in], gw_ref[hh])
                                for hh in range(n_halves)], axis=-1) + gb_ref[...]
        y_ssm = _rms(g * jax.nn.sigmoid(gate), sng_ref[...])
        ffn_chunks(1)

        vext = jnp.concatenate([halo[...], v], axis=0)
        halo[...] = v[rows - halo_rows:, :]
        gd = v.shape[-1] // len(POOL_WINDOWS)
        pks = []
        for k, w in enumerate(POOL_WINDOWS):
            acc = vext[:, k * gd:(k + 1) * gd]
            span = 1
            while span < w:
                shift = span * BATCH_ROWS
                acc = acc[shift:, :] + acc[:-shift, :]
                span *= 2
            pks.append((acc[acc.shape[0] - rows:, :] * (1.0 / w) - v[:, k * gd:(k + 1) * gd]).astype(_BF16))
        yp = jnp.concatenate([_dot(jnp.concatenate(pks[2 * j:2 * j + 2], axis=-1), pw_ref[j])
                              for j in range(len(pks) // 2)], axis=-1)
        ffn_chunks(1)
        y_pool = _rms(yp * ps_ref[...], png_ref[...])
        ffn_chunks(n_ff_chunks)

        mixed = jnp.concatenate([y_ssm, y_pool], axis=-1).astype(_BF16)
        h_next = x + _dot(mixed, wout_ref[...])
        h2 = h + _dot(act[...], wd_ref[...])
        return h_next, _rms(h2, fg_ref[...])

    @pl.when(i == 0)
    def _():
        for c in fetch(0, 0):
            c.start()

    @pl.when(i < n_tiles)
    def _():
        for c in fetch(jnp.minimum(i + 1, last_tile), oslot):
            c.start()

    @pl.when(i >= 3)
    def _():
        for c in writeback(i - 3, oslot):
            c.wait()

    @pl.when(i == 0)
    def _():
        st_r[...] = jnp.zeros_like(st_r)
        st_i[...] = jnp.zeros_like(st_i)
        u_last[...] = jnp.zeros_like(u_last)
        hbuf[...] = jnp.zeros_like(hbuf)
        pm = project(meta_ref[...])
        u_prev_odd, u_even, _ = s5_inputs(pm[:, :d_ssm], N_META // 2)
        for m in range(n_octets):
            s5_states(u_prev_odd, u_even, N_META // 2, m)
        halo[...] = pm[:, d_ssm:]

    for c in fetch(jnp.minimum(i, last_tile), slot):
        c.wait()

    x = xbuf[slot].reshape(T_TILE * BATCH_ROWS, xbuf.shape[-1])
    h_next, out = step(x, hbuf[...])
    hbuf[...] = h_next
    obuf[oslot] = out.reshape(obuf.shape[1:])

    @pl.when(i >= 1)
    def _():
        for c in writeback(i - 1, oslot):
            c.start()

    @pl.when(i == n_tiles)
    def _():
        @pl.when(n_tiles >= 2)
        def _():
            for c in writeback(i - 2, slot):
                c.wait()
        for c in writeback(i - 1, oslot):
            c.wait()


def _const_spec(a):
    nd = a.ndim
    return pl.BlockSpec(a.shape, lambda i: (0,) * nd, pipeline_mode=pl.Buffered(1))


def _block(x, consts, n_state, d_ssm, d_pool, d_ff):
    B, S, D = x.shape
    rows = T_TILE * B
    return pl.pallas_call(
        _block_kernel,
        grid=(S // T_TILE + 1,),
        in_specs=[pl.BlockSpec(memory_space=pl.ANY)] + [_const_spec(c) for c in consts],
        out_specs=pl.BlockSpec(memory_space=pl.ANY),
        out_shape=jax.ShapeDtypeStruct((B, S, D), _F32),
        scratch_shapes=[
            pltpu.VMEM((2, T_TILE, B, D), _F32),
            pltpu.SemaphoreType.DMA((2, BATCH_ROWS)),
            pltpu.VMEM((2, T_TILE, B, D), _F32),
            pltpu.SemaphoreType.DMA((2, BATCH_ROWS)),
            pltpu.VMEM((rows, D), _F32),
            pltpu.VMEM((rows // 2, n_state), _F32),
            pltpu.VMEM((rows // 2, n_state), _F32),
            pltpu.VMEM((B, n_state), _F32),
            pltpu.VMEM((B, n_state), _F32),
            pltpu.VMEM((B, d_ssm), _F32),
            pltpu.VMEM((N_META * B, d_pool), _F32),
            pltpu.VMEM((rows, d_ff), _BF16),
        ],
        compiler_params=pltpu.CompilerParams(dimension_semantics=("arbitrary",),
                                             vmem_limit_bytes=VMEM_LIMIT),
        name="block",
    )(x, *consts)


def _grouped(blocks, per, phase_axis=None):
    if phase_axis is None:
        blocks = blocks[None]
    nph, G, r, c = blocks.shape
    m = G // per
    eye = jnp.eye(per, dtype=blocks.dtype)
    b = blocks.reshape(nph, m, per, r, c)
    if phase_axis == 2:
        out = jnp.transpose(b, (1, 2, 3, 0, 4))[:, :, :, :, None, :] * eye[None, :, None, None, :, None]
        return out.reshape(m, per * r, nph * per * c)
    out = jnp.transpose(b, (1, 0, 2, 3, 4))[:, :, :, :, None, :] * eye[None, None, :, None, :, None]
    return out.reshape(m, nph * per * r, per * c)


def _s5_params(lam_re, lam_im, log_step, b_re, b_im, c_re, c_im):
    lr = jnp.minimum(lam_re, -1e-4)
    li = lam_im
    step = jnp.exp(log_step)[:, None]
    mag = jnp.exp(lr * step)
    ang = li * step
    abr = mag * jnp.cos(ang)
    abi = mag * jnp.sin(ang)
    nr = abr - 1.0
    ni = abi
    den = lr * lr + li * li
    cr = ((nr * lr + ni * li) / den)[..., None]
    ci = ((ni * lr - nr * li) / den)[..., None]
    bbr = cr * b_re - ci * b_im
    bbi = cr * b_im + ci * b_re
    a_r, a_i = abr[..., None], abi[..., None]
    abbr = a_r * bbr - a_i * bbi
    abbi = a_r * bbi + a_i * bbr
    a2r = abr * abr - abi * abi
    a2i = 2.0 * abr * abi
    car = c_re * abr[:, None, :] - c_im * abi[:, None, :]
    cai = c_re * abi[:, None, :] + c_im * abr[:, None, :]
    loc = jnp.einsum('ghp,gpk->ghk', c_re, bbr) - jnp.einsum('ghp,gpk->ghk', c_im, bbi)
    t = lambda a: jnp.swapaxes(a, 1, 2)
    per = OCT_GROUPS
    wzr = _grouped(jnp.stack([t(abbr), t(bbr)]), per, phase_axis=1)
    wzi = _grouped(jnp.stack([t(abbi), t(bbi)]), per, phase_axis=1)
    vr = _grouped(jnp.stack([t(c_re), t(car)]), per, phase_axis=2)
    vi = _grouped(jnp.stack([-t(c_im), -t(cai)]), per, phase_axis=2)
    return a2r, a2i, wzr, wzi, vr, vi, _grouped(t(loc), HALF_GROUPS)


def kernel(x, meta_tokens, norm1_g, w_in, ssm_lambda_re, ssm_lambda_im, ssm_log_step, ssm_b_re, ssm_b_im, ssm_c_re, ssm_c_im, ssm_d, ssm_glu_w, ssm_glu_b, ssm_norm_g, pool_w, pool_scale, pool_norm_g, w_out, norm2_g, w_gate, w_up, w_down, final_norm_g):
    B, S, D = x.shape
    assert B == BATCH_ROWS and norm1_g.shape[0] == 1
    assert S % T_TILE == 0 and T_TILE % 2 == 0 and N_META % 2 == 0
    f32 = _F32
    a2r, a2i, wzr, wzi, vr, vi, loc = _s5_params(
        ssm_lambda_re[0].astype(f32), ssm_lambda_im[0].astype(f32), ssm_log_step[0].astype(f32),
        ssm_b_re[0].astype(f32), ssm_b_im[0].astype(f32), ssm_c_re[0].astype(f32), ssm_c_im[0].astype(f32))
    row = lambda a: a.astype(f32).reshape(1, -1)
    consts = (
        jnp.repeat(meta_tokens.astype(f32), B, axis=0),
        row(norm1_g[0]),
        w_in[0].astype(_BF16),
        row(a2r), row(a2i),
        wzr.astype(_BF16), wzi.astype(_BF16),
        vr.astype(_BF16), vi.astype(_BF16),
        loc.astype(_BF16),
        row(ssm_d[0]),
        _grouped(ssm_glu_w[0].astype(f32), HALF_GROUPS).astype(_BF16),
        row(ssm_glu_b[0]),
        row(ssm_norm_g[0]),
        _grouped(pool_w[0].astype(f32), 2).astype(_BF16),
        row(pool_scale[0]),
        row(pool_norm_g[0]),
        w_out[0].astype(_BF16),
        row(norm2_g[0]),
        w_gate[0].astype(_BF16),
        w_up[0].astype(_BF16),
        w_down[0].astype(_BF16),
        row(final_norm_g),
    )
    out = _block(x.astype(f32), consts, n_state=a2r.size, d_ssm=ssm_d[0].size,
                 d_pool=pool_scale[0].size, d_ff=w_gate.shape[-1])
    return out.astype(x.dtype)
```

```python
import jax
import jax.numpy as jnp
from jax import lax
from jax.experimental import pallas as pl
from jax.experimental.pallas import tpu as pltpu

N_META = 16
SSM_GROUP = 16
SSM_STATE = 64
POOL_WINDOWS = (2, 4, 8, 16)
EPS = 1e-6

BATCH_ROWS = 8
HALF_GROUPS = 16
OCT_GROUPS = 8
T_TILE = 64
FF_CHUNK = 256
WIDE_STAGE_ROWS = 64
NARROW_STAGE_ROWS = 256
VMEM_LIMIT = 60 * 1024 * 1024

_BF16 = jnp.bfloat16
_F32 = jnp.float32


def _rms(x, g):
    return x * lax.rsqrt(jnp.mean(x * x, axis=-1, keepdims=True) + EPS) * g


def _dot(a, b):
    return jnp.dot(a, b, preferred_element_type=_F32)


def _split_phases(a, n_pairs):
    blk = lambda t: a[t * BATCH_ROWS:(t + 1) * BATCH_ROWS]
    even = jnp.concatenate([blk(2 * k) for k in range(n_pairs)], axis=0)
    odd = jnp.concatenate([blk(2 * k + 1) for k in range(n_pairs)], axis=0)
    return even, odd


def _merge_phases(even, odd, n_pairs):
    blk = lambda a, k: a[k * BATCH_ROWS:(k + 1) * BATCH_ROWS]
    return jnp.concatenate([blk(p, k) for k in range(n_pairs) for p in (even, odd)], axis=0)


def _tile_copies(hbm, vmem, sems, slot, tile, nt, to_vmem):
    copies = []
    for b in range(BATCH_ROWS):
        h = hbm.at[b, pl.ds(tile * nt, nt), :]
        v = vmem.at[slot, :, b, :]
        src, dst = (h, v) if to_vmem else (v, h)
        copies.append(pltpu.make_async_copy(src, dst, sems.at[slot, b]))
    return copies


def _load_cast(w_hbm, dst, stage, sems, chunk):
    n = dst.shape[0] // chunk

    def copy(c, s):
        return pltpu.make_async_copy(w_hbm.at[0, pl.ds(c * chunk, chunk), :],
                                     stage.at[pl.ds(s * chunk, chunk), :], sems.at[s])

    copy(0, 0).start()

    def body(c, carry):
        s = lax.rem(c, 2)

        @pl.when(c + 1 < n)
        def _():
            copy(c + 1, 1 - s).start()

        copy(c, s).wait()
        src = pl.multiple_of(s * chunk, chunk)
        row = pl.multiple_of(c * chunk, chunk)
        dst[pl.ds(row, chunk), :] = stage[pl.ds(src, chunk), :].astype(dst.dtype)
        return carry

    lax.fori_loop(0, n, body, 0)


def _expand_block_diag(src, dst, block_rows, block_lanes):
    n = dst.shape[0] // block_rows
    lane_block = lax.broadcasted_iota(jnp.int32, src.shape, 1) // block_lanes
    for g in range(n):
        dst[g * block_rows:(g + 1) * block_rows, :] = jnp.where(lane_block == g, src, 0.0).astype(dst.dtype)


def _block_kernel(x_hbm, meta_ref, n1g_ref, a2r_ref, a2i_ref, zc_ref, vc_ref, locc_ref, d_ref, gluc_ref,
                  gb_ref, sng_ref, pw_ref, ps_ref, png_ref, n2g_ref, fg_ref,
                  win_hbm, wout_hbm, wg_hbm, wu_hbm, wd_hbm, o_hbm,
                  xbuf, xsem, obuf, osem, hbuf, zr, zi, st_r, st_i, u_last, halo, act,
                  win_ref, wout_ref, wg_ref, wu_ref, wd_ref, wz_ref, v_ref, loc_ref, gw_ref,
                  wide_stage, wsem):
    i = pl.program_id(0)
    n_tiles = pl.num_programs(0) - 1
    slot = lax.rem(i, 2)
    oslot = 1 - slot
    last_tile = n_tiles - 1

    def fetch(tile, s):
        return _tile_copies(x_hbm, xbuf, xsem, s, tile, T_TILE, to_vmem=True)

    def writeback(tile, s):
        return _tile_copies(o_hbm, obuf, osem, s, tile, T_TILE, to_vmem=False)

    d_ssm = d_ref.shape[-1]
    half_in = HALF_GROUPS * SSM_GROUP
    oct_in = OCT_GROUPS * SSM_GROUP
    oct_state = OCT_GROUPS * SSM_STATE
    n_octets = d_ssm // oct_in
    n_halves = d_ssm // half_in
    halo_rows = halo.shape[0]
    n_ff_chunks = wg_ref.shape[-1] // FF_CHUNK

    def build_weights():
        _load_cast(win_hbm, win_ref, hbuf, wsem, NARROW_STAGE_ROWS)
        _load_cast(wout_hbm, wout_ref, hbuf, wsem, NARROW_STAGE_ROWS)
        _load_cast(wd_hbm, wd_ref, hbuf, wsem, NARROW_STAGE_ROWS)
        _load_cast(wg_hbm, wg_ref, wide_stage, wsem, WIDE_STAGE_ROWS)
        _load_cast(wu_hbm, wu_ref, wide_stage, wsem, WIDE_STAGE_ROWS)
        for ri in range(2):
            for m in range(n_octets):
                for ph in range(2):
                    _expand_block_diag(zc_ref[ri, ph, m],
                                       wz_ref.at[ri, m, ph * oct_in:(ph + 1) * oct_in, :],
                                       SSM_GROUP, SSM_STATE)
                    _expand_block_diag(vc_ref[ri, ph, m],
                                       v_ref.at[ri, m, :, ph * oct_in:(ph + 1) * oct_in],
                                       SSM_STATE, SSM_GROUP)
        for hh in range(n_halves):
            _expand_block_diag(locc_ref[hh], loc_ref.at[hh], SSM_GROUP, SSM_GROUP)
            _expand_block_diag(gluc_ref[hh], gw_ref.at[hh], SSM_GROUP, SSM_GROUP)

    def project(rows):
        n1 = _rms(rows, n1g_ref[...]).astype(_BF16)
        return _dot(n1, win_ref[...])

    def s5_states(u_prev_odd, u_even, n_pairs, m):
        rows = n_pairs * BATCH_ROWS
        cs = slice(m * oct_state, (m + 1) * oct_state)
        ic = slice(m * oct_in, (m + 1) * oct_in)
        lhs = jnp.concatenate([u_prev_odd[:, ic], u_even[:, ic]], axis=1)
        zr[0:rows, cs] = _dot(lhs, wz_ref[0, m])
        zi[0:rows, cs] = _dot(lhs, wz_ref[1, m])
        ar = jnp.broadcast_to(a2r_ref[:, cs], (BATCH_ROWS, oct_state))
        ai = jnp.broadcast_to(a2i_ref[:, cs], (BATCH_ROWS, oct_state))
        sr = st_r[:, cs]
        si = st_i[:, cs]
        for k in range(n_pairs):
            rs = slice(k * BATCH_ROWS, (k + 1) * BATCH_ROWS)
            sr, si = (ar * sr - ai * si + zr[rs, cs], ar * si + ai * sr + zi[rs, cs])
            zr[rs, cs] = sr
            zi[rs, cs] = si
        st_r[:, cs] = sr
        st_i[:, cs] = si

    def s5_inputs(u, n_pairs):
        u_even, u_odd = _split_phases(u, n_pairs)
        u_prev_odd = jnp.concatenate([u_last[...], u_odd[:-BATCH_ROWS]], axis=0)
        u_last[...] = u_odd[-BATCH_ROWS:]
        return u_prev_odd.astype(_BF16), u_even.astype(_BF16), u_odd.astype(_BF16)

    def ffn_chunk(n2, c):
        cs = slice(c * FF_CHUNK, (c + 1) * FF_CHUNK)
        gate = _dot(n2, wg_ref[:, cs])
        up = _dot(n2, wu_ref[:, cs])
        act[:, cs] = (gate * jax.nn.sigmoid(gate) * up).astype(_BF16)

    def step(x, h):
        rows = x.shape[0]
        n_pairs = T_TILE // 2
        prows = n_pairs * BATCH_ROWS
        ff = iter(range(n_ff_chunks))

        def ffn_chunks(n):
            for _ in range(n):
                c = next(ff, None)
                if c is not None:
                    ffn_chunk(n2, c)

        n1 = _rms(x, n1g_ref[...]).astype(_BF16)
        n2 = _rms(h, n2g_ref[...]).astype(_BF16)
        proj = _dot(n1, win_ref[...])
        u = proj[:, :d_ssm]
        v = proj[:, d_ssm:]

        u_prev_odd, u_even, u_odd = s5_inputs(u, n_pairs)
        pair_out = []
        for m in range(n_octets):
            s5_states(u_prev_odd, u_even, n_pairs, m)
            ffn_chunks(1)
        for m in range(n_octets):
            cs = slice(m * oct_state, (m + 1) * oct_state)
            pair_out.append(_dot(zr[0:prows, cs].astype(_BF16), v_ref[0, m])
                            + _dot(zi[0:prows, cs].astype(_BF16), v_ref[1, m]))
            if m % 2 == 1:
                ffn_chunks(1)
        y_even = jnp.concatenate([p[:, :oct_in] for p in pair_out], axis=-1)
        y_odd = jnp.concatenate([p[:, oct_in:] for p in pair_out], axis=-1)
        y_odd = y_odd + jnp.concatenate(
            [_dot(u_odd[:, hh * half_in:(hh + 1) * half_in], loc_ref[hh]) for hh in range(n_halves)], axis=-1)
        y = _merge_phases(y_even, y_odd, n_pairs) + d_ref[...] * u
        g = jax.nn.gelu(y)
        gb = g.astype(_BF16)
        ffn_chunks(1)
        gate = jnp.concatenate([_dot(gb[:, hh * half_in:(hh + 1) * half_in], gw_ref[hh])
                                for hh in range(n_halves)], axis=-1) + gb_ref[...]
        y_ssm = _rms(g * jax.nn.sigmoid(gate), sng_ref[...])
        ffn_chunks(1)

        vext = jnp.concatenate([halo[...], v], axis=0)
        halo[...] = v[rows - halo_rows:, :]
        gd = v.shape[-1] // len(POOL_WINDOWS)
        yps = []
        for k, w in enumerate(POOL_WINDOWS):
            acc = vext[:, k * gd:(k + 1) * gd]
            span = 1
            while span < w:
                shift = span * BATCH_ROWS
                acc = acc[shift:, :] + acc[:-shift, :]
                span *= 2
            pk = acc[acc.shape[0] - rows:, :] * (1.0 / w) - v[:, k * gd:(k + 1) * gd]
            yps.append(_dot(pk.astype(_BF16), pw_ref[k]))
        ffn_chunks(1)
        y_pool = _rms(jnp.concatenate(yps, axis=-1) * ps_ref[...], png_ref[...])
        ffn_chunks(n_ff_chunks)

        mixed = jnp.concatenate([y_ssm, y_pool], axis=-1).astype(_BF16)
        h_next = x + _dot(mixed, wout_ref[...])
        h2 = h + _dot(act[...], wd_ref[...])
        return h_next, _rms(h2, fg_ref[...])

    @pl.when(i == 0)
    def _():
        for c in fetch(0, 0):
            c.start()

    @pl.when(i < n_tiles)
    def _():
        for c in fetch(jnp.minimum(i + 1, last_tile), oslot):
            c.start()

    @pl.when(i >= 3)
    def _():
        for c in writeback(i - 3, oslot):
            c.wait()

    @pl.when(i == 0)
    def _():
        build_weights()
        st_r[...] = jnp.zeros_like(st_r)
        st_i[...] = jnp.zeros_like(st_i)
        u_last[...] = jnp.zeros_like(u_last)
        hbuf[...] = jnp.zeros_like(hbuf)
        pm = project(meta_ref[...])
        u_prev_odd, u_even, _ = s5_inputs(pm[:, :d_ssm], N_META // 2)
        for m in range(n_octets):
            s5_states(u_prev_odd, u_even, N_META // 2, m)
        halo[...] = pm[:, d_ssm:]

    for c in fetch(jnp.minimum(i, last_tile), slot):
        c.wait()

    x = xbuf[slot].reshape(T_TILE * BATCH_ROWS, xbuf.shape[-1])
    h_next, out = step(x, hbuf[...])
    hbuf[...] = h_next
    obuf[oslot] = out.reshape(obuf.shape[1:])

    @pl.when(i >= 1)
    def _():
        for c in writeback(i - 1, oslot):
            c.start()

    @pl.when(i == n_tiles)
    def _():
        @pl.when(n_tiles >= 2)
        def _():
            for c in writeback(i - 2, slot):
                c.wait()
        for c in writeback(i - 1, oslot):
            c.wait()


def _const_spec(a):
    nd = a.ndim
    return pl.BlockSpec(a.shape, lambda i: (0,) * nd, pipeline_mode=pl.Buffered(1))


def _block(x, consts, hbm_weights, n_state, d_ssm, d_pool, d_ff):
    B, S, D = x.shape
    rows = T_TILE * B
    assert rows == 2 * NARROW_STAGE_ROWS
    any_spec = pl.BlockSpec(memory_space=pl.ANY)
    n_oct = d_ssm // (OCT_GROUPS * SSM_GROUP)
    n_half = d_ssm // (HALF_GROUPS * SSM_GROUP)
    return pl.pallas_call(
        _block_kernel,
        grid=(S // T_TILE + 1,),
        in_specs=[any_spec] + [_const_spec(c) for c in consts] + [any_spec] * len(hbm_weights),
        out_specs=any_spec,
        out_shape=jax.ShapeDtypeStruct((B, S, D), _F32),
        scratch_shapes=[
            pltpu.VMEM((2, T_TILE, B, D), _F32),
            pltpu.SemaphoreType.DMA((2, BATCH_ROWS)),
            pltpu.VMEM((2, T_TILE, B, D), _F32),
            pltpu.SemaphoreType.DMA((2, BATCH_ROWS)),
            pltpu.VMEM((rows, D), _F32),
            pltpu.VMEM((rows // 2, n_state), _F32),
            pltpu.VMEM((rows // 2, n_state), _F32),
            pltpu.VMEM((B, n_state), _F32),
            pltpu.VMEM((B, n_state), _F32),
            pltpu.VMEM((B, d_ssm), _F32),
            pltpu.VMEM((N_META * B, d_pool), _F32),
            pltpu.VMEM((rows, d_ff), _BF16),
            pltpu.VMEM((D, D), _BF16),
            pltpu.VMEM((D, D), _BF16),
            pltpu.VMEM((D, d_ff), _BF16),
            pltpu.VMEM((D, d_ff), _BF16),
            pltpu.VMEM((d_ff, D), _BF16),
            pltpu.VMEM((2, n_oct, 2 * OCT_GROUPS * SSM_GROUP, OCT_GROUPS * SSM_STATE), _BF16),
            pltpu.VMEM((2, n_oct, OCT_GROUPS * SSM_STATE, 2 * OCT_GROUPS * SSM_GROUP), _BF16),
            pltpu.VMEM((n_half, HALF_GROUPS * SSM_GROUP, HALF_GROUPS * SSM_GROUP), _BF16),
            pltpu.VMEM((n_half, HALF_GROUPS * SSM_GROUP, HALF_GROUPS * SSM_GROUP), _BF16),
            pltpu.VMEM((2 * WIDE_STAGE_ROWS, d_ff), _F32),
            pltpu.SemaphoreType.DMA((2,)),
        ],
        compiler_params=pltpu.CompilerParams(dimension_semantics=("arbitrary",),
                                             vmem_limit_bytes=VMEM_LIMIT),
        name="block",
    )(x, *consts, *hbm_weights)


def _lanes_by_group(blocks, per):
    G, r, c = blocks.shape
    return jnp.transpose(blocks.reshape(G // per, per, r, c), (0, 2, 1, 3)).reshape(G // per, r, per * c)


def _s5_params(lam_re, lam_im, log_step, b_re, b_im, c_re, c_im):
    lr = jnp.minimum(lam_re, -1e-4)
    li = lam_im
    step = jnp.exp(log_step)[:, None]
    mag = jnp.exp(lr * step)
    ang = li * step
    abr = mag * jnp.cos(ang)
    abi = mag * jnp.sin(ang)
    nr = abr - 1.0
    ni = abi
    den = lr * lr + li * li
    cr = ((nr * lr + ni * li) / den)[..., None]
    ci = ((ni * lr - nr * li) / den)[..., None]
    bbr = cr * b_re - ci * b_im
    bbi = cr * b_im + ci * b_re
    a_r, a_i = abr[..., None], abi[..., None]
    abbr = a_r * bbr - a_i * bbi
    abbi = a_r * bbi + a_i * bbr
    a2r = abr * abr - abi * abi
    a2i = 2.0 * abr * abi
    car = c_re * abr[:, None, :] - c_im * abi[:, None, :]
    cai = c_re * abi[:, None, :] + c_im * abr[:, None, :]
    loc = jnp.einsum('ghp,gpk->ghk', c_re, bbr) - jnp.einsum('ghp,gpk->ghk', c_im, bbi)
    t = lambda a: jnp.swapaxes(a, 1, 2)
    by_oct = lambda a: _lanes_by_group(a, OCT_GROUPS)
    zc = jnp.stack([jnp.stack([by_oct(t(abbr)), by_oct(t(bbr))]),
                    jnp.stack([by_oct(t(abbi)), by_oct(t(bbi))])])
    vc = jnp.stack([jnp.stack([by_oct(t(c_re)), by_oct(t(car))]),
                    jnp.stack([by_oct(-t(c_im)), by_oct(-t(cai))])])
    return a2r, a2i, zc, vc, _lanes_by_group(t(loc), HALF_GROUPS)


def kernel(x, meta_tokens, norm1_g, w_in, ssm_lambda_re, ssm_lambda_im, ssm_log_step, ssm_b_re, ssm_b_im, ssm_c_re, ssm_c_im, ssm_d, ssm_glu_w, ssm_glu_b, ssm_norm_g, pool_w, pool_scale, pool_norm_g, w_out, norm2_g, w_gate, w_up, w_down, final_norm_g):
    B, S, D = x.shape
    assert B == BATCH_ROWS and norm1_g.shape[0] == 1
    assert S % T_TILE == 0 and T_TILE % 2 == 0 and N_META % 2 == 0
    f32 = _F32
    a2r, a2i, zc, vc, locc = _s5_params(
        ssm_lambda_re[0].astype(f32), ssm_lambda_im[0].astype(f32), ssm_log_step[0].astype(f32),
        ssm_b_re[0].astype(f32), ssm_b_im[0].astype(f32), ssm_c_re[0].astype(f32), ssm_c_im[0].astype(f32))
    row = lambda a: a.astype(f32).reshape(1, -1)
    consts = (
        jnp.repeat(meta_tokens.astype(f32), B, axis=0),
        row(norm1_g[0]),
        row(a2r), row(a2i),
        zc,
        vc,
        locc,
        row(ssm_d[0]),
        _lanes_by_group(ssm_glu_w[0].astype(f32), HALF_GROUPS),
        row(ssm_glu_b[0]),
        row(ssm_norm_g[0]),
        pool_w[0].astype(_BF16),
        row(pool_scale[0]),
        row(pool_norm_g[0]),
        row(norm2_g[0]),
        row(final_norm_g),
    )
    hbm_weights = tuple(w.astype(f32) for w in (w_in, w_out, w_gate, w_up, w_down))
    out = _block(x.astype(f32), consts, hbm_weights, n_state=a2r.size, d_ssm=ssm_d[0].size,
                 d_pool=pool_scale[0].size, d_ff=w_gate.shape[-1])
    return out.astype(x.dtype)
```

```python
import jax
import jax.numpy as jnp
from jax import lax
from jax.experimental import pallas as pl
from jax.experimental.pallas import tpu as pltpu

N_META = 16
SSM_GROUP = 16
SSM_STATE = 64
POOL_WINDOWS = (2, 4, 8, 16)
EPS = 1e-6

BATCH_ROWS = 8
HALF_GROUPS = 16
OCT_GROUPS = 8
T_TILE = 64
FF_CHUNK = 256
STAGE_SLOTS = 4
WIDE_STAGE_ROWS = 64
NARROW_STAGE_ROWS = 128
VMEM_LIMIT = 60 * 1024 * 1024

_BF16 = jnp.bfloat16
_F32 = jnp.float32


def _rms(x, g):
    return x * lax.rsqrt(jnp.mean(x * x, axis=-1, keepdims=True) + EPS) * g


def _dot(a, b):
    return jnp.dot(a, b, preferred_element_type=_F32)


def _split_phases(a, n_pairs):
    blk = lambda t: a[t * BATCH_ROWS:(t + 1) * BATCH_ROWS]
    even = jnp.concatenate([blk(2 * k) for k in range(n_pairs)], axis=0)
    odd = jnp.concatenate([blk(2 * k + 1) for k in range(n_pairs)], axis=0)
    return even, odd


def _merge_phases(even, odd, n_pairs):
    blk = lambda a, k: a[k * BATCH_ROWS:(k + 1) * BATCH_ROWS]
    return jnp.concatenate([blk(p, k) for k in range(n_pairs) for p in (even, odd)], axis=0)


def _tile_copies(hbm, vmem, sems, slot, tile, nt, to_vmem):
    copies = []
    for b in range(BATCH_ROWS):
        h = hbm.at[b, pl.ds(tile * nt, nt), :]
        v = vmem.at[slot, :, b, :]
        src, dst = (h, v) if to_vmem else (v, h)
        copies.append(pltpu.make_async_copy(src, dst, sems.at[slot, b]))
    return copies


def _load_cast(w_hbm, dst, stage, sems, chunk):
    n = dst.shape[0] // chunk
    ahead = STAGE_SLOTS - 1

    def copy(c, s):
        return pltpu.make_async_copy(w_hbm.at[0, pl.ds(c * chunk, chunk), :],
                                     stage.at[pl.ds(s * chunk, chunk), :], sems.at[s])

    for c in range(min(ahead, n)):
        copy(c, c).start()

    def body(c, carry):
        s = lax.rem(c, STAGE_SLOTS)

        @pl.when(c + ahead < n)
        def _():
            copy(c + ahead, lax.rem(c + ahead, STAGE_SLOTS)).start()

        copy(c, s).wait()
        src = pl.multiple_of(s * chunk, chunk)
        row = pl.multiple_of(c * chunk, chunk)
        dst[pl.ds(row, chunk), :] = stage[pl.ds(src, chunk), :].astype(dst.dtype)
        return carry

    lax.fori_loop(0, n, body, 0)


def _expand_block_diag(src, dst, block_rows, block_lanes):
    n = dst.shape[0] // block_rows
    lane_block = lax.broadcasted_iota(jnp.int32, src.shape, 1) // block_lanes
    for g in range(n):
        dst[g * block_rows:(g + 1) * block_rows, :] = jnp.where(lane_block == g, src, 0.0).astype(dst.dtype)


def _block_kernel(x_hbm, meta_ref, n1g_ref, a2r_ref, a2i_ref, zc_ref, vc_ref, locc_ref, d_ref, gluc_ref,
                  gb_ref, sng_ref, pw_ref, ps_ref, png_ref, n2g_ref, fg_ref,
                  win_hbm, wout_hbm, wg_hbm, wu_hbm, wd_hbm, o_hbm,
                  xbuf, xsem, obuf, osem, hbuf, zr, zi, st_r, st_i, u_last, halo, act,
                  win_ref, wout_ref, wg_ref, wu_ref, wd_ref, wz_ref, v_ref, loc_ref, gw_ref,
                  wide_stage, wsem):
    i = pl.program_id(0)
    n_tiles = pl.num_programs(0) - 1
    slot = lax.rem(i, 2)
    oslot = 1 - slot
    last_tile = n_tiles - 1

    def fetch(tile, s):
        return _tile_copies(x_hbm, xbuf, xsem, s, tile, T_TILE, to_vmem=True)

    def writeback(tile, s):
        return _tile_copies(o_hbm, obuf, osem, s, tile, T_TILE, to_vmem=False)

    d_ssm = d_ref.shape[-1]
    half_in = HALF_GROUPS * SSM_GROUP
    oct_in = OCT_GROUPS * SSM_GROUP
    oct_state = OCT_GROUPS * SSM_STATE
    n_octets = d_ssm // oct_in
    n_halves = d_ssm // half_in
    halo_rows = halo.shape[0]
    n_ff_chunks = wg_ref.shape[-1] // FF_CHUNK

    def build_weights():
        _load_cast(win_hbm, win_ref, hbuf, wsem, NARROW_STAGE_ROWS)
        _load_cast(wout_hbm, wout_ref, hbuf, wsem, NARROW_STAGE_ROWS)
        _load_cast(wd_hbm, wd_ref, hbuf, wsem, NARROW_STAGE_ROWS)
        _load_cast(wg_hbm, wg_ref, wide_stage, wsem, WIDE_STAGE_ROWS)
        _load_cast(wu_hbm, wu_ref, wide_stage, wsem, WIDE_STAGE_ROWS)
        for ri in range(2):
            for m in range(n_octets):
                for ph in range(2):
                    _expand_block_diag(zc_ref[ri, ph, m],
                                       wz_ref.at[ri, m, ph * oct_in:(ph + 1) * oct_in, :],
                                       SSM_GROUP, SSM_STATE)
                    _expand_block_diag(vc_ref[ri, ph, m],
                                       v_ref.at[ri, m, :, ph * oct_in:(ph + 1) * oct_in],
                                       SSM_STATE, SSM_GROUP)
        for hh in range(n_halves):
            _expand_block_diag(locc_ref[hh], loc_ref.at[hh], SSM_GROUP, SSM_GROUP)
            _expand_block_diag(gluc_ref[hh], gw_ref.at[hh], SSM_GROUP, SSM_GROUP)

    def project(rows):
        n1 = _rms(rows, n1g_ref[...]).astype(_BF16)
        return _dot(n1, win_ref[...])

    def s5_states(u_prev_odd, u_even, n_pairs, m):
        rows = n_pairs * BATCH_ROWS
        cs = slice(m * oct_state, (m + 1) * oct_state)
        ic = slice(m * oct_in, (m + 1) * oct_in)
        lhs = jnp.concatenate([u_prev_odd[:, ic], u_even[:, ic]], axis=1)
        zr[0:rows, cs] = _dot(lhs, wz_ref[0, m])
        zi[0:rows, cs] = _dot(lhs, wz_ref[1, m])
        ar = jnp.broadcast_to(a2r_ref[:, cs], (BATCH_ROWS, oct_state))
        ai = jnp.broadcast_to(a2i_ref[:, cs], (BATCH_ROWS, oct_state))
        sr = st_r[:, cs]
        si = st_i[:, cs]
        for k in range(n_pairs):
            rs = slice(k * BATCH_ROWS, (k + 1) * BATCH_ROWS)
            sr, si = (ar * sr - ai * si + zr[rs, cs], ar * si + ai * sr + zi[rs, cs])
            zr[rs, cs] = sr
            zi[rs, cs] = si
        st_r[:, cs] = sr
        st_i[:, cs] = si

    def s5_inputs(u, n_pairs):
        u_even, u_odd = _split_phases(u, n_pairs)
        u_prev_odd = jnp.concatenate([u_last[...], u_odd[:-BATCH_ROWS]], axis=0)
        u_last[...] = u_odd[-BATCH_ROWS:]
        return u_prev_odd.astype(_BF16), u_even.astype(_BF16), u_odd.astype(_BF16)

    def ffn_chunk(n2, c):
        cs = slice(c * FF_CHUNK, (c + 1) * FF_CHUNK)
        gate = _dot(n2, wg_ref[:, cs])
        up = _dot(n2, wu_ref[:, cs])
        act[:, cs] = (gate * jax.nn.sigmoid(gate) * up).astype(_BF16)

    def step(x, h):
        rows = x.shape[0]
        n_pairs = T_TILE // 2
        prows = n_pairs * BATCH_ROWS
        ff = iter(range(n_ff_chunks))

        def ffn_chunks(n):
            for _ in range(n):
                c = next(ff, None)
                if c is not None:
                    ffn_chunk(n2, c)

        n1 = _rms(x, n1g_ref[...]).astype(_BF16)
        proj = _dot(n1, win_ref[...])
        n2 = _rms(h, n2g_ref[...]).astype(_BF16)
        u = proj[:, :d_ssm]
        v = proj[:, d_ssm:]

        u_prev_odd, u_even, u_odd = s5_inputs(u, n_pairs)
        pair_out = []
        for m in range(n_octets):
            s5_states(u_prev_odd, u_even, n_pairs, m)
            ffn_chunks(1)
        for m in range(n_octets):
            cs = slice(m * oct_state, (m + 1) * oct_state)
            pair_out.append(_dot(zr[0:prows, cs].astype(_BF16), v_ref[0, m])
                            + _dot(zi[0:prows, cs].astype(_BF16), v_ref[1, m]))
            if m % 2 == 1:
                ffn_chunks(1)
        y_even = jnp.concatenate([p[:, :oct_in] for p in pair_out], axis=-1)
        y_odd = jnp.concatenate([p[:, oct_in:] for p in pair_out], axis=-1)
        y_odd = y_odd + jnp.concatenate(
            [_dot(u_odd[:, hh * half_in:(hh + 1) * half_in], loc_ref[hh]) for hh in range(n_halves)], axis=-1)
        y = _merge_phases(y_even, y_odd, n_pairs) + d_ref[...] * u
        g = jax.nn.gelu(y)
        gb = g.astype(_BF16)
        ffn_chunks(1)
        gate = jnp.concatenate([_dot(gb[:, hh * half_in:(hh + 1) * half_in], gw_ref[hh])
                                for hh in range(n_halves)], axis=-1) + gb_ref[...]
        y_ssm = _rms(g * jax.nn.sigmoid(gate), sng_ref[...])
        ffn_chunks(1)

        vext = jnp.concatenate([halo[...], v], axis=0)
        halo[...] = v[rows - halo_rows:, :]
        gd = v.shape[-1] // len(POOL_WINDOWS)
        yps = []
        for k, w in enumerate(POOL_WINDOWS):
            acc = vext[:, k * gd:(k + 1) * gd]
            span = 1
            while span < w:
                shift = span * BATCH_ROWS
                acc = acc[shift:, :] + acc[:-shift, :]
                span *= 2
            pk = acc[acc.shape[0] - rows:, :] * (1.0 / w) - v[:, k * gd:(k + 1) * gd]
            yps.append(_dot(pk.astype(_BF16), pw_ref[k]))
        ffn_chunks(1)
        y_pool = _rms(jnp.concatenate(yps, axis=-1) * ps_ref[...], png_ref[...])
        ffn_chunks(n_ff_chunks)

        mixed = jnp.concatenate([y_ssm, y_pool], axis=-1).astype(_BF16)
        h2 = h + _dot(act[...], wd_ref[...])
        out = _rms(h2, fg_ref[...])
        h_next = x + _dot(mixed, wout_ref[...])
        return h_next, out

    @pl.when(i == 0)
    def _():
        for c in fetch(0, 0):
            c.start()

    @pl.when(i < n_tiles)
    def _():
        for c in fetch(jnp.minimum(i + 1, last_tile), oslot):
            c.start()

    @pl.when(i >= 3)
    def _():
        for c in writeback(i - 3, oslot):
            c.wait()

    @pl.when(i == 0)
    def _():
        build_weights()
        st_r[...] = jnp.zeros_like(st_r)
        st_i[...] = jnp.zeros_like(st_i)
        u_last[...] = jnp.zeros_like(u_last)
        hbuf[...] = jnp.zeros_like(hbuf)
        pm = project(meta_ref[...])
        u_prev_odd, u_even, _ = s5_inputs(pm[:, :d_ssm], N_META // 2)
        for m in range(n_octets):
            s5_states(u_prev_odd, u_even, N_META // 2, m)
        halo[...] = pm[:, d_ssm:]

    for c in fetch(jnp.minimum(i, last_tile), slot):
        c.wait()

    x = xbuf[slot].reshape(T_TILE * BATCH_ROWS, xbuf.shape[-1])
    h_next, out = step(x, hbuf[...])
    hbuf[...] = h_next
    obuf[oslot] = out.reshape(obuf.shape[1:])

    @pl.when(i >= 1)
    def _():
        for c in writeback(i - 1, oslot):
            c.start()

    @pl.when(i == n_tiles)
    def _():
        @pl.when(n_tiles >= 2)
        def _():
            for c in writeback(i - 2, slot):
                c.wait()
        for c in writeback(i - 1, oslot):
            c.wait()


def _const_spec(a):
    nd = a.ndim
    return pl.BlockSpec(a.shape, lambda i: (0,) * nd, pipeline_mode=pl.Buffered(1))


def _block(x, consts, hbm_weights, n_state, d_ssm, d_pool, d_ff):
    B, S, D = x.shape
    rows = T_TILE * B
    assert rows == STAGE_SLOTS * NARROW_STAGE_ROWS
    any_spec = pl.BlockSpec(memory_space=pl.ANY)
    n_oct = d_ssm // (OCT_GROUPS * SSM_GROUP)
    n_half = d_ssm // (HALF_GROUPS * SSM_GROUP)
    return pl.pallas_call(
        _block_kernel,
        grid=(S // T_TILE + 1,),
        in_specs=[any_spec] + [_const_spec(c) for c in consts] + [any_spec] * len(hbm_weights),
        out_specs=any_spec,
        out_shape=jax.ShapeDtypeStruct((B, S, D), _F32),
        scratch_shapes=[
            pltpu.VMEM((2, T_TILE, B, D), _F32),
            pltpu.SemaphoreType.DMA((2, BATCH_ROWS)),
            pltpu.VMEM((2, T_TILE, B, D), _F32),
            pltpu.SemaphoreType.DMA((2, BATCH_ROWS)),
            pltpu.VMEM((rows, D), _F32),
            pltpu.VMEM((rows // 2, n_state), _F32),
            pltpu.VMEM((rows // 2, n_state), _F32),
            pltpu.VMEM((B, n_state), _F32),
            pltpu.VMEM((B, n_state), _F32),
            pltpu.VMEM((B, d_ssm), _F32),
            pltpu.VMEM((N_META * B, d_pool), _F32),
            pltpu.VMEM((rows, d_ff), _BF16),
            pltpu.VMEM((D, D), _BF16),
            pltpu.VMEM((D, D), _BF16),
            pltpu.VMEM((D, d_ff), _BF16),
            pltpu.VMEM((D, d_ff), _BF16),
            pltpu.VMEM((d_ff, D), _BF16),
            pltpu.VMEM((2, n_oct, 2 * OCT_GROUPS * SSM_GROUP, OCT_GROUPS * SSM_STATE), _BF16),
            pltpu.VMEM((2, n_oct, OCT_GROUPS * SSM_STATE, 2 * OCT_GROUPS * SSM_GROUP), _BF16),
            pltpu.VMEM((n_half, HALF_GROUPS * SSM_GROUP, HALF_GROUPS * SSM_GROUP), _BF16),
            pltpu.VMEM((n_half, HALF_GROUPS * SSM_GROUP, HALF_GROUPS * SSM_GROUP), _BF16),
            pltpu.VMEM((STAGE_SLOTS * WIDE_STAGE_ROWS, d_ff), _F32),
            pltpu.SemaphoreType.DMA((STAGE_SLOTS,)),
        ],
        compiler_params=pltpu.CompilerParams(dimension_semantics=("arbitrary",),
                                             vmem_limit_bytes=VMEM_LIMIT),
        name="block",
    )(x, *consts, *hbm_weights)


def _lanes_by_group(blocks, per):
    G, r, c = blocks.shape
    return jnp.transpose(blocks.reshape(G // per, per, r, c), (0, 2, 1, 3)).reshape(G // per, r, per * c)


def _s5_params(lam_re, lam_im, log_step, b_re, b_im, c_re, c_im):
    lr = jnp.minimum(lam_re, -1e-4)
    li = lam_im
    step = jnp.exp(log_step)[:, None]
    mag = jnp.exp(lr * step)
    ang = li * step
    abr = mag * jnp.cos(ang)
    abi = mag * jnp.sin(ang)
    nr = abr - 1.0
    ni = abi
    den = lr * lr + li * li
    cr = ((nr * lr + ni * li) / den)[..., None]
    ci = ((ni * lr - nr * li) / den)[..., None]
    bbr = cr * b_re - ci * b_im
    bbi = cr * b_im + ci * b_re
    a_r, a_i = abr[..., None], abi[..., None]
    abbr = a_r * bbr - a_i * bbi
    abbi = a_r * bbi + a_i * bbr
    a2r = abr * abr - abi * abi
    a2i = 2.0 * abr * abi
    car = c_re * abr[:, None, :] - c_im * abi[:, None, :]
    cai = c_re * abi[:, None, :] + c_im * abr[:, None, :]
    loc = jnp.einsum('ghp,gpk->ghk', c_re, bbr) - jnp.einsum('ghp,gpk->ghk', c_im, bbi)
    t = lambda a: jnp.swapaxes(a, 1, 2)
    by_oct = lambda a: _lanes_by_group(a, OCT_GROUPS)
    zc = jnp.stack([jnp.stack([by_oct(t(abbr)), by_oct(t(bbr))]),
                    jnp.stack([by_oct(t(abbi)), by_oct(t(bbi))])])
    vc = jnp.stack([jnp.stack([by_oct(t(c_re)), by_oct(t(car))]),
                    jnp.stack([by_oct(-t(c_im)), by_oct(-t(cai))])])
    return a2r, a2i, zc, vc, _lanes_by_group(t(loc), HALF_GROUPS)


def kernel(x, meta_tokens, norm1_g, w_in, ssm_lambda_re, ssm_lambda_im, ssm_log_step, ssm_b_re, ssm_b_im, ssm_c_re, ssm_c_im, ssm_d, ssm_glu_w, ssm_glu_b, ssm_norm_g, pool_w, pool_scale, pool_norm_g, w_out, norm2_g, w_gate, w_up, w_down, final_norm_g):
    B, S, D = x.shape
    assert B == BATCH_ROWS and norm1_g.shape[0] == 1
    assert S % T_TILE == 0 and T_TILE % 2 == 0 and N_META % 2 == 0
    f32 = _F32
    a2r, a2i, zc, vc, locc = _s5_params(
        ssm_lambda_re[0].astype(f32), ssm_lambda_im[0].astype(f32), ssm_log_step[0].astype(f32),
        ssm_b_re[0].astype(f32), ssm_b_im[0].astype(f32), ssm_c_re[0].astype(f32), ssm_c_im[0].astype(f32))
    row = lambda a: a.astype(f32).reshape(1, -1)
    consts = (
        jnp.repeat(meta_tokens.astype(f32), B, axis=0),
        row(norm1_g[0]),
        row(a2r), row(a2i),
        zc,
        vc,
        locc,
        row(ssm_d[0]),
        _lanes_by_group(ssm_glu_w[0].astype(f32), HALF_GROUPS),
        row(ssm_glu_b[0]),
        row(ssm_norm_g[0]),
        pool_w[0].astype(_BF16),
        row(pool_scale[0]),
        row(pool_norm_g[0]),
        row(norm2_g[0]),
        row(final_norm_g),
    )
    hbm_weights = tuple(w.astype(f32) for w in (w_in, w_out, w_gate, w_up, w_down))
    out = _block(x.astype(f32), consts, hbm_weights, n_state=a2r.size, d_ssm=ssm_d[0].size,
                 d_pool=pool_scale[0].size, d_ff=w_gate.shape[-1])
    return out.astype(x.dtype)
```

```python
import jax
import jax.numpy as jnp
from jax import lax
from jax.experimental import pallas as pl
from jax.experimental.pallas import tpu as pltpu

N_META = 16
SSM_GROUP = 16
SSM_STATE = 64
POOL_WINDOWS = (2, 4, 8, 16)
EPS = 1e-6

BATCH_ROWS = 8
HALF_GROUPS = 16
PHASES = 4
QUAD_GROUPS = 4
SCAN_QUADS = 2
T_TILE = 64
FF_CHUNK = 256
STAGE_SLOTS = 4
WIDE_STAGE_ROWS = 64
NARROW_STAGE_ROWS = 128
VMEM_LIMIT = 60 * 1024 * 1024

_BF16 = jnp.bfloat16
_F32 = jnp.float32


def _rms(x, g):
    return x * lax.rsqrt(jnp.mean(x * x, axis=-1, keepdims=True) + EPS) * g


def _dot(a, b):
    return jnp.dot(a, b, preferred_element_type=_F32)


def _split_phases(a, n_chunks):
    blk = lambda t: a[t * BATCH_ROWS:(t + 1) * BATCH_ROWS]
    return [jnp.concatenate([blk(PHASES * k + p) for k in range(n_chunks)], axis=0) for p in range(PHASES)]


def _merge_phases(parts, n_chunks):
    blk = lambda a, k: a[k * BATCH_ROWS:(k + 1) * BATCH_ROWS]
    return jnp.concatenate([blk(p, k) for k in range(n_chunks) for p in parts], axis=0)


def _tile_copies(hbm, vmem, sems, slot, tile, nt, to_vmem):
    copies = []
    for b in range(BATCH_ROWS):
        h = hbm.at[b, pl.ds(tile * nt, nt), :]
        v = vmem.at[slot, :, b, :]
        src, dst = (h, v) if to_vmem else (v, h)
        copies.append(pltpu.make_async_copy(src, dst, sems.at[slot, b]))
    return copies


def _load_cast(w_hbm, dst, stage, sems, chunk):
    n = dst.shape[0] // chunk
    ahead = STAGE_SLOTS - 1

    def copy(c, s):
        return pltpu.make_async_copy(w_hbm.at[0, pl.ds(c * chunk, chunk), :],
                                     stage.at[pl.ds(s * chunk, chunk), :], sems.at[s])

    for c in range(min(ahead, n)):
        copy(c, c).start()

    def body(c, carry):
        s = lax.rem(c, STAGE_SLOTS)

        @pl.when(c + ahead < n)
        def _():
            copy(c + ahead, lax.rem(c + ahead, STAGE_SLOTS)).start()

        copy(c, s).wait()
        src = pl.multiple_of(s * chunk, chunk)
        row = pl.multiple_of(c * chunk, chunk)
        dst[pl.ds(row, chunk), :] = stage[pl.ds(src, chunk), :].astype(dst.dtype)
        return carry

    lax.fori_loop(0, n, body, 0)


def _expand_block_diag(src, dst, block_rows, block_lanes, n_blocks):
    lane = lax.broadcasted_iota(jnp.int32, src.shape, 1)
    lane_group = lax.rem(lane, n_blocks * block_lanes) // block_lanes
    for g in range(n_blocks):
        dst[g * block_rows:(g + 1) * block_rows, :] = jnp.where(lane_group == g, src, 0.0).astype(dst.dtype)


def _block_kernel(x_hbm, meta_ref, n1g_ref, a4r_ref, a4i_ref, zc_ref, vc_ref, locc_ref, d_ref, gluc_ref,
                  gb_ref, sng_ref, pw_ref, ps_ref, png_ref, n2g_ref, fg_ref,
                  win_hbm, wout_hbm, wg_hbm, wu_hbm, wd_hbm, o_hbm,
                  xbuf, xsem, obuf, osem, hbuf, zr, zi, st_r, st_i, u_last, halo, act,
                  win_ref, wout_ref, wg_ref, wu_ref, wd_ref, wz_ref, v_ref, loc_ref, gw_ref,
                  wide_stage, wsem):
    i = pl.program_id(0)
    n_tiles = pl.num_programs(0) - 1
    slot = lax.rem(i, 2)
    oslot = 1 - slot
    last_tile = n_tiles - 1

    def fetch(tile, s):
        return _tile_copies(x_hbm, xbuf, xsem, s, tile, T_TILE, to_vmem=True)

    def writeback(tile, s):
        return _tile_copies(o_hbm, obuf, osem, s, tile, T_TILE, to_vmem=False)

    d_ssm = d_ref.shape[-1]
    half_in = HALF_GROUPS * SSM_GROUP
    quad_in = QUAD_GROUPS * SSM_GROUP
    quad_state = QUAD_GROUPS * SSM_STATE
    n_quads = d_ssm // quad_in
    n_halves = d_ssm // half_in
    halo_rows = halo.shape[0]
    n_ff_chunks = wg_ref.shape[-1] // FF_CHUNK

    def build_weights():
        _load_cast(win_hbm, win_ref, hbuf, wsem, NARROW_STAGE_ROWS)
        _load_cast(wout_hbm, wout_ref, hbuf, wsem, NARROW_STAGE_ROWS)
        _load_cast(wd_hbm, wd_ref, hbuf, wsem, NARROW_STAGE_ROWS)
        _load_cast(wg_hbm, wg_ref, wide_stage, wsem, WIDE_STAGE_ROWS)
        _load_cast(wu_hbm, wu_ref, wide_stage, wsem, WIDE_STAGE_ROWS)
        ph_rows = QUAD_GROUPS * SSM_GROUP
        for q in range(n_quads):
            for ri in range(2):
                for ph in range(PHASES):
                    _expand_block_diag(zc_ref[ri, ph, q], wz_ref.at[ri, q, ph * ph_rows:(ph + 1) * ph_rows, :],
                                       SSM_GROUP, SSM_STATE, QUAD_GROUPS)
                _expand_block_diag(vc_ref[ri, q], v_ref.at[ri, q], SSM_STATE, SSM_GROUP, QUAD_GROUPS)
            for ph in range(PHASES):
                _expand_block_diag(locc_ref[ph, q], loc_ref.at[q, ph * ph_rows:(ph + 1) * ph_rows, :],
                                   SSM_GROUP, SSM_GROUP, QUAD_GROUPS)
        for hh in range(n_halves):
            _expand_block_diag(gluc_ref[hh], gw_ref.at[hh], SSM_GROUP, SSM_GROUP, HALF_GROUPS)

    def project(rows):
        n1 = _rms(rows, n1g_ref[...]).astype(_BF16)
        return _dot(n1, win_ref[...])

    def quad_lanes(parts, q):
        return jnp.concatenate([p[:, q * quad_in:(q + 1) * quad_in] for p in parts], axis=1)

    def s5_states(u_in, n_chunks, qq):
        rows = n_chunks * BATCH_ROWS
        for q in range(qq * SCAN_QUADS, (qq + 1) * SCAN_QUADS):
            qs = slice(q * quad_state, (q + 1) * quad_state)
            lhs = quad_lanes(u_in, q)
            zr[0:rows, qs] = _dot(lhs, wz_ref[0, q])
            zi[0:rows, qs] = _dot(lhs, wz_ref[1, q])
        cs = slice(qq * SCAN_QUADS * quad_state, (qq + 1) * SCAN_QUADS * quad_state)
        width = SCAN_QUADS * quad_state
        ar = jnp.broadcast_to(a4r_ref[:, cs], (BATCH_ROWS, width))
        ai = jnp.broadcast_to(a4i_ref[:, cs], (BATCH_ROWS, width))
        sr = st_r[:, cs]
        si = st_i[:, cs]
        for k in range(n_chunks):
            rs = slice(k * BATCH_ROWS, (k + 1) * BATCH_ROWS)
            sr, si = (ar * sr - ai * si + zr[rs, cs], ar * si + ai * sr + zi[rs, cs])
            zr[rs, cs] = sr
            zi[rs, cs] = si
        st_r[:, cs] = sr
        st_i[:, cs] = si

    def s5_inputs(u, n_chunks):
        parts = _split_phases(u, n_chunks)
        state_in = []
        for p in range(1, PHASES):
            carry = slice((p - 1) * BATCH_ROWS, p * BATCH_ROWS)
            state_in.append(jnp.concatenate([u_last[carry, :], parts[p][:-BATCH_ROWS]], axis=0).astype(_BF16))
            u_last[carry, :] = parts[p][-BATCH_ROWS:]
        local_in = [p.astype(_BF16) for p in parts]
        return state_in + [local_in[0]], local_in

    def ffn_chunk(n2, c):
        cs = slice(c * FF_CHUNK, (c + 1) * FF_CHUNK)
        gate = _dot(n2, wg_ref[:, cs])
        up = _dot(n2, wu_ref[:, cs])
        act[:, cs] = (gate * jax.nn.sigmoid(gate) * up).astype(_BF16)

    def step(x, h):
        rows = x.shape[0]
        n_chunks = T_TILE // PHASES
        crows = n_chunks * BATCH_ROWS
        ff = iter(range(n_ff_chunks))

        def ffn_chunks(n):
            for _ in range(n):
                c = next(ff, None)
                if c is not None:
                    ffn_chunk(n2, c)

        n1 = _rms(x, n1g_ref[...]).astype(_BF16)
        proj = _dot(n1, win_ref[...])
        n2 = _rms(h, n2g_ref[...]).astype(_BF16)
        u = proj[:, :d_ssm]
        v = proj[:, d_ssm:]

        state_in, local_in = s5_inputs(u, n_chunks)
        for qq in range(n_quads // SCAN_QUADS):
            s5_states(state_in, n_chunks, qq)
            ffn_chunks(1)
        chunk_out = []
        for q in range(n_quads):
            qs = slice(q * quad_state, (q + 1) * quad_state)
            chunk_out.append(_dot(zr[0:crows, qs].astype(_BF16), v_ref[0, q])
                             + _dot(zi[0:crows, qs].astype(_BF16), v_ref[1, q])
                             + _dot(quad_lanes(local_in, q), loc_ref[q]))
            if q % 4 == 3:
                ffn_chunks(1)
        y_parts = [jnp.concatenate([c[:, p * quad_in:(p + 1) * quad_in] for c in chunk_out], axis=-1)
                   for p in range(PHASES)]
        y = _merge_phases(y_parts, n_chunks) + d_ref[...] * u
        g = jax.nn.gelu(y)
        gb = g.astype(_BF16)
        ffn_chunks(1)
        gate = jnp.concatenate([_dot(gb[:, hh * half_in:(hh + 1) * half_in], gw_ref[hh])
                                for hh in range(n_halves)], axis=-1) + gb_ref[...]
        y_ssm = _rms(g * jax.nn.sigmoid(gate), sng_ref[...])
        ffn_chunks(1)

        vext = jnp.concatenate([halo[...], v], axis=0)
        halo[...] = v[rows - halo_rows:, :]
        gd = v.shape[-1] // len(POOL_WINDOWS)
        yps = []
        for k, w in enumerate(POOL_WINDOWS):
            acc = vext[:, k * gd:(k + 1) * gd]
            span = 1
            while span < w:
                shift = span * BATCH_ROWS
                acc = acc[shift:, :] + acc[:-shift, :]
                span *= 2
            pk = acc[acc.shape[0] - rows:, :] * (1.0 / w) - v[:, k * gd:(k + 1) * gd]
            yps.append(_dot(pk.astype(_BF16), pw_ref[k]))
        ffn_chunks(1)
        y_pool = _rms(jnp.concatenate(yps, axis=-1) * ps_ref[...], png_ref[...])
        ffn_chunks(n_ff_chunks)

        mixed = jnp.concatenate([y_ssm, y_pool], axis=-1).astype(_BF16)
        h2 = h + _dot(act[...], wd_ref[...])
        out = _rms(h2, fg_ref[...])
        h_next = x + _dot(mixed, wout_ref[...])
        return h_next, out

    @pl.when(i == 0)
    def _():
        for c in fetch(0, 0):
            c.start()

    @pl.when(i < n_tiles)
    def _():
        for c in fetch(jnp.minimum(i + 1, last_tile), oslot):
            c.start()

    @pl.when(i >= 3)
    def _():
        for c in writeback(i - 3, oslot):
            c.wait()

    @pl.when(i == 0)
    def _():
        build_weights()
        st_r[...] = jnp.zeros_like(st_r)
        st_i[...] = jnp.zeros_like(st_i)
        u_last[...] = jnp.zeros_like(u_last)
        hbuf[...] = jnp.zeros_like(hbuf)
        pm = project(meta_ref[...])
        state_in, _ = s5_inputs(pm[:, :d_ssm], N_META // PHASES)
        for qq in range(n_quads // SCAN_QUADS):
            s5_states(state_in, N_META // PHASES, qq)
        halo[...] = pm[:, d_ssm:]

    for c in fetch(jnp.minimum(i, last_tile), slot):
        c.wait()

    x = xbuf[slot].reshape(T_TILE * BATCH_ROWS, xbuf.shape[-1])
    h_next, out = step(x, hbuf[...])
    hbuf[...] = h_next
    obuf[oslot] = out.reshape(obuf.shape[1:])

    @pl.when(i >= 1)
    def _():
        for c in writeback(i - 1, oslot):
            c.start()

    @pl.when(i == n_tiles)
    def _():
        @pl.when(n_tiles >= 2)
        def _():
            for c in writeback(i - 2, slot):
                c.wait()
        for c in writeback(i - 1, oslot):
            c.wait()


def _const_spec(a):
    nd = a.ndim
    return pl.BlockSpec(a.shape, lambda i: (0,) * nd, pipeline_mode=pl.Buffered(1))


def _block(x, consts, hbm_weights, n_state, d_ssm, d_pool, d_ff):
    B, S, D = x.shape
    rows = T_TILE * B
    assert rows == STAGE_SLOTS * NARROW_STAGE_ROWS
    any_spec = pl.BlockSpec(memory_space=pl.ANY)
    n_quad = d_ssm // (QUAD_GROUPS * SSM_GROUP)
    n_half = d_ssm // (HALF_GROUPS * SSM_GROUP)
    tile_k = PHASES * QUAD_GROUPS * SSM_GROUP
    quad_state = QUAD_GROUPS * SSM_STATE
    return pl.pallas_call(
        _block_kernel,
        grid=(S // T_TILE + 1,),
        in_specs=[any_spec] + [_const_spec(c) for c in consts] + [any_spec] * len(hbm_weights),
        out_specs=any_spec,
        out_shape=jax.ShapeDtypeStruct((B, S, D), _F32),
        scratch_shapes=[
            pltpu.VMEM((2, T_TILE, B, D), _F32),
            pltpu.SemaphoreType.DMA((2, BATCH_ROWS)),
            pltpu.VMEM((2, T_TILE, B, D), _F32),
            pltpu.SemaphoreType.DMA((2, BATCH_ROWS)),
            pltpu.VMEM((rows, D), _F32),
            pltpu.VMEM((rows // PHASES, n_state), _F32),
            pltpu.VMEM((rows // PHASES, n_state), _F32),
            pltpu.VMEM((B, n_state), _F32),
            pltpu.VMEM((B, n_state), _F32),
            pltpu.VMEM(((PHASES - 1) * B, d_ssm), _F32),
            pltpu.VMEM((N_META * B, d_pool), _F32),
            pltpu.VMEM((rows, d_ff), _BF16),
            pltpu.VMEM((D, D), _BF16),
            pltpu.VMEM((D, D), _BF16),
            pltpu.VMEM((D, d_ff), _BF16),
            pltpu.VMEM((D, d_ff), _BF16),
            pltpu.VMEM((d_ff, D), _BF16),
            pltpu.VMEM((2, n_quad, tile_k, quad_state), _BF16),
            pltpu.VMEM((2, n_quad, quad_state, tile_k), _BF16),
            pltpu.VMEM((n_quad, tile_k, tile_k), _BF16),
            pltpu.VMEM((n_half, HALF_GROUPS * SSM_GROUP, HALF_GROUPS * SSM_GROUP), _BF16),
            pltpu.VMEM((STAGE_SLOTS * WIDE_STAGE_ROWS, d_ff), _F32),
            pltpu.SemaphoreType.DMA((STAGE_SLOTS,)),
        ],
        compiler_params=pltpu.CompilerParams(dimension_semantics=("arbitrary",),
                                             vmem_limit_bytes=VMEM_LIMIT),
        name="block",
    )(x, *consts, *hbm_weights)


def _lanes_by_group(blocks, per):
    G, r, c = blocks.shape
    return jnp.transpose(blocks.reshape(G // per, per, r, c), (0, 2, 1, 3)).reshape(G // per, r, per * c)


def _cmul(ar, ai, br, bi):
    return ar * br - ai * bi, ar * bi + ai * br


def _s5_params(lam_re, lam_im, log_step, b_re, b_im, c_re, c_im):
    lr = jnp.minimum(lam_re, -1e-4)
    li = lam_im
    step = jnp.exp(log_step)[:, None]
    mag = jnp.exp(lr * step)
    ang = li * step
    abr = mag * jnp.cos(ang)
    abi = mag * jnp.sin(ang)
    nr = abr - 1.0
    ni = abi
    den = lr * lr + li * li
    cr = ((nr * lr + ni * li) / den)[..., None]
    ci = ((ni * lr - nr * li) / den)[..., None]
    bbr = cr * b_re - ci * b_im
    bbi = cr * b_im + ci * b_re
    pw = [(jnp.ones_like(abr), jnp.zeros_like(abi))]
    for _ in range(PHASES):
        pw.append(_cmul(pw[-1][0], pw[-1][1], abr, abi))
    t = lambda a: jnp.swapaxes(a, 1, 2)
    by_quad = lambda a: _lanes_by_group(a, QUAD_GROUPS)
    ab = [_cmul(pr[..., None], pi[..., None], bbr, bbi) for pr, pi in pw[:PHASES]]
    ca = [_cmul(c_re, c_im, pr[:, None, :], pi[:, None, :]) for pr, pi in pw[:PHASES]]
    zc = jnp.stack([jnp.stack([by_quad(t(ab[PHASES - 1 - ph][ri])) for ph in range(PHASES)]) for ri in range(2)])
    sign = (1.0, -1.0)
    vc = jnp.stack([jnp.concatenate([by_quad(sign[ri] * t(ca[ph][ri])) for ph in range(PHASES)], axis=-1)
                    for ri in range(2)])
    loc = [jnp.einsum('ghp,gpk->ghk', ca[d][0], bbr) - jnp.einsum('ghp,gpk->ghk', ca[d][1], bbi)
           for d in range(PHASES - 1)]
    zero = jnp.zeros_like(by_quad(t(loc[0])))
    locc = jnp.stack([jnp.concatenate([by_quad(t(loc[p - j])) if 1 <= j <= p else zero for p in range(PHASES)],
                                      axis=-1) for j in range(PHASES)])
    return pw[PHASES], zc, vc, locc


def kernel(x, meta_tokens, norm1_g, w_in, ssm_lambda_re, ssm_lambda_im, ssm_log_step, ssm_b_re, ssm_b_im, ssm_c_re, ssm_c_im, ssm_d, ssm_glu_w, ssm_glu_b, ssm_norm_g, pool_w, pool_scale, pool_norm_g, w_out, norm2_g, w_gate, w_up, w_down, final_norm_g):
    B, S, D = x.shape
    assert B == BATCH_ROWS and norm1_g.shape[0] == 1
    assert S % T_TILE == 0 and T_TILE % PHASES == 0 and N_META % PHASES == 0
    f32 = _F32
    (a4r, a4i), zc, vc, locc = _s5_params(
        ssm_lambda_re[0].astype(f32), ssm_lambda_im[0].astype(f32), ssm_log_step[0].astype(f32),
        ssm_b_re[0].astype(f32), ssm_b_im[0].astype(f32), ssm_c_re[0].astype(f32), ssm_c_im[0].astype(f32))
    row = lambda a: a.astype(f32).reshape(1, -1)
    consts = (
        jnp.repeat(meta_tokens.astype(f32), B, axis=0),
        row(norm1_g[0]),
        row(a4r), row(a4i),
        zc,
        vc,
        locc,
        row(ssm_d[0]),
        _lanes_by_group(ssm_glu_w[0].astype(f32), HALF_GROUPS),
        row(ssm_glu_b[0]),
        row(ssm_norm_g[0]),
        pool_w[0].astype(_BF16),
        row(pool_scale[0]),
        row(pool_norm_g[0]),
        row(norm2_g[0]),
        row(final_norm_g),
    )
    hbm_weights = tuple(w.astype(f32) for w in (w_in, w_out, w_gate, w_up, w_down))
    out = _block(x.astype(f32), consts, hbm_weights, n_state=a4r.size, d_ssm=ssm_d[0].size,
                 d_pool=pool_scale[0].size, d_ff=w_gate.shape[-1])
    return out.astype(x.dtype)
```

```python
import jax
import jax.numpy as jnp
from jax import lax
from jax.experimental import pallas as pl
from jax.experimental.pallas import tpu as pltpu

N_META = 16
SSM_GROUP = 16
SSM_STATE = 64
POOL_WINDOWS = (2, 4, 8, 16)
EPS = 1e-6

BATCH_ROWS = 8
HALF_GROUPS = 16
PHASES = 4
QUAD_GROUPS = 4
SCAN_QUADS = 2
T_TILE = 64
FF_CHUNK = 256
STAGE_SLOTS = 8
WIDE_STAGE_ROWS = 32
NARROW_STAGE_ROWS = 64
VMEM_LIMIT = 60 * 1024 * 1024

_BF16 = jnp.bfloat16
_F32 = jnp.float32


def _rms(x, g):
    return x * lax.rsqrt(jnp.mean(x * x, axis=-1, keepdims=True) + EPS) * g


def _dot(a, b):
    return jnp.dot(a, b, preferred_element_type=_F32)


def _split_phases(a, n_chunks):
    blk = lambda t: a[t * BATCH_ROWS:(t + 1) * BATCH_ROWS]
    return [jnp.concatenate([blk(PHASES * k + p) for k in range(n_chunks)], axis=0) for p in range(PHASES)]


def _merge_phases(parts, n_chunks):
    blk = lambda a, k: a[k * BATCH_ROWS:(k + 1) * BATCH_ROWS]
    return jnp.concatenate([blk(p, k) for k in range(n_chunks) for p in parts], axis=0)


def _tile_copies(hbm, vmem, sems, slot, tile, nt, to_vmem):
    copies = []
    for b in range(BATCH_ROWS):
        h = hbm.at[b, pl.ds(tile * nt, nt), :]
        v = vmem.at[slot, :, b, :]
        src, dst = (h, v) if to_vmem else (v, h)
        copies.append(pltpu.make_async_copy(src, dst, sems.at[slot, b]))
    return copies


def _load_cast(w_hbm, dst, stage, sems, chunk):
    n = dst.shape[0] // chunk
    ahead = STAGE_SLOTS - 1

    def copy(c, s):
        return pltpu.make_async_copy(w_hbm.at[0, pl.ds(c * chunk, chunk), :],
                                     stage.at[pl.ds(s * chunk, chunk), :], sems.at[s])

    for c in range(min(ahead, n)):
        copy(c, c).start()

    def body(c, carry):
        s = lax.rem(c, STAGE_SLOTS)

        @pl.when(c + ahead < n)
        def _():
            copy(c + ahead, lax.rem(c + ahead, STAGE_SLOTS)).start()

        copy(c, s).wait()
        src = pl.multiple_of(s * chunk, chunk)
        row = pl.multiple_of(c * chunk, chunk)
        dst[pl.ds(row, chunk), :] = stage[pl.ds(src, chunk), :].astype(dst.dtype)
        return carry

    lax.fori_loop(0, n, body, 0)


def _expand_block_diag(src, dst, block_rows, block_lanes, n_blocks):
    lane = lax.broadcasted_iota(jnp.int32, src.shape, 1)
    lane_group = lax.rem(lane, n_blocks * block_lanes) // block_lanes
    for g in range(n_blocks):
        dst[g * block_rows:(g + 1) * block_rows, :] = jnp.where(lane_group == g, src, 0.0).astype(dst.dtype)


def _block_kernel(x_hbm, meta_ref, n1g_ref, a4r_ref, a4i_ref, zc_ref, vc_ref, locc_ref, d_ref, gluc_ref,
                  gb_ref, sng_ref, pw_ref, ps_ref, png_ref, n2g_ref, fg_ref,
                  win_hbm, wout_hbm, wg_hbm, wu_hbm, wd_hbm, o_hbm,
                  xbuf, xsem, obuf, osem, hbuf, zr, zi, st_r, st_i, u_last, halo, act,
                  win_ref, wout_ref, wg_ref, wu_ref, wd_ref, wz_ref, v_ref, loc_ref, gw_ref,
                  wide_stage, wsem):
    i = pl.program_id(0)
    n_tiles = pl.num_programs(0) - 1
    slot = lax.rem(i, 2)
    oslot = 1 - slot

    def fetch(tile, s):
        return _tile_copies(x_hbm, xbuf, xsem, s, tile, T_TILE, to_vmem=True)

    def writeback(tile, s):
        return _tile_copies(o_hbm, obuf, osem, s, tile, T_TILE, to_vmem=False)

    d_ssm = d_ref.shape[-1]
    half_in = HALF_GROUPS * SSM_GROUP
    quad_in = QUAD_GROUPS * SSM_GROUP
    quad_state = QUAD_GROUPS * SSM_STATE
    n_quads = d_ssm // quad_in
    n_halves = d_ssm // half_in
    halo_rows = halo.shape[0]
    n_ff_chunks = wg_ref.shape[-1] // FF_CHUNK

    def build_weights():
        _load_cast(win_hbm, win_ref, hbuf, wsem, NARROW_STAGE_ROWS)
        _load_cast(wout_hbm, wout_ref, hbuf, wsem, NARROW_STAGE_ROWS)
        _load_cast(wd_hbm, wd_ref, hbuf, wsem, NARROW_STAGE_ROWS)
        _load_cast(wg_hbm, wg_ref, wide_stage, wsem, WIDE_STAGE_ROWS)
        _load_cast(wu_hbm, wu_ref, wide_stage, wsem, WIDE_STAGE_ROWS)
        ph_rows = QUAD_GROUPS * SSM_GROUP
        for q in range(n_quads):
            for ri in range(2):
                for ph in range(PHASES):
                    _expand_block_diag(zc_ref[ri, ph, q], wz_ref.at[ri, q, ph * ph_rows:(ph + 1) * ph_rows, :],
                                       SSM_GROUP, SSM_STATE, QUAD_GROUPS)
                _expand_block_diag(vc_ref[ri, q], v_ref.at[ri, q], SSM_STATE, SSM_GROUP, QUAD_GROUPS)
            for ph in range(PHASES):
                _expand_block_diag(locc_ref[ph, q], loc_ref.at[q, ph * ph_rows:(ph + 1) * ph_rows, :],
                                   SSM_GROUP, SSM_GROUP, QUAD_GROUPS)
        for hh in range(n_halves):
            _expand_block_diag(gluc_ref[hh], gw_ref.at[hh], SSM_GROUP, SSM_GROUP, HALF_GROUPS)

    def project(rows):
        n1 = _rms(rows, n1g_ref[...]).astype(_BF16)
        return _dot(n1, win_ref[...])

    def quad_lanes(parts, q):
        return jnp.concatenate([p[:, q * quad_in:(q + 1) * quad_in] for p in parts], axis=1)

    def s5_states(u_in, n_chunks, qq):
        rows = n_chunks * BATCH_ROWS
        for q in range(qq * SCAN_QUADS, (qq + 1) * SCAN_QUADS):
            qs = slice(q * quad_state, (q + 1) * quad_state)
            lhs = quad_lanes(u_in, q)
            zr[0:rows, qs] = _dot(lhs, wz_ref[0, q])
            zi[0:rows, qs] = _dot(lhs, wz_ref[1, q])
        cs = slice(qq * SCAN_QUADS * quad_state, (qq + 1) * SCAN_QUADS * quad_state)
        width = SCAN_QUADS * quad_state
        ar = jnp.broadcast_to(a4r_ref[:, cs], (BATCH_ROWS, width))
        ai = jnp.broadcast_to(a4i_ref[:, cs], (BATCH_ROWS, width))
        sr = st_r[:, cs]
        si = st_i[:, cs]
        for k in range(n_chunks):
            rs = slice(k * BATCH_ROWS, (k + 1) * BATCH_ROWS)
            sr, si = (ar * sr - ai * si + zr[rs, cs], ar * si + ai * sr + zi[rs, cs])
            zr[rs, cs] = sr
            zi[rs, cs] = si
        st_r[:, cs] = sr
        st_i[:, cs] = si

    def s5_inputs(u, n_chunks):
        parts = _split_phases(u, n_chunks)
        state_in = []
        for p in range(1, PHASES):
            carry = slice((p - 1) * BATCH_ROWS, p * BATCH_ROWS)
            state_in.append(jnp.concatenate([u_last[carry, :], parts[p][:-BATCH_ROWS]], axis=0).astype(_BF16))
            u_last[carry, :] = parts[p][-BATCH_ROWS:]
        local_in = [p.astype(_BF16) for p in parts]
        return state_in + [local_in[0]], local_in

    def ffn_chunk(n2, c):
        cs = slice(c * FF_CHUNK, (c + 1) * FF_CHUNK)
        gate = _dot(n2, wg_ref[:, cs])
        up = _dot(n2, wu_ref[:, cs])
        act[:, cs] = (gate * jax.nn.sigmoid(gate) * up).astype(_BF16)

    def step(with_mixer, with_ffn):
        rows = T_TILE * BATCH_ROWS
        n_chunks = T_TILE // PHASES
        crows = n_chunks * BATCH_ROWS
        ff = iter(range(n_ff_chunks if with_ffn else 0))

        def ffn_chunks(n):
            for _ in range(n):
                c = next(ff, None)
                if c is not None:
                    ffn_chunk(n2, c)

        def ffn_tail():
            ffn_chunks(n_ff_chunks)
            out = _rms(h + _dot(act[...], wd_ref[...]), fg_ref[...])
            obuf[oslot] = out.reshape(obuf.shape[1:])

        if with_ffn:
            h = hbuf[...]
        if not with_mixer:
            n2 = _rms(h, n2g_ref[...]).astype(_BF16)
            ffn_tail()
            return

        x = xbuf[slot].reshape(rows, xbuf.shape[-1])
        n1 = _rms(x, n1g_ref[...]).astype(_BF16)
        proj = _dot(n1, win_ref[...])
        if with_ffn:
            n2 = _rms(h, n2g_ref[...]).astype(_BF16)
        u = proj[:, :d_ssm]
        v = proj[:, d_ssm:]

        state_in, local_in = s5_inputs(u, n_chunks)
        for qq in range(n_quads // SCAN_QUADS):
            s5_states(state_in, n_chunks, qq)
            ffn_chunks(1)
        chunk_out = []
        for q in range(n_quads):
            qs = slice(q * quad_state, (q + 1) * quad_state)
            chunk_out.append(_dot(zr[0:crows, qs].astype(_BF16), v_ref[0, q])
                             + _dot(zi[0:crows, qs].astype(_BF16), v_ref[1, q])
                             + _dot(quad_lanes(local_in, q), loc_ref[q]))
            if q % 4 == 3:
                ffn_chunks(1)
        y_parts = [jnp.concatenate([c[:, p * quad_in:(p + 1) * quad_in] for c in chunk_out], axis=-1)
                   for p in range(PHASES)]
        y = _merge_phases(y_parts, n_chunks) + d_ref[...] * u
        g = jax.nn.gelu(y)
        gb = g.astype(_BF16)
        ffn_chunks(1)
        gate = jnp.concatenate([_dot(gb[:, hh * half_in:(hh + 1) * half_in], gw_ref[hh])
                                for hh in range(n_halves)], axis=-1) + gb_ref[...]
        y_ssm = _rms(g * jax.nn.sigmoid(gate), sng_ref[...])
        ffn_chunks(1)

        vext = jnp.concatenate([halo[...], v], axis=0)
        halo[...] = v[rows - halo_rows:, :]
        gd = v.shape[-1] // len(POOL_WINDOWS)
        yps = []
        for k, w in enumerate(POOL_WINDOWS):
            acc = vext[:, k * gd:(k + 1) * gd]
            span = 1
            while span < w:
                shift = span * BATCH_ROWS
                acc = acc[shift:, :] + acc[:-shift, :]
                span *= 2
            pk = acc[acc.shape[0] - rows:, :] * (1.0 / w) - v[:, k * gd:(k + 1) * gd]
            yps.append(_dot(pk.astype(_BF16), pw_ref[k]))
        ffn_chunks(1)
        y_pool = _rms(jnp.concatenate(yps, axis=-1) * ps_ref[...], png_ref[...])

        mixed = jnp.concatenate([y_ssm, y_pool], axis=-1).astype(_BF16)
        if with_ffn:
            ffn_tail()
        hbuf[...] = x + _dot(mixed, wout_ref[...])

    @pl.when(i == 0)
    def _():
        for c in fetch(0, 0):
            c.start()

    @pl.when(i + 1 < n_tiles)
    def _():
        for c in fetch(i + 1, oslot):
            c.start()

    @pl.when(i >= 3)
    def _():
        for c in writeback(i - 3, oslot):
            c.wait()

    @pl.when(i == 0)
    def _():
        build_weights()
        st_r[...] = jnp.zeros_like(st_r)
        st_i[...] = jnp.zeros_like(st_i)
        u_last[...] = jnp.zeros_like(u_last)
        pm = project(meta_ref[...])
        state_in, _ = s5_inputs(pm[:, :d_ssm], N_META // PHASES)
        for qq in range(n_quads // SCAN_QUADS):
            s5_states(state_in, N_META // PHASES, qq)
        halo[...] = pm[:, d_ssm:]

    @pl.when(i < n_tiles)
    def _():
        for c in fetch(i, slot):
            c.wait()

    @pl.when(i == 0)
    def _():
        step(with_mixer=True, with_ffn=False)

    @pl.when(jnp.logical_and(i > 0, i < n_tiles))
    def _():
        step(with_mixer=True, with_ffn=True)

    @pl.when(i == n_tiles)
    def _():
        step(with_mixer=False, with_ffn=True)

    @pl.when(i >= 1)
    def _():
        for c in writeback(i - 1, oslot):
            c.start()

    @pl.when(i == n_tiles)
    def _():
        @pl.when(n_tiles >= 2)
        def _():
            for c in writeback(i - 2, slot):
                c.wait()
        for c in writeback(i - 1, oslot):
            c.wait()


def _const_spec(a):
    nd = a.ndim
    return pl.BlockSpec(a.shape, lambda i: (0,) * nd, pipeline_mode=pl.Buffered(1))


def _block(x, consts, hbm_weights, n_state, d_ssm, d_pool, d_ff):
    B, S, D = x.shape
    rows = T_TILE * B
    assert rows == STAGE_SLOTS * NARROW_STAGE_ROWS
    any_spec = pl.BlockSpec(memory_space=pl.ANY)
    n_quad = d_ssm // (QUAD_GROUPS * SSM_GROUP)
    n_half = d_ssm // (HALF_GROUPS * SSM_GROUP)
    tile_k = PHASES * QUAD_GROUPS * SSM_GROUP
    quad_state = QUAD_GROUPS * SSM_STATE
    return pl.pallas_call(
        _block_kernel,
        grid=(S // T_TILE + 1,),
        in_specs=[any_spec] + [_const_spec(c) for c in consts] + [any_spec] * len(hbm_weights),
        out_specs=any_spec,
        out_shape=jax.ShapeDtypeStruct((B, S, D), _F32),
        scratch_shapes=[
            pltpu.VMEM((2, T_TILE, B, D), _F32),
            pltpu.SemaphoreType.DMA((2, BATCH_ROWS)),
            pltpu.VMEM((2, T_TILE, B, D), _F32),
            pltpu.SemaphoreType.DMA((2, BATCH_ROWS)),
            pltpu.VMEM((rows, D), _F32),
            pltpu.VMEM((rows // PHASES, n_state), _F32),
            pltpu.VMEM((rows // PHASES, n_state), _F32),
            pltpu.VMEM((B, n_state), _F32),
            pltpu.VMEM((B, n_state), _F32),
            pltpu.VMEM(((PHASES - 1) * B, d_ssm), _F32),
            pltpu.VMEM((N_META * B, d_pool), _F32),
            pltpu.VMEM((rows, d_ff), _BF16),
            pltpu.VMEM((D, D), _BF16),
            pltpu.VMEM((D, D), _BF16),
            pltpu.VMEM((D, d_ff), _BF16),
            pltpu.VMEM((D, d_ff), _BF16),
            pltpu.VMEM((d_ff, D), _BF16),
            pltpu.VMEM((2, n_quad, tile_k, quad_state), _BF16),
            pltpu.VMEM((2, n_quad, quad_state, tile_k), _BF16),
            pltpu.VMEM((n_quad, tile_k, tile_k), _BF16),
            pltpu.VMEM((n_half, HALF_GROUPS * SSM_GROUP, HALF_GROUPS * SSM_GROUP), _BF16),
            pltpu.VMEM((STAGE_SLOTS * WIDE_STAGE_ROWS, d_ff), _F32),
            pltpu.SemaphoreType.DMA((STAGE_SLOTS,)),
        ],
        compiler_params=pltpu.CompilerParams(dimension_semantics=("arbitrary",),
                                             vmem_limit_bytes=VMEM_LIMIT),
        name="block",
    )(x, *consts, *hbm_weights)


def _lanes_by_group(blocks, per):
    G, r, c = blocks.shape
    return jnp.transpose(blocks.reshape(G // per, per, r, c), (0, 2, 1, 3)).reshape(G // per, r, per * c)


def _cmul(ar, ai, br, bi):
    return ar * br - ai * bi, ar * bi + ai * br


def _s5_params(lam_re, lam_im, log_step, b_re, b_im, c_re, c_im):
    lr = jnp.minimum(lam_re, -1e-4)
    li = lam_im
    step = jnp.exp(log_step)[:, None]
    mag = jnp.exp(lr * step)
    ang = li * step
    abr = mag * jnp.cos(ang)
    abi = mag * jnp.sin(ang)
    nr = abr - 1.0
    ni = abi
    den = lr * lr + li * li
    cr = ((nr * lr + ni * li) / den)[..., None]
    ci = ((ni * lr - nr * li) / den)[..., None]
    bbr = cr * b_re - ci * b_im
    bbi = cr * b_im + ci * b_re
    pw = [(jnp.ones_like(abr), jnp.zeros_like(abi))]
    for _ in range(PHASES):
        pw.append(_cmul(pw[-1][0], pw[-1][1], abr, abi))
    t = lambda a: jnp.swapaxes(a, 1, 2)
    by_quad = lambda a: _lanes_by_group(a, QUAD_GROUPS)
    ab = [_cmul(pr[..., None], pi[..., None], bbr, bbi) for pr, pi in pw[:PHASES]]
    ca = [_cmul(c_re, c_im, pr[:, None, :], pi[:, None, :]) for pr, pi in pw[:PHASES]]
    zc = jnp.stack([jnp.stack([by_quad(t(ab[PHASES - 1 - ph][ri])) for ph in range(PHASES)]) for ri in range(2)])
    sign = (1.0, -1.0)
    vc = jnp.stack([jnp.concatenate([by_quad(sign[ri] * t(ca[ph][ri])) for ph in range(PHASES)], axis=-1)
                    for ri in range(2)])
    loc = [jnp.einsum('ghp,gpk->ghk', ca[d][0], bbr) - jnp.einsum('ghp,gpk->ghk', ca[d][1], bbi)
           for d in range(PHASES - 1)]
    zero = jnp.zeros_like(by_quad(t(loc[0])))
    locc = jnp.stack([jnp.concatenate([by_quad(t(loc[p - j])) if 1 <= j <= p else zero for p in range(PHASES)],
                                      axis=-1) for j in range(PHASES)])
    return pw[PHASES], zc, vc, locc


def kernel(x, meta_tokens, norm1_g, w_in, ssm_lambda_re, ssm_lambda_im, ssm_log_step, ssm_b_re, ssm_b_im, ssm_c_re, ssm_c_im, ssm_d, ssm_glu_w, ssm_glu_b, ssm_norm_g, pool_w, pool_scale, pool_norm_g, w_out, norm2_g, w_gate, w_up, w_down, final_norm_g):
    B, S, D = x.shape
    assert B == BATCH_ROWS and norm1_g.shape[0] == 1
    assert S % T_TILE == 0 and T_TILE % PHASES == 0 and N_META % PHASES == 0
    f32 = _F32
    (a4r, a4i), zc, vc, locc = _s5_params(
        ssm_lambda_re[0].astype(f32), ssm_lambda_im[0].astype(f32), ssm_log_step[0].astype(f32),
        ssm_b_re[0].astype(f32), ssm_b_im[0].astype(f32), ssm_c_re[0].astype(f32), ssm_c_im[0].astype(f32))
    row = lambda a: a.astype(f32).reshape(1, -1)
    consts = (
        jnp.repeat(meta_tokens.astype(f32), B, axis=0),
        row(norm1_g[0]),
        row(a4r), row(a4i),
        zc,
        vc,
        locc,
        row(ssm_d[0]),
        _lanes_by_group(ssm_glu_w[0].astype(f32), HALF_GROUPS),
        row(ssm_glu_b[0]),
        row(ssm_norm_g[0]),
        pool_w[0].astype(_BF16),
        row(pool_scale[0]),
        row(pool_norm_g[0]),
        row(norm2_g[0]),
        row(final_norm_g),
    )
    hbm_weights = tuple(w.astype(f32) for w in (w_in, w_out, w_gate, w_up, w_down))
    out = _block(x.astype(f32), consts, hbm_weights, n_state=a4r.size, d_ssm=ssm_d[0].size,
                 d_pool=pool_scale[0].size, d_ff=w_gate.shape[-1])
    return out.astype(x.dtype)
```

```python
import jax
import jax.numpy as jnp
from jax import lax
from jax.experimental import pallas as pl
from jax.experimental.pallas import tpu as pltpu

N_META = 16
SSM_GROUP = 16
SSM_STATE = 64
POOL_WINDOWS = (2, 4, 8, 16)
EPS = 1e-6

BATCH_ROWS = 8
HALF_GROUPS = 16
PHASES = 4
QUAD_GROUPS = 4
SCAN_QUADS = 2
T_TILE = 64
FF_CHUNK = 256
STAGE_SLOTS = 8
WIDE_STAGE_ROWS = 32
NARROW_STAGE_ROWS = 128
VMEM_LIMIT = 60 * 1024 * 1024

_BF16 = jnp.bfloat16
_F32 = jnp.float32


def _rms(x, g):
    return x * lax.rsqrt(jnp.mean(x * x, axis=-1, keepdims=True) + EPS) * g


def _dot(a, b):
    return jnp.dot(a, b, preferred_element_type=_F32)


def _split_phases(a, n_chunks):
    blk = lambda t: a[t * BATCH_ROWS:(t + 1) * BATCH_ROWS]
    return [jnp.concatenate([blk(PHASES * k + p) for k in range(n_chunks)], axis=0) for p in range(PHASES)]


def _merge_phases(parts, n_chunks):
    blk = lambda a, k: a[k * BATCH_ROWS:(k + 1) * BATCH_ROWS]
    return jnp.concatenate([blk(p, k) for k in range(n_chunks) for p in parts], axis=0)


def _tile_copies(hbm, vmem, sems, slot, tile, nt, to_vmem):
    copies = []
    for b in range(BATCH_ROWS):
        h = hbm.at[b, pl.ds(tile * nt, nt), :]
        v = vmem.at[slot, :, b, :]
        src, dst = (h, v) if to_vmem else (v, h)
        copies.append(pltpu.make_async_copy(src, dst, sems.at[slot, b]))
    return copies


def _load_cast(w_hbm, dst, stage, sems, chunk):
    n = dst.shape[0] // chunk
    ahead = STAGE_SLOTS - 1

    def copy(c, s):
        return pltpu.make_async_copy(w_hbm.at[0, pl.ds(c * chunk, chunk), :],
                                     stage.at[pl.ds(s * chunk, chunk), :], sems.at[s])

    for c in range(min(ahead, n)):
        copy(c, c).start()

    def body(c, carry):
        s = lax.rem(c, STAGE_SLOTS)

        @pl.when(c + ahead < n)
        def _():
            copy(c + ahead, lax.rem(c + ahead, STAGE_SLOTS)).start()

        copy(c, s).wait()
        src = pl.multiple_of(s * chunk, chunk)
        row = pl.multiple_of(c * chunk, chunk)
        dst[pl.ds(row, chunk), :] = stage[pl.ds(src, chunk), :].astype(dst.dtype)
        return carry

    lax.fori_loop(0, n, body, 0)


def _expand_block_diag(src, dst, block_rows, block_lanes, n_blocks):
    lane = lax.broadcasted_iota(jnp.int32, src.shape, 1)
    lane_group = lax.rem(lane, n_blocks * block_lanes) // block_lanes
    for g in range(n_blocks):
        dst[g * block_rows:(g + 1) * block_rows, :] = jnp.where(lane_group == g, src, 0.0).astype(dst.dtype)


def _block_kernel(x_hbm, meta_ref, n1g_ref, a4r_ref, a4i_ref, zc_ref, vc_ref, locc_ref, d_ref, gluc_ref,
                  gb_ref, sng_ref, pw_ref, ps_ref, png_ref, n2g_ref, fg_ref,
                  win_hbm, wout_hbm, wg_hbm, wu_hbm, wd_hbm, o_hbm,
                  xbuf, xsem, obuf, osem, hbuf, zr, zi, st_r, st_i, u_last, halo, act,
                  win_ref, wout_ref, wg_ref, wu_ref, wd_ref, wz_ref, v_ref, loc_ref, gw_ref,
                  narrow_stage, wide_stage, wsem):
    i = pl.program_id(0)
    n_tiles = pl.num_programs(0) - 1
    slot = lax.rem(i, 2)
    oslot = 1 - slot

    def fetch(tile, s):
        return _tile_copies(x_hbm, xbuf, xsem, s, tile, T_TILE, to_vmem=True)

    def writeback(tile, s):
        return _tile_copies(o_hbm, obuf, osem, s, tile, T_TILE, to_vmem=False)

    d_ssm = d_ref.shape[-1]
    half_in = HALF_GROUPS * SSM_GROUP
    quad_in = QUAD_GROUPS * SSM_GROUP
    quad_state = QUAD_GROUPS * SSM_STATE
    n_quads = d_ssm // quad_in
    n_halves = d_ssm // half_in
    halo_rows = halo.shape[0]
    n_ff_chunks = wg_ref.shape[-1] // FF_CHUNK

    def build_weights():
        _load_cast(win_hbm, win_ref, narrow_stage, wsem, NARROW_STAGE_ROWS)
        _load_cast(wout_hbm, wout_ref, narrow_stage, wsem, NARROW_STAGE_ROWS)
        _load_cast(wd_hbm, wd_ref, narrow_stage, wsem, NARROW_STAGE_ROWS)
        _load_cast(wg_hbm, wg_ref, wide_stage, wsem, WIDE_STAGE_ROWS)
        _load_cast(wu_hbm, wu_ref, wide_stage, wsem, WIDE_STAGE_ROWS)
        ph_rows = QUAD_GROUPS * SSM_GROUP
        for q in range(n_quads):
            for ri in range(2):
                for ph in range(PHASES):
                    _expand_block_diag(zc_ref[ri, ph, q], wz_ref.at[ri, q, ph * ph_rows:(ph + 1) * ph_rows, :],
                                       SSM_GROUP, SSM_STATE, QUAD_GROUPS)
                _expand_block_diag(vc_ref[ri, q], v_ref.at[ri, q], SSM_STATE, SSM_GROUP, QUAD_GROUPS)
            for ph in range(PHASES):
                _expand_block_diag(locc_ref[ph, q], loc_ref.at[q, ph * ph_rows:(ph + 1) * ph_rows, :],
                                   SSM_GROUP, SSM_GROUP, QUAD_GROUPS)
        for hh in range(n_halves):
            _expand_block_diag(gluc_ref[hh], gw_ref.at[hh], SSM_GROUP, SSM_GROUP, HALF_GROUPS)

    def project(rows):
        n1 = _rms(rows, n1g_ref[...]).astype(_BF16)
        return _dot(n1, win_ref[...])

    def quad_lanes(parts, q):
        return jnp.concatenate([p[:, q * quad_in:(q + 1) * quad_in] for p in parts], axis=1)

    def s5_states(u_in, n_chunks, qq):
        rows = n_chunks * BATCH_ROWS
        for q in range(qq * SCAN_QUADS, (qq + 1) * SCAN_QUADS):
            qs = slice(q * quad_state, (q + 1) * quad_state)
            lhs = quad_lanes(u_in, q)
            zr[0:rows, qs] = _dot(lhs, wz_ref[0, q])
            zi[0:rows, qs] = _dot(lhs, wz_ref[1, q])
        cs = slice(qq * SCAN_QUADS * quad_state, (qq + 1) * SCAN_QUADS * quad_state)
        width = SCAN_QUADS * quad_state
        ar = jnp.broadcast_to(a4r_ref[:, cs], (BATCH_ROWS, width))
        ai = jnp.broadcast_to(a4i_ref[:, cs], (BATCH_ROWS, width))
        sr = st_r[:, cs]
        si = st_i[:, cs]
        for k in range(n_chunks):
            rs = slice(k * BATCH_ROWS, (k + 1) * BATCH_ROWS)
            sr, si = (ar * sr - ai * si + zr[rs, cs], ar * si + ai * sr + zi[rs, cs])
            zr[rs, cs] = sr
            zi[rs, cs] = si
        st_r[:, cs] = sr
        st_i[:, cs] = si

    def s5_inputs(u, n_chunks):
        parts = _split_phases(u, n_chunks)
        state_in = []
        for p in range(1, PHASES):
            carry = slice((p - 1) * BATCH_ROWS, p * BATCH_ROWS)
            state_in.append(jnp.concatenate([u_last[carry, :], parts[p][:-BATCH_ROWS]], axis=0).astype(_BF16))
            u_last[carry, :] = parts[p][-BATCH_ROWS:]
        local_in = [p.astype(_BF16) for p in parts]
        return state_in + [local_in[0]], local_in

    def ffn_chunk(n2, c):
        cs = slice(c * FF_CHUNK, (c + 1) * FF_CHUNK)
        gate = _dot(n2, wg_ref[:, cs])
        up = _dot(n2, wu_ref[:, cs])
        act[:, cs] = (gate * jax.nn.sigmoid(gate) * up).astype(_BF16)

    def step(with_mixer, with_ffn):
        rows = T_TILE * BATCH_ROWS
        n_chunks = T_TILE // PHASES
        crows = n_chunks * BATCH_ROWS
        ff = iter(range(n_ff_chunks if with_ffn else 0))

        def ffn_chunks(n):
            for _ in range(n):
                c = next(ff, None)
                if c is not None:
                    ffn_chunk(n2, c)

        def ffn_tail():
            ffn_chunks(n_ff_chunks)
            out = _rms(h + _dot(act[...], wd_ref[...]), fg_ref[...])
            obuf[oslot] = out.reshape(obuf.shape[1:])

        if with_ffn:
            h = hbuf[...]
        if not with_mixer:
            n2 = _rms(h, n2g_ref[...]).astype(_BF16)
            ffn_tail()
            return

        x = xbuf[slot].reshape(rows, xbuf.shape[-1])
        n1 = _rms(x, n1g_ref[...]).astype(_BF16)
        proj = _dot(n1, win_ref[...])
        if with_ffn:
            n2 = _rms(h, n2g_ref[...]).astype(_BF16)
        u = proj[:, :d_ssm]
        v = proj[:, d_ssm:]

        state_in, local_in = s5_inputs(u, n_chunks)
        for qq in range(n_quads // SCAN_QUADS):
            s5_states(state_in, n_chunks, qq)
            ffn_chunks(1)
        chunk_out = []
        for q in range(n_quads):
            qs = slice(q * quad_state, (q + 1) * quad_state)
            chunk_out.append(_dot(zr[0:crows, qs].astype(_BF16), v_ref[0, q])
                             + _dot(zi[0:crows, qs].astype(_BF16), v_ref[1, q])
                             + _dot(quad_lanes(local_in, q), loc_ref[q]))
            if q % 4 == 3:
                ffn_chunks(1)
        y_parts = [jnp.concatenate([c[:, p * quad_in:(p + 1) * quad_in] for c in chunk_out], axis=-1)
                   for p in range(PHASES)]
        y = _merge_phases(y_parts, n_chunks) + d_ref[...] * u
        g = jax.nn.gelu(y)
        gb = g.astype(_BF16)
        ffn_chunks(1)
        gate = jnp.concatenate([_dot(gb[:, hh * half_in:(hh + 1) * half_in], gw_ref[hh])
                                for hh in range(n_halves)], axis=-1) + gb_ref[...]
        y_ssm = _rms(g * jax.nn.sigmoid(gate), sng_ref[...])
        ffn_chunks(1)

        vext = jnp.concatenate([halo[...], v], axis=0)
        halo[...] = v[rows - halo_rows:, :]
        gd = v.shape[-1] // len(POOL_WINDOWS)
        yps = []
        for k, w in enumerate(POOL_WINDOWS):
            acc = vext[:, k * gd:(k + 1) * gd]
            span = 1
            while span < w:
                shift = span * BATCH_ROWS
                acc = acc[shift:, :] + acc[:-shift, :]
                span *= 2
            pk = acc[acc.shape[0] - rows:, :] * (1.0 / w) - v[:, k * gd:(k + 1) * gd]
            yps.append(_dot(pk.astype(_BF16), pw_ref[k]))
        ffn_chunks(1)
        y_pool = _rms(jnp.concatenate(yps, axis=-1) * ps_ref[...], png_ref[...])

        mixed = jnp.concatenate([y_ssm, y_pool], axis=-1).astype(_BF16)
        if with_ffn:
            ffn_tail()
        hbuf[...] = x + _dot(mixed, wout_ref[...])

    @pl.when(i == 0)
    def _():
        for c in fetch(0, 0):
            c.start()

    @pl.when(i + 1 < n_tiles)
    def _():
        for c in fetch(i + 1, oslot):
            c.start()

    @pl.when(i >= 3)
    def _():
        for c in writeback(i - 3, oslot):
            c.wait()

    @pl.when(i == 0)
    def _():
        build_weights()
        st_r[...] = jnp.zeros_like(st_r)
        st_i[...] = jnp.zeros_like(st_i)
        u_last[...] = jnp.zeros_like(u_last)
        pm = project(meta_ref[...])
        state_in, _ = s5_inputs(pm[:, :d_ssm], N_META // PHASES)
        for qq in range(n_quads // SCAN_QUADS):
            s5_states(state_in, N_META // PHASES, qq)
        halo[...] = pm[:, d_ssm:]

    @pl.when(i < n_tiles)
    def _():
        for c in fetch(i, slot):
            c.wait()

    @pl.when(i == 0)
    def _():
        step(with_mixer=True, with_ffn=False)

    @pl.when(jnp.logical_and(i > 0, i < n_tiles))
    def _():
        step(with_mixer=True, with_ffn=True)

    @pl.when(i == n_tiles)
    def _():
        step(with_mixer=False, with_ffn=True)

    @pl.when(i >= 1)
    def _():
        for c in writeback(i - 1, oslot):
            c.start()

    @pl.when(i == n_tiles)
    def _():
        @pl.when(n_tiles >= 2)
        def _():
            for c in writeback(i - 2, slot):
                c.wait()
        for c in writeback(i - 1, oslot):
            c.wait()


def _const_spec(a):
    nd = a.ndim
    return pl.BlockSpec(a.shape, lambda i: (0,) * nd, pipeline_mode=pl.Buffered(1))


def _block(x, consts, hbm_weights, n_state, d_ssm, d_pool, d_ff):
    B, S, D = x.shape
    rows = T_TILE * B
    any_spec = pl.BlockSpec(memory_space=pl.ANY)
    n_quad = d_ssm // (QUAD_GROUPS * SSM_GROUP)
    n_half = d_ssm // (HALF_GROUPS * SSM_GROUP)
    tile_k = PHASES * QUAD_GROUPS * SSM_GROUP
    quad_state = QUAD_GROUPS * SSM_STATE
    return pl.pallas_call(
        _block_kernel,
        grid=(S // T_TILE + 1,),
        in_specs=[any_spec] + [_const_spec(c) for c in consts] + [any_spec] * len(hbm_weights),
        out_specs=any_spec,
        out_shape=jax.ShapeDtypeStruct((B, S, D), _F32),
        scratch_shapes=[
            pltpu.VMEM((2, T_TILE, B, D), _F32),
            pltpu.SemaphoreType.DMA((2, BATCH_ROWS)),
            pltpu.VMEM((2, T_TILE, B, D), _F32),
            pltpu.SemaphoreType.DMA((2, BATCH_ROWS)),
            pltpu.VMEM((rows, D), _F32),
            pltpu.VMEM((rows // PHASES, n_state), _F32),
            pltpu.VMEM((rows // PHASES, n_state), _F32),
            pltpu.VMEM((B, n_state), _F32),
            pltpu.VMEM((B, n_state), _F32),
            pltpu.VMEM(((PHASES - 1) * B, d_ssm), _F32),
            pltpu.VMEM((N_META * B, d_pool), _F32),
            pltpu.VMEM((rows, d_ff), _BF16),
            pltpu.VMEM((D, D), _BF16),
            pltpu.VMEM((D, D), _BF16),
            pltpu.VMEM((D, d_ff), _BF16),
            pltpu.VMEM((D, d_ff), _BF16),
            pltpu.VMEM((d_ff, D), _BF16),
            pltpu.VMEM((2, n_quad, tile_k, quad_state), _BF16),
            pltpu.VMEM((2, n_quad, quad_state, tile_k), _BF16),
            pltpu.VMEM((n_quad, tile_k, tile_k), _BF16),
            pltpu.VMEM((n_half, HALF_GROUPS * SSM_GROUP, HALF_GROUPS * SSM_GROUP), _BF16),
            pltpu.VMEM((STAGE_SLOTS * NARROW_STAGE_ROWS, D), _F32),
            pltpu.VMEM((STAGE_SLOTS * WIDE_STAGE_ROWS, d_ff), _F32),
            pltpu.SemaphoreType.DMA((STAGE_SLOTS,)),
        ],
        compiler_params=pltpu.CompilerParams(dimension_semantics=("arbitrary",),
                                             vmem_limit_bytes=VMEM_LIMIT),
        name="block",
    )(x, *consts, *hbm_weights)


def _lanes_by_group(blocks, per):
    G, r, c = blocks.shape
    return jnp.transpose(blocks.reshape(G // per, per, r, c), (0, 2, 1, 3)).reshape(G // per, r, per * c)


def _cmul(ar, ai, br, bi):
    return ar * br - ai * bi, ar * bi + ai * br


def _s5_params(lam_re, lam_im, log_step, b_re, b_im, c_re, c_im):
    lr = jnp.minimum(lam_re, -1e-4)
    li = lam_im
    step = jnp.exp(log_step)[:, None]
    mag = jnp.exp(lr * step)
    ang = li * step
    abr = mag * jnp.cos(ang)
    abi = mag * jnp.sin(ang)
    nr = abr - 1.0
    ni = abi
    den = lr * lr + li * li
    cr = ((nr * lr + ni * li) / den)[..., None]
    ci = ((ni * lr - nr * li) / den)[..., None]
    bbr = cr * b_re - ci * b_im
    bbi = cr * b_im + ci * b_re
    pw = [(jnp.ones_like(abr), jnp.zeros_like(abi))]
    for _ in range(PHASES):
        pw.append(_cmul(pw[-1][0], pw[-1][1], abr, abi))
    t = lambda a: jnp.swapaxes(a, 1, 2)
    by_quad = lambda a: _lanes_by_group(a, QUAD_GROUPS)
    ab = [_cmul(pr[..., None], pi[..., None], bbr, bbi) for pr, pi in pw[:PHASES]]
    ca = [_cmul(c_re, c_im, pr[:, None, :], pi[:, None, :]) for pr, pi in pw[:PHASES]]
    zc = jnp.stack([jnp.stack([by_quad(t(ab[PHASES - 1 - ph][ri])) for ph in range(PHASES)]) for ri in range(2)])
    sign = (1.0, -1.0)
    vc = jnp.stack([jnp.concatenate([by_quad(sign[ri] * t(ca[ph][ri])) for ph in range(PHASES)], axis=-1)
                    for ri in range(2)])
    loc = [jnp.einsum('ghp,gpk->ghk', ca[d][0], bbr) - jnp.einsum('ghp,gpk->ghk', ca[d][1], bbi)
           for d in range(PHASES - 1)]
    zero = jnp.zeros_like(by_quad(t(loc[0])))
    locc = jnp.stack([jnp.concatenate([by_quad(t(loc[p - j])) if 1 <= j <= p else zero for p in range(PHASES)],
                                      axis=-1) for j in range(PHASES)])
    return pw[PHASES], zc, vc, locc


def kernel(x, meta_tokens, norm1_g, w_in, ssm_lambda_re, ssm_lambda_im, ssm_log_step, ssm_b_re, ssm_b_im, ssm_c_re, ssm_c_im, ssm_d, ssm_glu_w, ssm_glu_b, ssm_norm_g, pool_w, pool_scale, pool_norm_g, w_out, norm2_g, w_gate, w_up, w_down, final_norm_g):
    B, S, D = x.shape
    assert B == BATCH_ROWS and norm1_g.shape[0] == 1
    assert S % T_TILE == 0 and T_TILE % PHASES == 0 and N_META % PHASES == 0
    f32 = _F32
    (a4r, a4i), zc, vc, locc = _s5_params(
        ssm_lambda_re[0].astype(f32), ssm_lambda_im[0].astype(f32), ssm_log_step[0].astype(f32),
        ssm_b_re[0].astype(f32), ssm_b_im[0].astype(f32), ssm_c_re[0].astype(f32), ssm_c_im[0].astype(f32))
    row = lambda a: a.astype(f32).reshape(1, -1)
    consts = (
        jnp.repeat(meta_tokens.astype(f32), B, axis=0),
        row(norm1_g[0]),
        row(a4r), row(a4i),
        zc,
        vc,
        locc,
        row(ssm_d[0]),
        _lanes_by_group(ssm_glu_w[0].astype(f32), HALF_GROUPS),
        row(ssm_glu_b[0]),
        row(ssm_norm_g[0]),
        pool_w[0].astype(_BF16),
        row(pool_scale[0]),
        row(pool_norm_g[0]),
        row(norm2_g[0]),
        row(final_norm_g),
    )
    hbm_weights = tuple(w.astype(f32) for w in (w_in, w_out, w_gate, w_up, w_down))
    out = _block(x.astype(f32), consts, hbm_weights, n_state=a4r.size, d_ssm=ssm_d[0].size,
                 d_pool=pool_scale[0].size, d_ff=w_gate.shape[-1])
    return out.astype(x.dtype)
```

```python
import jax
import jax.numpy as jnp
from jax import lax
from jax.experimental import pallas as pl
from jax.experimental.pallas import tpu as pltpu

N_META = 16
SSM_GROUP = 16
SSM_STATE = 64
POOL_WINDOWS = (2, 4, 8, 16)
EPS = 1e-6

BATCH_ROWS = 8
HALF_GROUPS = 16
PHASES = 4
QUAD_GROUPS = 4
SCAN_QUADS = 2
T_TILE = 64
FF_CHUNK = 256
STAGE_SLOTS = 8
WIDE_STAGE_ROWS = 32
NARROW_STAGE_ROWS = 128
VMEM_LIMIT = 60 * 1024 * 1024

_BF16 = jnp.bfloat16
_F32 = jnp.float32


def _rms(x, g):
    return x * lax.rsqrt(jnp.mean(x * x, axis=-1, keepdims=True) + EPS) * g


def _dot(a, b):
    return jnp.dot(a, b, preferred_element_type=_F32)


def _split_phases(a, n_chunks):
    blk = lambda t: a[t * BATCH_ROWS:(t + 1) * BATCH_ROWS]
    return [jnp.concatenate([blk(PHASES * k + p) for k in range(n_chunks)], axis=0) for p in range(PHASES)]


def _merge_phases(parts, n_chunks):
    blk = lambda a, k: a[k * BATCH_ROWS:(k + 1) * BATCH_ROWS]
    return jnp.concatenate([blk(p, k) for k in range(n_chunks) for p in parts], axis=0)


def _tile_copies(hbm, vmem, sems, slot, tile, nt, to_vmem):
    copies = []
    for b in range(BATCH_ROWS):
        h = hbm.at[b, pl.ds(tile * nt, nt), :]
        v = vmem.at[slot, :, b, :]
        src, dst = (h, v) if to_vmem else (v, h)
        copies.append(pltpu.make_async_copy(src, dst, sems.at[slot, b]))
    return copies


def _load_cast(w_hbm, dst, stage, sems, chunk, convert=None):
    if convert is None:
        convert = lambda rows: rows.astype(dst.dtype)
    n = dst.shape[0] // chunk
    ahead = STAGE_SLOTS - 1

    def copy(c, s):
        return pltpu.make_async_copy(w_hbm.at[0, pl.ds(c * chunk, chunk), :],
                                     stage.at[pl.ds(s * chunk, chunk), :], sems.at[s])

    for c in range(min(ahead, n)):
        copy(c, c).start()

    def body(c, carry):
        s = lax.rem(c, STAGE_SLOTS)

        @pl.when(c + ahead < n)
        def _():
            copy(c + ahead, lax.rem(c + ahead, STAGE_SLOTS)).start()

        copy(c, s).wait()
        src = pl.multiple_of(s * chunk, chunk)
        row = pl.multiple_of(c * chunk, chunk)
        dst[pl.ds(row, chunk), :] = convert(stage[pl.ds(src, chunk), :])
        return carry

    lax.fori_loop(0, n, body, 0)


def _expand_block_diag(src, dst, block_rows, block_lanes, n_blocks):
    lane = lax.broadcasted_iota(jnp.int32, src.shape, 1)
    lane_group = lax.rem(lane, n_blocks * block_lanes) // block_lanes
    for g in range(n_blocks):
        dst[g * block_rows:(g + 1) * block_rows, :] = jnp.where(lane_group == g, src, 0.0).astype(dst.dtype)


def _block_kernel(x_hbm, meta_ref, n1g_ref, a4r_ref, a4i_ref, zc_ref, vc_ref, locc_ref, d_ref, gluc_ref,
                  gb_ref, sng_ref, pw_ref, ps_ref, png_ref, n2g_ref, fg_ref,
                  win_hbm, wout_hbm, wg_hbm, wu_hbm, wd_hbm, o_hbm,
                  xbuf, xsem, obuf, osem, hbuf, zr, zi, st_r, st_i, u_last, halo, act,
                  win_ref, wout_ref, wg_ref, wu_ref, wd_ref, wz_ref, v_ref, loc_ref, gw_ref,
                  narrow_stage, wide_stage, wsem):
    i = pl.program_id(0)
    n_tiles = pl.num_programs(0) - 1
    slot = lax.rem(i, 2)
    oslot = 1 - slot

    def fetch(tile, s):
        return _tile_copies(x_hbm, xbuf, xsem, s, tile, T_TILE, to_vmem=True)

    def writeback(tile, s):
        return _tile_copies(o_hbm, obuf, osem, s, tile, T_TILE, to_vmem=False)

    d_ssm = d_ref.shape[-1]
    half_in = HALF_GROUPS * SSM_GROUP
    quad_in = QUAD_GROUPS * SSM_GROUP
    quad_state = QUAD_GROUPS * SSM_STATE
    n_quads = d_ssm // quad_in
    n_halves = d_ssm // half_in
    halo_rows = halo.shape[0]
    n_ff_chunks = wg_ref.shape[-1] // FF_CHUNK

    def build_weights():
        gd = pw_ref.shape[-1]
        pool_maps = [pw_ref[k] * ps_ref[:, k * gd:(k + 1) * gd] for k in range(pw_ref.shape[0])]

        def fold_pool(rows):
            cols = [rows[:, :d_ssm]]
            for k, pm in enumerate(pool_maps):
                cols.append(jnp.dot(rows[:, d_ssm + k * gd:d_ssm + (k + 1) * gd], pm,
                                    precision=lax.Precision.HIGHEST, preferred_element_type=_F32))
            return jnp.concatenate(cols, axis=-1).astype(_BF16)

        _load_cast(win_hbm, win_ref, narrow_stage, wsem, NARROW_STAGE_ROWS, convert=fold_pool)
        _load_cast(wout_hbm, wout_ref, narrow_stage, wsem, NARROW_STAGE_ROWS)
        _load_cast(wd_hbm, wd_ref, narrow_stage, wsem, NARROW_STAGE_ROWS)
        _load_cast(wg_hbm, wg_ref, wide_stage, wsem, WIDE_STAGE_ROWS)
        _load_cast(wu_hbm, wu_ref, wide_stage, wsem, WIDE_STAGE_ROWS)
        ph_rows = QUAD_GROUPS * SSM_GROUP
        for q in range(n_quads):
            for ri in range(2):
                for ph in range(PHASES):
                    _expand_block_diag(zc_ref[ri, ph, q], wz_ref.at[ri, q, ph * ph_rows:(ph + 1) * ph_rows, :],
                                       SSM_GROUP, SSM_STATE, QUAD_GROUPS)
                _expand_block_diag(vc_ref[ri, q], v_ref.at[ri, q], SSM_STATE, SSM_GROUP, QUAD_GROUPS)
            for ph in range(PHASES):
                _expand_block_diag(locc_ref[ph, q], loc_ref.at[q, ph * ph_rows:(ph + 1) * ph_rows, :],
                                   SSM_GROUP, SSM_GROUP, QUAD_GROUPS)
        for hh in range(n_halves):
            _expand_block_diag(gluc_ref[hh], gw_ref.at[hh], SSM_GROUP, SSM_GROUP, HALF_GROUPS)

    def project(rows):
        n1 = _rms(rows, n1g_ref[...]).astype(_BF16)
        return _dot(n1, win_ref[...])

    def quad_lanes(parts, q):
        return jnp.concatenate([p[:, q * quad_in:(q + 1) * quad_in] for p in parts], axis=1)

    def s5_states(u_in, n_chunks, qq):
        rows = n_chunks * BATCH_ROWS
        for q in range(qq * SCAN_QUADS, (qq + 1) * SCAN_QUADS):
            qs = slice(q * quad_state, (q + 1) * quad_state)
            lhs = quad_lanes(u_in, q)
            zr[0:rows, qs] = _dot(lhs, wz_ref[0, q])
            zi[0:rows, qs] = _dot(lhs, wz_ref[1, q])
        cs = slice(qq * SCAN_QUADS * quad_state, (qq + 1) * SCAN_QUADS * quad_state)
        width = SCAN_QUADS * quad_state
        ar = jnp.broadcast_to(a4r_ref[:, cs], (BATCH_ROWS, width))
        ai = jnp.broadcast_to(a4i_ref[:, cs], (BATCH_ROWS, width))
        sr = st_r[:, cs]
        si = st_i[:, cs]
        for k in range(n_chunks):
            rs = slice(k * BATCH_ROWS, (k + 1) * BATCH_ROWS)
            sr, si = (ar * sr - ai * si + zr[rs, cs], ar * si + ai * sr + zi[rs, cs])
            zr[rs, cs] = sr
            zi[rs, cs] = si
        st_r[:, cs] = sr
        st_i[:, cs] = si

    def s5_inputs(u, n_chunks):
        parts = _split_phases(u, n_chunks)
        state_in = []
        for p in range(1, PHASES):
            carry = slice((p - 1) * BATCH_ROWS, p * BATCH_ROWS)
            state_in.append(jnp.concatenate([u_last[carry, :], parts[p][:-BATCH_ROWS]], axis=0).astype(_BF16))
            u_last[carry, :] = parts[p][-BATCH_ROWS:]
        local_in = [p.astype(_BF16) for p in parts]
        return state_in + [local_in[0]], local_in

    def ffn_chunk(n2, c):
        cs = slice(c * FF_CHUNK, (c + 1) * FF_CHUNK)
        gate = _dot(n2, wg_ref[:, cs])
        up = _dot(n2, wu_ref[:, cs])
        act[:, cs] = (gate * jax.nn.sigmoid(gate) * up).astype(_BF16)

    def step(with_mixer, with_ffn):
        rows = T_TILE * BATCH_ROWS
        n_chunks = T_TILE // PHASES
        crows = n_chunks * BATCH_ROWS
        ff = iter(range(n_ff_chunks if with_ffn else 0))

        def ffn_chunks(n):
            for _ in range(n):
                c = next(ff, None)
                if c is not None:
                    ffn_chunk(n2, c)

        def ffn_tail():
            ffn_chunks(n_ff_chunks)
            out = _rms(h + _dot(act[...], wd_ref[...]), fg_ref[...])
            obuf[oslot] = out.reshape(obuf.shape[1:])

        if with_ffn:
            h = hbuf[...]
        if not with_mixer:
            n2 = _rms(h, n2g_ref[...]).astype(_BF16)
            ffn_tail()
            return

        x = xbuf[slot].reshape(rows, xbuf.shape[-1])
        n1 = _rms(x, n1g_ref[...]).astype(_BF16)
        proj = _dot(n1, win_ref[...])
        if with_ffn:
            n2 = _rms(h, n2g_ref[...]).astype(_BF16)
        u = proj[:, :d_ssm]
        v = proj[:, d_ssm:]

        state_in, local_in = s5_inputs(u, n_chunks)
        for qq in range(n_quads // SCAN_QUADS):
            s5_states(state_in, n_chunks, qq)
            ffn_chunks(1)
        chunk_out = []
        for q in range(n_quads):
            qs = slice(q * quad_state, (q + 1) * quad_state)
            chunk_out.append(_dot(zr[0:crows, qs].astype(_BF16), v_ref[0, q])
                             + _dot(zi[0:crows, qs].astype(_BF16), v_ref[1, q])
                             + _dot(quad_lanes(local_in, q), loc_ref[q]))
            if q % 4 == 3:
                ffn_chunks(1)
        y_parts = [jnp.concatenate([c[:, p * quad_in:(p + 1) * quad_in] for c in chunk_out], axis=-1)
                   for p in range(PHASES)]
        y = _merge_phases(y_parts, n_chunks) + d_ref[...] * u
        g = jax.nn.gelu(y)
        gb = g.astype(_BF16)
        ffn_chunks(1)
        gate = jnp.concatenate([_dot(gb[:, hh * half_in:(hh + 1) * half_in], gw_ref[hh])
                                for hh in range(n_halves)], axis=-1) + gb_ref[...]
        y_ssm = _rms(g * jax.nn.sigmoid(gate), sng_ref[...])
        ffn_chunks(1)

        vext = jnp.concatenate([halo[...], v], axis=0)
        halo[...] = v[rows - halo_rows:, :]
        gd = v.shape[-1] // len(POOL_WINDOWS)
        yps = []
        for k, w in enumerate(POOL_WINDOWS):
            acc = vext[:, k * gd:(k + 1) * gd]
            span = 1
            while span < w:
                shift = span * BATCH_ROWS
                acc = acc[shift:, :] + acc[:-shift, :]
                span *= 2
            yps.append(acc[acc.shape[0] - rows:, :] * (1.0 / w) - v[:, k * gd:(k + 1) * gd])
        ffn_chunks(1)
        y_pool = _rms(jnp.concatenate(yps, axis=-1), png_ref[...])

        mixed = jnp.concatenate([y_ssm, y_pool], axis=-1).astype(_BF16)
        if with_ffn:
            ffn_tail()
        hbuf[...] = x + _dot(mixed, wout_ref[...])

    @pl.when(i == 0)
    def _():
        for c in fetch(0, 0):
            c.start()

    @pl.when(i + 1 < n_tiles)
    def _():
        for c in fetch(i + 1, oslot):
            c.start()

    @pl.when(i >= 3)
    def _():
        for c in writeback(i - 3, oslot):
            c.wait()

    @pl.when(i == 0)
    def _():
        build_weights()
        st_r[...] = jnp.zeros_like(st_r)
        st_i[...] = jnp.zeros_like(st_i)
        u_last[...] = jnp.zeros_like(u_last)
        pm = project(meta_ref[...])
        state_in, _ = s5_inputs(pm[:, :d_ssm], N_META // PHASES)
        for qq in range(n_quads // SCAN_QUADS):
            s5_states(state_in, N_META // PHASES, qq)
        halo[...] = pm[:, d_ssm:]

    @pl.when(i < n_tiles)
    def _():
        for c in fetch(i, slot):
            c.wait()

    @pl.when(i == 0)
    def _():
        step(with_mixer=True, with_ffn=False)

    @pl.when(jnp.logical_and(i > 0, i < n_tiles))
    def _():
        step(with_mixer=True, with_ffn=True)

    @pl.when(i == n_tiles)
    def _():
        step(with_mixer=False, with_ffn=True)

    @pl.when(i >= 1)
    def _():
        for c in writeback(i - 1, oslot):
            c.start()

    @pl.when(i == n_tiles)
    def _():
        @pl.when(n_tiles >= 2)
        def _():
            for c in writeback(i - 2, slot):
                c.wait()
        for c in writeback(i - 1, oslot):
            c.wait()


def _const_spec(a):
    nd = a.ndim
    return pl.BlockSpec(a.shape, lambda i: (0,) * nd, pipeline_mode=pl.Buffered(1))


def _block(x, consts, hbm_weights, n_state, d_ssm, d_pool, d_ff):
    B, S, D = x.shape
    rows = T_TILE * B
    any_spec = pl.BlockSpec(memory_space=pl.ANY)
    n_quad = d_ssm // (QUAD_GROUPS * SSM_GROUP)
    n_half = d_ssm // (HALF_GROUPS * SSM_GROUP)
    tile_k = PHASES * QUAD_GROUPS * SSM_GROUP
    quad_state = QUAD_GROUPS * SSM_STATE
    return pl.pallas_call(
        _block_kernel,
        grid=(S // T_TILE + 1,),
        in_specs=[any_spec] + [_const_spec(c) for c in consts] + [any_spec] * len(hbm_weights),
        out_specs=any_spec,
        out_shape=jax.ShapeDtypeStruct((B, S, D), _F32),
        scratch_shapes=[
            pltpu.VMEM((2, T_TILE, B, D), _F32),
            pltpu.SemaphoreType.DMA((2, BATCH_ROWS)),
            pltpu.VMEM((2, T_TILE, B, D), _F32),
            pltpu.SemaphoreType.DMA((2, BATCH_ROWS)),
            pltpu.VMEM((rows, D), _F32),
            pltpu.VMEM((rows // PHASES, n_state), _F32),
            pltpu.VMEM((rows // PHASES, n_state), _F32),
            pltpu.VMEM((B, n_state), _F32),
            pltpu.VMEM((B, n_state), _F32),
            pltpu.VMEM(((PHASES - 1) * B, d_ssm), _F32),
            pltpu.VMEM((N_META * B, d_pool), _F32),
            pltpu.VMEM((rows, d_ff), _BF16),
            pltpu.VMEM((D, D), _BF16),
            pltpu.VMEM((D, D), _BF16),
            pltpu.VMEM((D, d_ff), _BF16),
            pltpu.VMEM((D, d_ff), _BF16),
            pltpu.VMEM((d_ff, D), _BF16),
            pltpu.VMEM((2, n_quad, tile_k, quad_state), _BF16),
            pltpu.VMEM((2, n_quad, quad_state, tile_k), _BF16),
            pltpu.VMEM((n_quad, tile_k, tile_k), _BF16),
            pltpu.VMEM((n_half, HALF_GROUPS * SSM_GROUP, HALF_GROUPS * SSM_GROUP), _BF16),
            pltpu.VMEM((STAGE_SLOTS * NARROW_STAGE_ROWS, D), _F32),
            pltpu.VMEM((STAGE_SLOTS * WIDE_STAGE_ROWS, d_ff), _F32),
            pltpu.SemaphoreType.DMA((STAGE_SLOTS,)),
        ],
        compiler_params=pltpu.CompilerParams(dimension_semantics=("arbitrary",),
                                             vmem_limit_bytes=VMEM_LIMIT),
        name="block",
    )(x, *consts, *hbm_weights)


def _lanes_by_group(blocks, per):
    G, r, c = blocks.shape
    return jnp.transpose(blocks.reshape(G // per, per, r, c), (0, 2, 1, 3)).reshape(G // per, r, per * c)


def _cmul(ar, ai, br, bi):
    return ar * br - ai * bi, ar * bi + ai * br


def _s5_params(lam_re, lam_im, log_step, b_re, b_im, c_re, c_im):
    lr = jnp.minimum(lam_re, -1e-4)
    li = lam_im
    step = jnp.exp(log_step)[:, None]
    mag = jnp.exp(lr * step)
    ang = li * step
    abr = mag * jnp.cos(ang)
    abi = mag * jnp.sin(ang)
    nr = abr - 1.0
    ni = abi
    den = lr * lr + li * li
    cr = ((nr * lr + ni * li) / den)[..., None]
    ci = ((ni * lr - nr * li) / den)[..., None]
    bbr = cr * b_re - ci * b_im
    bbi = cr * b_im + ci * b_re
    pw = [(jnp.ones_like(abr), jnp.zeros_like(abi))]
    for _ in range(PHASES):
        pw.append(_cmul(pw[-1][0], pw[-1][1], abr, abi))
    t = lambda a: jnp.swapaxes(a, 1, 2)
    by_quad = lambda a: _lanes_by_group(a, QUAD_GROUPS)
    ab = [_cmul(pr[..., None], pi[..., None], bbr, bbi) for pr, pi in pw[:PHASES]]
    ca = [_cmul(c_re, c_im, pr[:, None, :], pi[:, None, :]) for pr, pi in pw[:PHASES]]
    zc = jnp.stack([jnp.stack([by_quad(t(ab[PHASES - 1 - ph][ri])) for ph in range(PHASES)]) for ri in range(2)])
    sign = (1.0, -1.0)
    vc = jnp.stack([jnp.concatenate([by_quad(sign[ri] * t(ca[ph][ri])) for ph in range(PHASES)], axis=-1)
                    for ri in range(2)])
    loc = [jnp.einsum('ghp,gpk->ghk', ca[d][0], bbr) - jnp.einsum('ghp,gpk->ghk', ca[d][1], bbi)
           for d in range(PHASES - 1)]
    zero = jnp.zeros_like(by_quad(t(loc[0])))
    locc = jnp.stack([jnp.concatenate([by_quad(t(loc[p - j])) if 1 <= j <= p else zero for p in range(PHASES)],
                                      axis=-1) for j in range(PHASES)])
    return pw[PHASES], zc, vc, locc


def kernel(x, meta_tokens, norm1_g, w_in, ssm_lambda_re, ssm_lambda_im, ssm_log_step, ssm_b_re, ssm_b_im, ssm_c_re, ssm_c_im, ssm_d, ssm_glu_w, ssm_glu_b, ssm_norm_g, pool_w, pool_scale, pool_norm_g, w_out, norm2_g, w_gate, w_up, w_down, final_norm_g):
    B, S, D = x.shape
    assert B == BATCH_ROWS and norm1_g.shape[0] == 1
    assert S % T_TILE == 0 and T_TILE % PHASES == 0 and N_META % PHASES == 0
    f32 = _F32
    (a4r, a4i), zc, vc, locc = _s5_params(
        ssm_lambda_re[0].astype(f32), ssm_lambda_im[0].astype(f32), ssm_log_step[0].astype(f32),
        ssm_b_re[0].astype(f32), ssm_b_im[0].astype(f32), ssm_c_re[0].astype(f32), ssm_c_im[0].astype(f32))
    row = lambda a: a.astype(f32).reshape(1, -1)
    consts = (
        jnp.repeat(meta_tokens.astype(f32), B, axis=0),
        row(norm1_g[0]),
        row(a4r), row(a4i),
        zc,
        vc,
        locc,
        row(ssm_d[0]),
        _lanes_by_group(ssm_glu_w[0].astype(f32), HALF_GROUPS),
        row(ssm_glu_b[0]),
        row(ssm_norm_g[0]),
        pool_w[0].astype(f32),
        row(pool_scale[0]),
        row(pool_norm_g[0]),
        row(norm2_g[0]),
        row(final_norm_g),
    )
    hbm_weights = tuple(w.astype(f32) for w in (w_in, w_out, w_gate, w_up, w_down))
    out = _block(x.astype(f32), consts, hbm_weights, n_state=a4r.size, d_ssm=ssm_d[0].size,
                 d_pool=pool_scale[0].size, d_ff=w_gate.shape[-1])
    return out.astype(x.dtype)
```

```python
import jax
import jax.numpy as jnp
from jax import lax
from jax.experimental import pallas as pl
from jax.experimental.pallas import tpu as pltpu

N_META = 16
SSM_GROUP = 16
SSM_STATE = 64
POOL_WINDOWS = (2, 4, 8, 16)
EPS = 1e-6

BATCH_ROWS = 8
HALF_GROUPS = 16
PHASES = 4
QUAD_GROUPS = 4
SCAN_QUADS = 2
T_TILE = 64
FF_CHUNK = 256
STAGE_SLOTS = 8
WIDE_STAGE_ROWS = 32
NARROW_STAGE_ROWS = 128
VMEM_LIMIT = 60 * 1024 * 1024

_BF16 = jnp.bfloat16
_F32 = jnp.float32


def _rms(x, g):
    return x * lax.rsqrt(jnp.mean(x * x, axis=-1, keepdims=True) + EPS) * g


def _dot(a, b):
    return jnp.dot(a, b, preferred_element_type=_F32)


def _split_phases(a, n_chunks):
    blk = lambda t: a[t * BATCH_ROWS:(t + 1) * BATCH_ROWS]
    return [jnp.concatenate([blk(PHASES * k + p) for k in range(n_chunks)], axis=0) for p in range(PHASES)]


def _merge_phases(parts, n_chunks):
    blk = lambda a, k: a[k * BATCH_ROWS:(k + 1) * BATCH_ROWS]
    return jnp.concatenate([blk(p, k) for k in range(n_chunks) for p in parts], axis=0)


def _tile_copies(hbm, vmem, sems, slot, tile, nt, to_vmem):
    copies = []
    for b in range(BATCH_ROWS):
        h = hbm.at[b, pl.ds(tile * nt, nt), :]
        v = vmem.at[slot, :, b, :]
        src, dst = (h, v) if to_vmem else (v, h)
        copies.append(pltpu.make_async_copy(src, dst, sems.at[slot, b]))
    return copies


def _load_cast(w_hbm, dst, stage, sems, chunk, convert=None):
    if convert is None:
        convert = lambda rows: rows.astype(dst.dtype)
    n = dst.shape[0] // chunk
    ahead = STAGE_SLOTS - 1

    def copy(c, s):
        return pltpu.make_async_copy(w_hbm.at[0, pl.ds(c * chunk, chunk), :],
                                     stage.at[pl.ds(s * chunk, chunk), :], sems.at[s])

    for c in range(min(ahead, n)):
        copy(c, c).start()

    def body(c, carry):
        s = lax.rem(c, STAGE_SLOTS)

        @pl.when(c + ahead < n)
        def _():
            copy(c + ahead, lax.rem(c + ahead, STAGE_SLOTS)).start()

        copy(c, s).wait()
        src = pl.multiple_of(s * chunk, chunk)
        row = pl.multiple_of(c * chunk, chunk)
        dst[pl.ds(row, chunk), :] = convert(stage[pl.ds(src, chunk), :])
        return carry

    lax.fori_loop(0, n, body, 0)


def _expand_block_diag(src, dst, block_rows, block_lanes, n_blocks):
    lane = lax.broadcasted_iota(jnp.int32, src.shape, 1)
    lane_group = lax.rem(lane, n_blocks * block_lanes) // block_lanes
    for g in range(n_blocks):
        dst[g * block_rows:(g + 1) * block_rows, :] = jnp.where(lane_group == g, src, 0.0).astype(dst.dtype)


def _block_kernel(x_hbm, meta_ref, n1g_ref, a4r_ref, a4i_ref, zc_ref, vc_ref, locc_ref, d_ref, gluc_ref,
                  gb_ref, sng_ref, pw_ref, ps_ref, png_ref, n2g_ref, fg_ref,
                  win_hbm, wout_hbm, wg_hbm, wu_hbm, wd_hbm, o_hbm,
                  xbuf, xsem, obuf, osem, hbuf, zr, zi, st_r, st_i, u_last, halo, act,
                  win_ref, wout_ref, wg_ref, wu_ref, wd_ref, wz_ref, v_ref, loc_ref, gw_ref,
                  narrow_stage, wide_stage, wsem):
    i = pl.program_id(0)
    n_tiles = pl.num_programs(0) - 1
    slot = lax.rem(i, 2)
    oslot = 1 - slot

    def fetch(tile, s):
        return _tile_copies(x_hbm, xbuf, xsem, s, tile, T_TILE, to_vmem=True)

    def writeback(tile, s):
        return _tile_copies(o_hbm, obuf, osem, s, tile, T_TILE, to_vmem=False)

    d_ssm = d_ref.shape[-1]
    half_in = HALF_GROUPS * SSM_GROUP
    quad_in = QUAD_GROUPS * SSM_GROUP
    quad_state = QUAD_GROUPS * SSM_STATE
    n_quads = d_ssm // quad_in
    n_halves = d_ssm // half_in
    halo_rows = halo.shape[0]
    n_ff_chunks = wg_ref.shape[-1] // FF_CHUNK

    def build_weights():
        gd = pw_ref.shape[-1]
        pool_maps = [pw_ref[k] * ps_ref[:, k * gd:(k + 1) * gd] for k in range(pw_ref.shape[0])]

        def fold_pool(rows):
            cols = [rows[:, :d_ssm]]
            for k, pm in enumerate(pool_maps):
                cols.append(jnp.dot(rows[:, d_ssm + k * gd:d_ssm + (k + 1) * gd], pm,
                                    precision=lax.Precision.HIGHEST, preferred_element_type=_F32))
            return jnp.concatenate(cols, axis=-1).astype(_BF16)

        _load_cast(win_hbm, win_ref, narrow_stage, wsem, NARROW_STAGE_ROWS, convert=fold_pool)
        _load_cast(wout_hbm, wout_ref, narrow_stage, wsem, NARROW_STAGE_ROWS)
        _load_cast(wd_hbm, wd_ref, narrow_stage, wsem, NARROW_STAGE_ROWS)
        _load_cast(wg_hbm, wg_ref, wide_stage, wsem, WIDE_STAGE_ROWS)
        _load_cast(wu_hbm, wu_ref, wide_stage, wsem, WIDE_STAGE_ROWS)
        ph_rows = QUAD_GROUPS * SSM_GROUP
        for q in range(n_quads):
            for ri in range(2):
                for ph in range(PHASES):
                    _expand_block_diag(zc_ref[ri, ph, q], wz_ref.at[ri, q, ph * ph_rows:(ph + 1) * ph_rows, :],
                                       SSM_GROUP, SSM_STATE, QUAD_GROUPS)
                _expand_block_diag(vc_ref[ri, q], v_ref.at[ri, q], SSM_STATE, SSM_GROUP, QUAD_GROUPS)
            for ph in range(PHASES):
                _expand_block_diag(locc_ref[ph, q], loc_ref.at[q, ph * ph_rows:(ph + 1) * ph_rows, :],
                                   SSM_GROUP, SSM_GROUP, QUAD_GROUPS)
        for hh in range(n_halves):
            _expand_block_diag(gluc_ref[hh], gw_ref.at[hh], SSM_GROUP, SSM_GROUP, HALF_GROUPS)

    def project(rows):
        n1 = _rms(rows, n1g_ref[...]).astype(_BF16)
        return _dot(n1, win_ref[...])

    def quad_lanes(parts, q):
        return jnp.concatenate([p[:, q * quad_in:(q + 1) * quad_in] for p in parts], axis=1)

    def s5_states(u_in, n_chunks, qq):
        rows = n_chunks * BATCH_ROWS
        for q in range(qq * SCAN_QUADS, (qq + 1) * SCAN_QUADS):
            qs = slice(q * quad_state, (q + 1) * quad_state)
            lhs = quad_lanes(u_in, q)
            zr[0:rows, qs] = _dot(lhs, wz_ref[0, q])
            zi[0:rows, qs] = _dot(lhs, wz_ref[1, q])
        cs = slice(qq * SCAN_QUADS * quad_state, (qq + 1) * SCAN_QUADS * quad_state)
        width = SCAN_QUADS * quad_state
        ar = jnp.broadcast_to(a4r_ref[:, cs], (BATCH_ROWS, width))
        ai = jnp.broadcast_to(a4i_ref[:, cs], (BATCH_ROWS, width))
        sr = st_r[:, cs]
        si = st_i[:, cs]
        for k in range(n_chunks):
            rs = slice(k * BATCH_ROWS, (k + 1) * BATCH_ROWS)
            sr, si = (ar * sr - ai * si + zr[rs, cs], ar * si + ai * sr + zi[rs, cs])
            zr[rs, cs] = sr
            zi[rs, cs] = si
        st_r[:, cs] = sr
        st_i[:, cs] = si

    def s5_inputs(u, n_chunks):
        parts = _split_phases(u, n_chunks)
        state_in = []
        for p in range(1, PHASES):
            carry = slice((p - 1) * BATCH_ROWS, p * BATCH_ROWS)
            state_in.append(jnp.concatenate([u_last[carry, :], parts[p][:-BATCH_ROWS]], axis=0).astype(_BF16))
            u_last[carry, :] = parts[p][-BATCH_ROWS:]
        local_in = [p.astype(_BF16) for p in parts]
        return state_in + [local_in[0]], local_in

    def ffn_chunk(n2, c):
        cs = slice(c * FF_CHUNK, (c + 1) * FF_CHUNK)
        gate = _dot(n2, wg_ref[:, cs])
        up = _dot(n2, wu_ref[:, cs])
        act[:, cs] = (gate * jax.nn.sigmoid(gate) * up).astype(_BF16)

    def step(with_mixer, with_ffn):
        rows = T_TILE * BATCH_ROWS
        n_chunks = T_TILE // PHASES
        crows = n_chunks * BATCH_ROWS
        ff = iter(range(n_ff_chunks if with_ffn else 0))

        def ffn_chunks(n):
            for _ in range(n):
                c = next(ff, None)
                if c is not None:
                    ffn_chunk(n2, c)

        def ffn_tail():
            ffn_chunks(n_ff_chunks)
            out = _rms(h + _dot(act[...], wd_ref[...]), fg_ref[...])
            obuf[oslot] = out.reshape(obuf.shape[1:])

        if with_ffn:
            h = hbuf[...]
        if not with_mixer:
            n2 = _rms(h, n2g_ref[...]).astype(_BF16)
            ffn_tail()
            return

        x = xbuf[slot].reshape(rows, xbuf.shape[-1])
        n1 = _rms(x, n1g_ref[...]).astype(_BF16)
        proj = _dot(n1, win_ref[...])
        if with_ffn:
            n2 = _rms(h, n2g_ref[...]).astype(_BF16)
        u = proj[:, :d_ssm]
        v = proj[:, d_ssm:]

        state_in, local_in = s5_inputs(u, n_chunks)
        for qq in range(n_quads // SCAN_QUADS):
            s5_states(state_in, n_chunks, qq)
            ffn_chunks(1)
        chunk_out = []
        for q in range(n_quads):
            qs = slice(q * quad_state, (q + 1) * quad_state)
            chunk_out.append(_dot(zr[0:crows, qs].astype(_BF16), v_ref[0, q])
                             + _dot(zi[0:crows, qs].astype(_BF16), v_ref[1, q])
                             + _dot(quad_lanes(local_in, q), loc_ref[q]))
            if q % 4 == 3:
                ffn_chunks(1)
        y_parts = [jnp.concatenate([c[:, p * quad_in:(p + 1) * quad_in] for c in chunk_out], axis=-1)
                   for p in range(PHASES)]
        y = _merge_phases(y_parts, n_chunks) + d_ref[...] * u
        g = jax.nn.gelu(y)
        gb = g.astype(_BF16)
        ffn_chunks(1)
        gate = jnp.concatenate([_dot(gb[:, hh * half_in:(hh + 1) * half_in], gw_ref[hh])
                                for hh in range(n_halves)], axis=-1) + gb_ref[...]
        y_ssm = _rms(g * jax.nn.sigmoid(gate), sng_ref[...])
        ffn_chunks(1)

        vext = jnp.concatenate([halo[...], v], axis=0)
        halo[...] = v[rows - halo_rows:, :]
        gd = v.shape[-1] // len(POOL_WINDOWS)
        yps = []
        for k, w in enumerate(POOL_WINDOWS):
            acc = vext[:, k * gd:(k + 1) * gd]
            span = 1
            while span < w:
                shift = span * BATCH_ROWS
                acc = acc[shift:, :] + acc[:-shift, :]
                span *= 2
            yps.append(acc[acc.shape[0] - rows:, :] * (1.0 / w) - v[:, k * gd:(k + 1) * gd])
        ffn_chunks(1)
        y_pool = _rms(jnp.concatenate(yps, axis=-1), png_ref[...])

        mixed = jnp.concatenate([y_ssm, y_pool], axis=-1).astype(_BF16)
        if with_ffn:
            ffn_tail()
        hbuf[...] = x + _dot(mixed, wout_ref[...])

    @pl.when(i == 0)
    def _():
        for c in fetch(0, 0):
            c.start()

    @pl.when(i + 1 < n_tiles)
    def _():
        for c in fetch(i + 1, oslot):
            c.start()

    @pl.when(i >= 3)
    def _():
        for c in writeback(i - 3, oslot):
            c.wait()

    @pl.when(i == 0)
    def _():
        build_weights()
        st_r[...] = jnp.zeros_like(st_r)
        st_i[...] = jnp.zeros_like(st_i)
        u_last[...] = jnp.zeros_like(u_last)
        pm = project(meta_ref[...])
        state_in, _ = s5_inputs(pm[:, :d_ssm], N_META // PHASES)
        for qq in range(n_quads // SCAN_QUADS):
            s5_states(state_in, N_META // PHASES, qq)
        halo[...] = pm[:, d_ssm:]

    @pl.when(i < n_tiles)
    def _():
        for c in fetch(i, slot):
            c.wait()

    @pl.when(i == 0)
    def _():
        step(with_mixer=True, with_ffn=False)

    @pl.when(jnp.logical_and(i > 0, i < n_tiles))
    def _():
        step(with_mixer=True, with_ffn=True)

    @pl.when(i == n_tiles)
    def _():
        step(with_mixer=False, with_ffn=True)

    @pl.when(i >= 1)
    def _():
        for c in writeback(i - 1, oslot):
            c.start()

    @pl.when(i == n_tiles)
    def _():
        @pl.when(n_tiles >= 2)
        def _():
            for c in writeback(i - 2, slot):
                c.wait()
        for c in writeback(i - 1, oslot):
            c.wait()


def _const_spec(a):
    nd = a.ndim
    return pl.BlockSpec(a.shape, lambda i: (0,) * nd, pipeline_mode=pl.Buffered(1))


def _block(x, consts, hbm_weights, n_state, d_ssm, d_pool, d_ff):
    B, S, D = x.shape
    rows = T_TILE * B
    any_spec = pl.BlockSpec(memory_space=pl.ANY)
    n_quad = d_ssm // (QUAD_GROUPS * SSM_GROUP)
    n_half = d_ssm // (HALF_GROUPS * SSM_GROUP)
    tile_k = PHASES * QUAD_GROUPS * SSM_GROUP
    quad_state = QUAD_GROUPS * SSM_STATE
    return pl.pallas_call(
        _block_kernel,
        grid=(S // T_TILE + 1,),
        in_specs=[any_spec] + [_const_spec(c) for c in consts] + [any_spec] * len(hbm_weights),
        out_specs=any_spec,
        out_shape=jax.ShapeDtypeStruct((B, S, D), _F32),
        scratch_shapes=[
            pltpu.VMEM((2, T_TILE, B, D), _F32),
            pltpu.SemaphoreType.DMA((2, BATCH_ROWS)),
            pltpu.VMEM((2, T_TILE, B, D), _F32),
            pltpu.SemaphoreType.DMA((2, BATCH_ROWS)),
            pltpu.VMEM((rows, D), _F32),
            pltpu.VMEM((rows // PHASES, n_state), _F32),
            pltpu.VMEM((rows // PHASES, n_state), _F32),
            pltpu.VMEM((B, n_state), _F32),
            pltpu.VMEM((B, n_state), _F32),
            pltpu.VMEM(((PHASES - 1) * B, d_ssm), _F32),
            pltpu.VMEM((N_META * B, d_pool), _F32),
            pltpu.VMEM((rows, d_ff), _BF16),
            pltpu.VMEM((D, D), _BF16),
            pltpu.VMEM((D, D), _BF16),
            pltpu.VMEM((D, d_ff), _BF16),
            pltpu.VMEM((D, d_ff), _BF16),
            pltpu.VMEM((d_ff, D), _BF16),
            pltpu.VMEM((2, n_quad, tile_k, quad_state), _BF16),
            pltpu.VMEM((2, n_quad, quad_state, tile_k), _BF16),
            pltpu.VMEM((n_quad, tile_k, tile_k), _BF16),
            pltpu.VMEM((n_half, HALF_GROUPS * SSM_GROUP, HALF_GROUPS * SSM_GROUP), _BF16),
            pltpu.VMEM((STAGE_SLOTS * NARROW_STAGE_ROWS, D), _F32),
            pltpu.VMEM((STAGE_SLOTS * WIDE_STAGE_ROWS, d_ff), _F32),
            pltpu.SemaphoreType.DMA((STAGE_SLOTS,)),
        ],
        compiler_params=pltpu.CompilerParams(dimension_semantics=("arbitrary",),
                                             vmem_limit_bytes=VMEM_LIMIT),
        name="block",
    )(x, *consts, *hbm_weights)


def _lanes_by_group(blocks, per):
    G, r, c = blocks.shape
    return jnp.transpose(blocks.reshape(G // per, per, r, c), (0, 2, 1, 3)).reshape(G // per, r, per * c)


def _cmul(ar, ai, br, bi):
    return ar * br - ai * bi, ar * bi + ai * br


def _s5_params(lam_re, lam_im, log_step, b_re, b_im, c_re, c_im):
    lr = jnp.minimum(lam_re, -1e-4)
    li = lam_im
    step = jnp.exp(log_step)[:, None]
    mag = jnp.exp(lr * step)
    ang = li * step
    abr = mag * jnp.cos(ang)
    abi = mag * jnp.sin(ang)
    nr = abr - 1.0
    ni = abi
    den = lr * lr + li * li
    cr = ((nr * lr + ni * li) / den)[..., None]
    ci = ((ni * lr - nr * li) / den)[..., None]
    bbr = cr * b_re - ci * b_im
    bbi = cr * b_im + ci * b_re
    pw = [(jnp.ones_like(abr), jnp.zeros_like(abi))]
    for _ in range(PHASES):
        pw.append(_cmul(pw[-1][0], pw[-1][1], abr, abi))
    G, P, H = bbr.shape
    Q, g4 = G // QUAD_GROUPS, QUAD_GROUPS
    ab = jnp.stack([jnp.stack(c) for c in zip(*[_cmul(pr[..., None], pi[..., None], bbr, bbi)
                                               for pr, pi in pw[:PHASES]])])
    ca = jnp.stack([jnp.stack(c) for c in zip(*[_cmul(c_re, c_im, pr[:, None, :], pi[:, None, :])
                                               for pr, pi in pw[:PHASES]])])
    zc = jnp.transpose(ab[:, ::-1].reshape(2, PHASES, Q, g4, P, H), (0, 1, 2, 5, 3, 4)).reshape(2, PHASES, Q, H, g4 * P)
    sign = jnp.array([1.0, -1.0], dtype=ca.dtype).reshape(2, 1, 1, 1, 1)
    vc = jnp.transpose((sign * ca).reshape(2, PHASES, Q, g4, H, P), (0, 2, 5, 1, 3, 4)).reshape(2, Q, P, PHASES * g4 * H)
    loc = jnp.einsum('dghp,gpk->dghk', ca[0], bbr) - jnp.einsum('dghp,gpk->dghk', ca[1], bbi)
    zero = jnp.zeros_like(loc[0])
    full = jnp.stack([jnp.stack([loc[p - j] if 1 <= j <= p else zero for p in range(PHASES)])
                      for j in range(PHASES)])
    locc = jnp.transpose(full.reshape(PHASES, PHASES, Q, g4, H, H), (0, 2, 5, 1, 3, 4)).reshape(PHASES, Q, H, PHASES * g4 * H)
    return pw[PHASES], zc, vc, locc


def kernel(x, meta_tokens, norm1_g, w_in, ssm_lambda_re, ssm_lambda_im, ssm_log_step, ssm_b_re, ssm_b_im, ssm_c_re, ssm_c_im, ssm_d, ssm_glu_w, ssm_glu_b, ssm_norm_g, pool_w, pool_scale, pool_norm_g, w_out, norm2_g, w_gate, w_up, w_down, final_norm_g):
    B, S, D = x.shape
    assert B == BATCH_ROWS and norm1_g.shape[0] == 1
    assert S % T_TILE == 0 and T_TILE % PHASES == 0 and N_META % PHASES == 0
    f32 = _F32
    (a4r, a4i), zc, vc, locc = _s5_params(
        ssm_lambda_re[0].astype(f32), ssm_lambda_im[0].astype(f32), ssm_log_step[0].astype(f32),
        ssm_b_re[0].astype(f32), ssm_b_im[0].astype(f32), ssm_c_re[0].astype(f32), ssm_c_im[0].astype(f32))
    row = lambda a: a.astype(f32).reshape(1, -1)
    consts = (
        jnp.repeat(meta_tokens.astype(f32), B, axis=0),
        row(norm1_g[0]),
        row(a4r), row(a4i),
        zc,
        vc,
        locc,
        row(ssm_d[0]),
        _lanes_by_group(ssm_glu_w[0].astype(f32), HALF_GROUPS),
        row(ssm_glu_b[0]),
        row(ssm_norm_g[0]),
        pool_w[0].astype(f32),
        row(pool_scale[0]),
        row(pool_norm_g[0]),
        row(norm2_g[0]),
        row(final_norm_g),
    )
    hbm_weights = tuple(w.astype(f32) for w in (w_in, w_out, w_gate, w_up, w_down))
    out = _block(x.astype(f32), consts, hbm_weights, n_state=a4r.size, d_ssm=ssm_d[0].size,
                 d_pool=pool_scale[0].size, d_ff=w_gate.shape[-1])
    return out.astype(x.dtype)
```

```python
import jax
import jax.numpy as jnp
from jax import lax
from jax.experimental import pallas as pl
from jax.experimental.pallas import tpu as pltpu

N_META = 16
SSM_GROUP = 16
SSM_STATE = 64
POOL_WINDOWS = (2, 4, 8, 16)
EPS = 1e-6

BATCH_ROWS = 8
HALF_GROUPS = 16
PHASES = 4
QUAD_GROUPS = 4
SCAN_QUADS = 2
T_TILE = 64
FF_CHUNK = 256
STAGE_SLOTS = 8
WIDE_STAGE_ROWS = 32
NARROW_STAGE_ROWS = 128
VMEM_LIMIT = 60 * 1024 * 1024

_BF16 = jnp.bfloat16
_F32 = jnp.float32


def _rms(x, g):
    return x * lax.rsqrt(jnp.mean(x * x, axis=-1, keepdims=True) + EPS) * g


def _dot(a, b):
    return jnp.dot(a, b, preferred_element_type=_F32)


def _split_phases(a, n_chunks):
    blk = lambda t: a[t * BATCH_ROWS:(t + 1) * BATCH_ROWS]
    return [jnp.concatenate([blk(PHASES * k + p) for k in range(n_chunks)], axis=0) for p in range(PHASES)]


def _merge_phases(parts, n_chunks):
    blk = lambda a, k: a[k * BATCH_ROWS:(k + 1) * BATCH_ROWS]
    return jnp.concatenate([blk(p, k) for k in range(n_chunks) for p in parts], axis=0)


def _tile_copies(hbm, vmem, sems, slot, tile, nt, to_vmem):
    copies = []
    for b in range(BATCH_ROWS):
        h = hbm.at[b, pl.ds(tile * nt, nt), :]
        v = vmem.at[slot, :, b, :]
        src, dst = (h, v) if to_vmem else (v, h)
        copies.append(pltpu.make_async_copy(src, dst, sems.at[slot, b]))
    return copies


def _load_cast(w_hbm, dst, stage, sems, chunk, convert=None):
    if convert is None:
        convert = lambda rows: rows.astype(dst.dtype)
    n = dst.shape[0] // chunk
    ahead = STAGE_SLOTS - 1

    def copy(c, s):
        return pltpu.make_async_copy(w_hbm.at[0, pl.ds(c * chunk, chunk), :],
                                     stage.at[pl.ds(s * chunk, chunk), :], sems.at[s])

    for c in range(min(ahead, n)):
        copy(c, c).start()

    def body(c, carry):
        s = lax.rem(c, STAGE_SLOTS)

        @pl.when(c + ahead < n)
        def _():
            copy(c + ahead, lax.rem(c + ahead, STAGE_SLOTS)).start()

        copy(c, s).wait()
        src = pl.multiple_of(s * chunk, chunk)
        row = pl.multiple_of(c * chunk, chunk)
        dst[pl.ds(row, chunk), :] = convert(stage[pl.ds(src, chunk), :])
        return carry

    lax.fori_loop(0, n, body, 0)


def _expand_block_diag(src, dst, block_rows, block_lanes, n_blocks):
    lane = lax.broadcasted_iota(jnp.int32, src.shape, 1)
    lane_group = lax.rem(lane, n_blocks * block_lanes) // block_lanes
    for g in range(n_blocks):
        dst[g * block_rows:(g + 1) * block_rows, :] = jnp.where(lane_group == g, src, 0.0).astype(dst.dtype)


def _block_kernel(x_hbm, meta_ref, n1g_ref, a4r_ref, a4i_ref, zc_ref, vc_ref, locc_ref, d_ref, gluc_ref,
                  gb_ref, sng_ref, pw_ref, ps_ref, png_ref, n2g_ref, fg_ref,
                  win_hbm, wout_hbm, wg_hbm, wu_hbm, wd_hbm, o_hbm,
                  xbuf, xsem, obuf, osem, hbuf, zr, zi, st_r, st_i, u_last, halo, act,
                  win_ref, wout_ref, wg_ref, wu_ref, wd_ref, wz_ref, v_ref, loc_ref, gw_ref,
                  narrow_stage, wide_stage, wsem):
    i = pl.program_id(0)
    n_tiles = pl.num_programs(0) - 1
    slot = lax.rem(i, 2)
    oslot = 1 - slot

    def fetch(tile, s):
        return _tile_copies(x_hbm, xbuf, xsem, s, tile, T_TILE, to_vmem=True)

    def writeback(tile, s):
        return _tile_copies(o_hbm, obuf, osem, s, tile, T_TILE, to_vmem=False)

    d_ssm = d_ref.shape[-1]
    half_in = HALF_GROUPS * SSM_GROUP
    quad_in = QUAD_GROUPS * SSM_GROUP
    quad_state = QUAD_GROUPS * SSM_STATE
    n_quads = d_ssm // quad_in
    n_halves = d_ssm // half_in
    halo_rows = halo.shape[0]
    n_ff_chunks = wg_ref.shape[-1] // FF_CHUNK

    def build_weights():
        gd = pw_ref.shape[-1]
        pool_maps = [pw_ref[k] * ps_ref[:, k * gd:(k + 1) * gd] for k in range(pw_ref.shape[0])]

        def fold_pool(rows):
            cols = [rows[:, :d_ssm]]
            for k, pm in enumerate(pool_maps):
                cols.append(jnp.dot(rows[:, d_ssm + k * gd:d_ssm + (k + 1) * gd], pm,
                                    precision=lax.Precision.HIGHEST, preferred_element_type=_F32))
            return jnp.concatenate(cols, axis=-1).astype(_BF16)

        _load_cast(win_hbm, win_ref, narrow_stage, wsem, NARROW_STAGE_ROWS, convert=fold_pool)
        _load_cast(wout_hbm, wout_ref, narrow_stage, wsem, NARROW_STAGE_ROWS)
        _load_cast(wd_hbm, wd_ref, narrow_stage, wsem, NARROW_STAGE_ROWS)
        _load_cast(wg_hbm, wg_ref, wide_stage, wsem, WIDE_STAGE_ROWS)
        _load_cast(wu_hbm, wu_ref, wide_stage, wsem, WIDE_STAGE_ROWS)
        ph_rows = QUAD_GROUPS * SSM_GROUP
        for q in range(n_quads):
            for ri in range(2):
                for ph in range(PHASES):
                    _expand_block_diag(zc_ref[ri, ph, q], wz_ref.at[ri, q, ph * ph_rows:(ph + 1) * ph_rows, :],
                                       SSM_GROUP, SSM_STATE, QUAD_GROUPS)
                by_phase = jnp.concatenate([vc_ref[ri, ph, q] for ph in range(PHASES)], axis=-1)
                _expand_block_diag(by_phase, v_ref.at[ri, q], SSM_STATE, SSM_GROUP, QUAD_GROUPS)
            for j in range(PHASES):
                by_phase = jnp.concatenate([locc_ref[j, p, q] for p in range(PHASES)], axis=-1)
                _expand_block_diag(by_phase, loc_ref.at[q, j * ph_rows:(j + 1) * ph_rows, :],
                                   SSM_GROUP, SSM_GROUP, QUAD_GROUPS)
        for hh in range(n_halves):
            _expand_block_diag(gluc_ref[hh], gw_ref.at[hh], SSM_GROUP, SSM_GROUP, HALF_GROUPS)

    def project(rows):
        n1 = _rms(rows, n1g_ref[...]).astype(_BF16)
        return _dot(n1, win_ref[...])

    def quad_lanes(parts, q):
        return jnp.concatenate([p[:, q * quad_in:(q + 1) * quad_in] for p in parts], axis=1)

    def s5_states(u_in, n_chunks, qq):
        rows = n_chunks * BATCH_ROWS
        for q in range(qq * SCAN_QUADS, (qq + 1) * SCAN_QUADS):
            qs = slice(q * quad_state, (q + 1) * quad_state)
            lhs = quad_lanes(u_in, q)
            zr[0:rows, qs] = _dot(lhs, wz_ref[0, q])
            zi[0:rows, qs] = _dot(lhs, wz_ref[1, q])
        cs = slice(qq * SCAN_QUADS * quad_state, (qq + 1) * SCAN_QUADS * quad_state)
        width = SCAN_QUADS * quad_state
        ar = jnp.broadcast_to(a4r_ref[:, cs], (BATCH_ROWS, width))
        ai = jnp.broadcast_to(a4i_ref[:, cs], (BATCH_ROWS, width))
        sr = st_r[:, cs]
        si = st_i[:, cs]
        for k in range(n_chunks):
            rs = slice(k * BATCH_ROWS, (k + 1) * BATCH_ROWS)
            sr, si = (ar * sr - ai * si + zr[rs, cs], ar * si + ai * sr + zi[rs, cs])
            zr[rs, cs] = sr
            zi[rs, cs] = si
        st_r[:, cs] = sr
        st_i[:, cs] = si

    def s5_inputs(u, n_chunks):
        parts = _split_phases(u, n_chunks)
        state_in = []
        for p in range(1, PHASES):
            carry = slice((p - 1) * BATCH_ROWS, p * BATCH_ROWS)
            state_in.append(jnp.concatenate([u_last[carry, :], parts[p][:-BATCH_ROWS]], axis=0).astype(_BF16))
            u_last[carry, :] = parts[p][-BATCH_ROWS:]
        local_in = [p.astype(_BF16) for p in parts]
        return state_in + [local_in[0]], local_in

    def ffn_chunk(n2, c):
        cs = slice(c * FF_CHUNK, (c + 1) * FF_CHUNK)
        gate = _dot(n2, wg_ref[:, cs])
        up = _dot(n2, wu_ref[:, cs])
        act[:, cs] = (gate * jax.nn.sigmoid(gate) * up).astype(_BF16)

    def step(with_mixer, with_ffn):
        rows = T_TILE * BATCH_ROWS
        n_chunks = T_TILE // PHASES
        crows = n_chunks * BATCH_ROWS
        ff = iter(range(n_ff_chunks if with_ffn else 0))

        def ffn_chunks(n):
            for _ in range(n):
                c = next(ff, None)
                if c is not None:
                    ffn_chunk(n2, c)

        def ffn_tail():
            ffn_chunks(n_ff_chunks)
            out = _rms(h + _dot(act[...], wd_ref[...]), fg_ref[...])
            obuf[oslot] = out.reshape(obuf.shape[1:])

        if with_ffn:
            h = hbuf[...]
        if not with_mixer:
            n2 = _rms(h, n2g_ref[...]).astype(_BF16)
            ffn_tail()
            return

        x = xbuf[slot].reshape(rows, xbuf.shape[-1])
        n1 = _rms(x, n1g_ref[...]).astype(_BF16)
        proj = _dot(n1, win_ref[...])
        if with_ffn:
            n2 = _rms(h, n2g_ref[...]).astype(_BF16)
        u = proj[:, :d_ssm]
        v = proj[:, d_ssm:]

        state_in, local_in = s5_inputs(u, n_chunks)
        for qq in range(n_quads // SCAN_QUADS):
            s5_states(state_in, n_chunks, qq)
            ffn_chunks(1)
        chunk_out = []
        for q in range(n_quads):
            qs = slice(q * quad_state, (q + 1) * quad_state)
            chunk_out.append(_dot(zr[0:crows, qs].astype(_BF16), v_ref[0, q])
                             + _dot(zi[0:crows, qs].astype(_BF16), v_ref[1, q])
                             + _dot(quad_lanes(local_in, q), loc_ref[q]))
            if q % 4 == 3:
                ffn_chunks(1)
        y_parts = [jnp.concatenate([c[:, p * quad_in:(p + 1) * quad_in] for c in chunk_out], axis=-1)
                   for p in range(PHASES)]
        y = _merge_phases(y_parts, n_chunks) + d_ref[...] * u
        g = jax.nn.gelu(y)
        gb = g.astype(_BF16)
        ffn_chunks(1)
        gate = jnp.concatenate([_dot(gb[:, hh * half_in:(hh + 1) * half_in], gw_ref[hh])
                                for hh in range(n_halves)], axis=-1) + gb_ref[...]
        y_ssm = _rms(g * jax.nn.sigmoid(gate), sng_ref[...])
        ffn_chunks(1)

        vext = jnp.concatenate([halo[...], v], axis=0)
        halo[...] = v[rows - halo_rows:, :]
        gd = v.shape[-1] // len(POOL_WINDOWS)
        yps = []
        for k, w in enumerate(POOL_WINDOWS):
            acc = vext[:, k * gd:(k + 1) * gd]
            span = 1
            while span < w:
                shift = span * BATCH_ROWS
                acc = acc[shift:, :] + acc[:-shift, :]
                span *= 2
            yps.append(acc[acc.shape[0] - rows:, :] * (1.0 / w) - v[:, k * gd:(k + 1) * gd])
        ffn_chunks(1)
        y_pool = _rms(jnp.concatenate(yps, axis=-1), png_ref[...])

        mixed = jnp.concatenate([y_ssm, y_pool], axis=-1).astype(_BF16)
        if with_ffn:
            ffn_tail()
        hbuf[...] = x + _dot(mixed, wout_ref[...])

    @pl.when(i == 0)
    def _():
        for c in fetch(0, 0):
            c.start()

    @pl.when(i + 1 < n_tiles)
    def _():
        for c in fetch(i + 1, oslot):
            c.start()

    @pl.when(i >= 3)
    def _():
        for c in writeback(i - 3, oslot):
            c.wait()

    @pl.when(i == 0)
    def _():
        build_weights()
        st_r[...] = jnp.zeros_like(st_r)
        st_i[...] = jnp.zeros_like(st_i)
        u_last[...] = jnp.zeros_like(u_last)
        pm = project(meta_ref[...])
        state_in, _ = s5_inputs(pm[:, :d_ssm], N_META // PHASES)
        for qq in range(n_quads // SCAN_QUADS):
            s5_states(state_in, N_META // PHASES, qq)
        halo[...] = pm[:, d_ssm:]

    @pl.when(i < n_tiles)
    def _():
        for c in fetch(i, slot):
            c.wait()

    @pl.when(i == 0)
    def _():
        step(with_mixer=True, with_ffn=False)

    @pl.when(jnp.logical_and(i > 0, i < n_tiles))
    def _():
        step(with_mixer=True, with_ffn=True)

    @pl.when(i == n_tiles)
    def _():
        step(with_mixer=False, with_ffn=True)

    @pl.when(i >= 1)
    def _():
        for c in writeback(i - 1, oslot):
            c.start()

    @pl.when(i == n_tiles)
    def _():
        @pl.when(n_tiles >= 2)
        def _():
            for c in writeback(i - 2, slot):
                c.wait()
        for c in writeback(i - 1, oslot):
            c.wait()


def _const_spec(a):
    nd = a.ndim
    return pl.BlockSpec(a.shape, lambda i: (0,) * nd, pipeline_mode=pl.Buffered(1))


def _block(x, consts, hbm_weights, n_state, d_ssm, d_pool, d_ff):
    B, S, D = x.shape
    rows = T_TILE * B
    any_spec = pl.BlockSpec(memory_space=pl.ANY)
    n_quad = d_ssm // (QUAD_GROUPS * SSM_GROUP)
    n_half = d_ssm // (HALF_GROUPS * SSM_GROUP)
    tile_k = PHASES * QUAD_GROUPS * SSM_GROUP
    quad_state = QUAD_GROUPS * SSM_STATE
    return pl.pallas_call(
        _block_kernel,
        grid=(S // T_TILE + 1,),
        in_specs=[any_spec] + [_const_spec(c) for c in consts] + [any_spec] * len(hbm_weights),
        out_specs=any_spec,
        out_shape=jax.ShapeDtypeStruct((B, S, D), _F32),
        scratch_shapes=[
            pltpu.VMEM((2, T_TILE, B, D), _F32),
            pltpu.SemaphoreType.DMA((2, BATCH_ROWS)),
            pltpu.VMEM((2, T_TILE, B, D), _F32),
            pltpu.SemaphoreType.DMA((2, BATCH_ROWS)),
            pltpu.VMEM((rows, D), _F32),
            pltpu.VMEM((rows // PHASES, n_state), _F32),
            pltpu.VMEM((rows // PHASES, n_state), _F32),
            pltpu.VMEM((B, n_state), _F32),
            pltpu.VMEM((B, n_state), _F32),
            pltpu.VMEM(((PHASES - 1) * B, d_ssm), _F32),
            pltpu.VMEM((N_META * B, d_pool), _F32),
            pltpu.VMEM((rows, d_ff), _BF16),
            pltpu.VMEM((D, D), _BF16),
            pltpu.VMEM((D, D), _BF16),
            pltpu.VMEM((D, d_ff), _BF16),
            pltpu.VMEM((D, d_ff), _BF16),
            pltpu.VMEM((d_ff, D), _BF16),
            pltpu.VMEM((2, n_quad, tile_k, quad_state), _BF16),
            pltpu.VMEM((2, n_quad, quad_state, tile_k), _BF16),
            pltpu.VMEM((n_quad, tile_k, tile_k), _BF16),
            pltpu.VMEM((n_half, HALF_GROUPS * SSM_GROUP, HALF_GROUPS * SSM_GROUP), _BF16),
            pltpu.VMEM((STAGE_SLOTS * NARROW_STAGE_ROWS, D), _F32),
            pltpu.VMEM((STAGE_SLOTS * WIDE_STAGE_ROWS, d_ff), _F32),
            pltpu.SemaphoreType.DMA((STAGE_SLOTS,)),
        ],
        compiler_params=pltpu.CompilerParams(dimension_semantics=("arbitrary",),
                                             vmem_limit_bytes=VMEM_LIMIT),
        name="block",
    )(x, *consts, *hbm_weights)


def _lanes_by_group(blocks, per):
    G, r, c = blocks.shape
    return jnp.transpose(blocks.reshape(G // per, per, r, c), (0, 2, 1, 3)).reshape(G // per, r, per * c)


def _cmul(ar, ai, br, bi):
    return ar * br - ai * bi, ar * bi + ai * br


def _s5_params(lam_re, lam_im, log_step, b_re, b_im, c_re, c_im):
    lr = jnp.minimum(lam_re, -1e-4)
    li = lam_im
    step = jnp.exp(log_step)[:, None]
    mag = jnp.exp(lr * step)
    ang = li * step
    abr = mag * jnp.cos(ang)
    abi = mag * jnp.sin(ang)
    nr = abr - 1.0
    ni = abi
    den = lr * lr + li * li
    cr = ((nr * lr + ni * li) / den)[..., None]
    ci = ((ni * lr - nr * li) / den)[..., None]
    bbr = cr * b_re - ci * b_im
    bbi = cr * b_im + ci * b_re
    pw = [(jnp.ones_like(abr), jnp.zeros_like(abi))]
    for _ in range(PHASES):
        pw.append(_cmul(pw[-1][0], pw[-1][1], abr, abi))
    G = bbr.shape[0]
    Q = G // QUAD_GROUPS
    t = lambda a: jnp.swapaxes(a, 1, 2)

    def by_quad(blocks, lead):
        packed = _lanes_by_group(jnp.concatenate(blocks, axis=0), QUAD_GROUPS)
        return packed.reshape(lead + (Q,) + packed.shape[1:])

    ab = [_cmul(pr[..., None], pi[..., None], bbr, bbi) for pr, pi in pw[:PHASES]]
    ca = [_cmul(c_re, c_im, pr[:, None, :], pi[:, None, :]) for pr, pi in pw[:PHASES]]
    zc = by_quad([t(ab[PHASES - 1 - ph][ri]) for ri in range(2) for ph in range(PHASES)], (2, PHASES))
    sign = (1.0, -1.0)
    vc = by_quad([sign[ri] * t(ca[ph][ri]) for ri in range(2) for ph in range(PHASES)], (2, PHASES))
    loc = [jnp.einsum('ghp,gpk->ghk', ca[d][0], bbr) - jnp.einsum('ghp,gpk->ghk', ca[d][1], bbi)
           for d in range(PHASES - 1)]
    zero = jnp.zeros_like(loc[0])
    locc = by_quad([t(loc[p - j]) if 1 <= j <= p else zero for j in range(PHASES) for p in range(PHASES)],
                   (PHASES, PHASES))
    return pw[PHASES], zc, vc, locc


def kernel(x, meta_tokens, norm1_g, w_in, ssm_lambda_re, ssm_lambda_im, ssm_log_step, ssm_b_re, ssm_b_im, ssm_c_re, ssm_c_im, ssm_d, ssm_glu_w, ssm_glu_b, ssm_norm_g, pool_w, pool_scale, pool_norm_g, w_out, norm2_g, w_gate, w_up, w_down, final_norm_g):
    B, S, D = x.shape
    assert B == BATCH_ROWS and norm1_g.shape[0] == 1
    assert S % T_TILE == 0 and T_TILE % PHASES == 0 and N_META % PHASES == 0
    f32 = _F32
    (a4r, a4i), zc, vc, locc = _s5_params(
        ssm_lambda_re[0].astype(f32), ssm_lambda_im[0].astype(f32), ssm_log_step[0].astype(f32),
        ssm_b_re[0].astype(f32), ssm_b_im[0].astype(f32), ssm_c_re[0].astype(f32), ssm_c_im[0].astype(f32))
    row = lambda a: a.astype(f32).reshape(1, -1)
    consts = (
        jnp.repeat(meta_tokens.astype(f32), B, axis=0),
        row(norm1_g[0]),
        row(a4r), row(a4i),
        zc,
        vc,
        locc,
        row(ssm_d[0]),
        _lanes_by_group(ssm_glu_w[0].astype(f32), HALF_GROUPS),
        row(ssm_glu_b[0]),
        row(ssm_norm_g[0]),
        pool_w[0].astype(f32),
        row(pool_scale[0]),
        row(pool_norm_g[0]),
        row(norm2_g[0]),
        row(final_norm_g),
    )
    hbm_weights = tuple(w.astype(f32) for w in (w_in, w_out, w_gate, w_up, w_down))
    out = _block(x.astype(f32), consts, hbm_weights, n_state=a4r.size, d_ssm=ssm_d[0].size,
                 d_pool=pool_scale[0].size, d_ff=w_gate.shape[-1])
    return out.astype(x.dtype)
```

```python
import jax
import jax.numpy as jnp
from jax import lax
from jax.experimental import pallas as pl
from jax.experimental.pallas import tpu as pltpu

N_META = 16
SSM_GROUP = 16
SSM_STATE = 64
POOL_WINDOWS = (2, 4, 8, 16)
EPS = 1e-6

BATCH_ROWS = 8
HALF_GROUPS = 16
PHASES = 4
QUAD_GROUPS = 4
SCAN_QUADS = 2
T_TILE = 64
FF_CHUNK = 256
STAGE_SLOTS = 8
WIDE_STAGE_ROWS = 32
NARROW_STAGE_ROWS = 128
VMEM_LIMIT = 60 * 1024 * 1024

_BF16 = jnp.bfloat16
_F32 = jnp.float32


def _rms(x, g):
    return x * lax.rsqrt(jnp.mean(x * x, axis=-1, keepdims=True) + EPS) * g


def _dot(a, b):
    return jnp.dot(a, b, preferred_element_type=_F32)


def _split_phases(a, n_chunks):
    blk = lambda t: a[t * BATCH_ROWS:(t + 1) * BATCH_ROWS]
    return [jnp.concatenate([blk(PHASES * k + p) for k in range(n_chunks)], axis=0) for p in range(PHASES)]


def _merge_phases(parts, n_chunks):
    blk = lambda a, k: a[k * BATCH_ROWS:(k + 1) * BATCH_ROWS]
    return jnp.concatenate([blk(p, k) for k in range(n_chunks) for p in parts], axis=0)


def _tile_copies(hbm, vmem, sems, slot, tile, nt, to_vmem):
    copies = []
    for b in range(BATCH_ROWS):
        h = hbm.at[b, pl.ds(tile * nt, nt), :]
        v = vmem.at[slot, :, b, :]
        src, dst = (h, v) if to_vmem else (v, h)
        copies.append(pltpu.make_async_copy(src, dst, sems.at[slot, b]))
    return copies


def _load_cast(w_hbm, dst, stage, sems, chunk, convert=None):
    if convert is None:
        convert = lambda rows: rows.astype(dst.dtype)
    n = dst.shape[0] // chunk
    ahead = STAGE_SLOTS - 1

    def copy(c, s):
        return pltpu.make_async_copy(w_hbm.at[0, pl.ds(c * chunk, chunk), :],
                                     stage.at[pl.ds(s * chunk, chunk), :], sems.at[s])

    for c in range(min(ahead, n)):
        copy(c, c).start()

    def body(c, carry):
        s = lax.rem(c, STAGE_SLOTS)

        @pl.when(c + ahead < n)
        def _():
            copy(c + ahead, lax.rem(c + ahead, STAGE_SLOTS)).start()

        copy(c, s).wait()
        src = pl.multiple_of(s * chunk, chunk)
        row = pl.multiple_of(c * chunk, chunk)
        dst[pl.ds(row, chunk), :] = convert(stage[pl.ds(src, chunk), :])
        return carry

    lax.fori_loop(0, n, body, 0)


def _expand_block_diag(src, dst, block_rows, block_lanes, n_blocks):
    lane = lax.broadcasted_iota(jnp.int32, src.shape, 1)
    lane_group = lax.rem(lane, n_blocks * block_lanes) // block_lanes
    for g in range(n_blocks):
        dst[g * block_rows:(g + 1) * block_rows, :] = jnp.where(lane_group == g, src, 0.0).astype(dst.dtype)


def _block_kernel(x_hbm, meta_ref, n1g_ref, a4r_ref, a4i_ref, zc_ref, vc_ref, locc_ref, d_ref, gluc_ref,
                  gb_ref, sng_ref, pw_ref, ps_ref, png_ref, n2g_ref, fg_ref,
                  win_hbm, wout_hbm, wg_hbm, wu_hbm, wd_hbm, o_hbm,
                  xbuf, xsem, obuf, osem, hbuf, zr, zi, st_r, st_i, u_last, halo, act,
                  win_ref, wout_ref, wg_ref, wu_ref, wd_ref, wz_ref, v_ref, loc_ref, gw_ref,
                  narrow_stage, wide_stage, wsem, junk):
    i = pl.program_id(0)
    n_tiles = pl.num_programs(0) - 1
    slot = lax.rem(i, 2)
    oslot = 1 - slot

    def fetch(tile, s):
        return _tile_copies(x_hbm, xbuf, xsem, s, tile, T_TILE, to_vmem=True)

    def writeback(tile, s):
        return _tile_copies(o_hbm, obuf, osem, s, tile, T_TILE, to_vmem=False)

    d_ssm = d_ref.shape[-1]
    half_in = HALF_GROUPS * SSM_GROUP
    quad_in = QUAD_GROUPS * SSM_GROUP
    quad_state = QUAD_GROUPS * SSM_STATE
    n_quads = d_ssm // quad_in
    n_halves = d_ssm // half_in
    halo_rows = halo.shape[0]
    n_ff_chunks = wg_ref.shape[-1] // FF_CHUNK

    def build_weights():
        gd = pw_ref.shape[-1]
        pool_maps = [pw_ref[k] * ps_ref[:, k * gd:(k + 1) * gd] for k in range(pw_ref.shape[0])]

        def fold_pool(rows):
            cols = [rows[:, :d_ssm]]
            for k, pm in enumerate(pool_maps):
                cols.append(jnp.dot(rows[:, d_ssm + k * gd:d_ssm + (k + 1) * gd], pm,
                                    precision=lax.Precision.HIGHEST, preferred_element_type=_F32))
            return jnp.concatenate(cols, axis=-1).astype(_BF16)

        _load_cast(win_hbm, win_ref, narrow_stage, wsem, NARROW_STAGE_ROWS, convert=fold_pool)
        _load_cast(wout_hbm, wout_ref, narrow_stage, wsem, NARROW_STAGE_ROWS)
        _load_cast(wd_hbm, wd_ref, narrow_stage, wsem, NARROW_STAGE_ROWS)
        _load_cast(wg_hbm, wg_ref, wide_stage, wsem, WIDE_STAGE_ROWS)
        _load_cast(wu_hbm, wu_ref, wide_stage, wsem, WIDE_STAGE_ROWS)
        ph_rows = QUAD_GROUPS * SSM_GROUP
        for q in range(n_quads):
            for ri in range(2):
                for ph in range(PHASES):
                    _expand_block_diag(zc_ref[ri, ph, q], wz_ref.at[ri, q, ph * ph_rows:(ph + 1) * ph_rows, :],
                                       SSM_GROUP, SSM_STATE, QUAD_GROUPS)
                by_phase = jnp.concatenate([vc_ref[ri, ph, q] for ph in range(PHASES)], axis=-1)
                _expand_block_diag(by_phase, v_ref.at[ri, q], SSM_STATE, SSM_GROUP, QUAD_GROUPS)
            for j in range(PHASES):
                by_phase = jnp.concatenate([locc_ref[j, p, q] for p in range(PHASES)], axis=-1)
                _expand_block_diag(by_phase, loc_ref.at[q, j * ph_rows:(j + 1) * ph_rows, :],
                                   SSM_GROUP, SSM_GROUP, QUAD_GROUPS)
        for hh in range(n_halves):
            _expand_block_diag(gluc_ref[hh], gw_ref.at[hh], SSM_GROUP, SSM_GROUP, HALF_GROUPS)

    def project(rows):
        n1 = _rms(rows, n1g_ref[...]).astype(_BF16)
        return _dot(n1, win_ref[...])

    def quad_lanes(parts, q):
        return jnp.concatenate([p[:, q * quad_in:(q + 1) * quad_in] for p in parts], axis=1)

    def s5_states(u_in, n_chunks, qq):
        rows = n_chunks * BATCH_ROWS
        for q in range(qq * SCAN_QUADS, (qq + 1) * SCAN_QUADS):
            qs = slice(q * quad_state, (q + 1) * quad_state)
            lhs = quad_lanes(u_in, q)
            zr[0:rows, qs] = _dot(lhs, wz_ref[0, q])
            zi[0:rows, qs] = _dot(lhs, wz_ref[1, q])
        cs = slice(qq * SCAN_QUADS * quad_state, (qq + 1) * SCAN_QUADS * quad_state)
        width = SCAN_QUADS * quad_state
        ar = jnp.broadcast_to(a4r_ref[:, cs], (BATCH_ROWS, width))
        ai = jnp.broadcast_to(a4i_ref[:, cs], (BATCH_ROWS, width))
        sr = st_r[:, cs]
        si = st_i[:, cs]
        for k in range(n_chunks):
            rs = slice(k * BATCH_ROWS, (k + 1) * BATCH_ROWS)
            sr, si = (ar * sr - ai * si + zr[rs, cs], ar * si + ai * sr + zi[rs, cs])
            zr[rs, cs] = sr
            zi[rs, cs] = si
        st_r[:, cs] = sr
        st_i[:, cs] = si

    def s5_inputs(u, n_chunks):
        parts = _split_phases(u, n_chunks)
        state_in = []
        for p in range(1, PHASES):
            carry = slice((p - 1) * BATCH_ROWS, p * BATCH_ROWS)
            state_in.append(jnp.concatenate([u_last[carry, :], parts[p][:-BATCH_ROWS]], axis=0).astype(_BF16))
            u_last[carry, :] = parts[p][-BATCH_ROWS:]
        local_in = [p.astype(_BF16) for p in parts]
        return state_in + [local_in[0]], local_in

    def ffn_chunk(n2, c):
        cs = slice(c * FF_CHUNK, (c + 1) * FF_CHUNK)
        gate = _dot(n2, wg_ref[:, cs])
        up = _dot(n2, wu_ref[:, cs])
        act[:, cs] = (gate * jax.nn.sigmoid(gate) * up).astype(_BF16)

    def step(with_mixer, with_ffn):
        rows = T_TILE * BATCH_ROWS
        n_chunks = T_TILE // PHASES
        crows = n_chunks * BATCH_ROWS
        ff = iter(range(n_ff_chunks if with_ffn else 0))

        def ffn_chunks(n):
            for _ in range(n):
                c = next(ff, None)
                if c is not None:
                    ffn_chunk(n2, c)

        def ffn_tail():
            ffn_chunks(n_ff_chunks)
            out = _rms(h + _dot(act[...], wd_ref[...]), fg_ref[...])
            obuf[oslot] = out.reshape(obuf.shape[1:])

        if with_ffn:
            h = hbuf[...]
        if not with_mixer:
            n2 = _rms(h, n2g_ref[...]).astype(_BF16)
            ffn_tail()
            return

        x = xbuf[slot].reshape(rows, xbuf.shape[-1])
        n1 = _rms(x, n1g_ref[...]).astype(_BF16)
        proj = _dot(n1, win_ref[...])
        if with_ffn:
            n2 = _rms(h, n2g_ref[...]).astype(_BF16)
        u = proj[:, :d_ssm]
        v = proj[:, d_ssm:]

        state_in, local_in = s5_inputs(u, n_chunks)
        for qq in range(n_quads // SCAN_QUADS):
            s5_states(state_in, n_chunks, qq)
            ffn_chunks(1)
        chunk_out = []
        for q in range(n_quads):
            qs = slice(q * quad_state, (q + 1) * quad_state)
            chunk_out.append(_dot(zr[0:crows, qs].astype(_BF16), v_ref[0, q])
                             + _dot(zi[0:crows, qs].astype(_BF16), v_ref[1, q])
                             + _dot(quad_lanes(local_in, q), loc_ref[q]))
            if q % 4 == 3:
                ffn_chunks(1)
        y_parts = [jnp.concatenate([c[:, p * quad_in:(p + 1) * quad_in] for c in chunk_out], axis=-1)
                   for p in range(PHASES)]
        y = _merge_phases(y_parts, n_chunks) + d_ref[...] * u
        g = jax.nn.gelu(y)
        gb = g.astype(_BF16)
        ffn_chunks(1)
        gate = jnp.concatenate([_dot(gb[:, hh * half_in:(hh + 1) * half_in], gw_ref[hh])
                                for hh in range(n_halves)], axis=-1) + gb_ref[...]
        y_ssm = _rms(g * jax.nn.sigmoid(gate), sng_ref[...])
        ffn_chunks(1)

        vext = jnp.concatenate([halo[...], v], axis=0)
        halo[...] = v[rows - halo_rows:, :]
        gd = v.shape[-1] // len(POOL_WINDOWS)
        yps = []
        for k, w in enumerate(POOL_WINDOWS):
            acc = vext[:, k * gd:(k + 1) * gd]
            span = 1
            while span < w:
                shift = span * BATCH_ROWS
                acc = acc[shift:, :] + acc[:-shift, :]
                span *= 2
            yps.append(acc[acc.shape[0] - rows:, :] * (1.0 / w) - v[:, k * gd:(k + 1) * gd])
        ffn_chunks(1)
        y_pool = _rms(jnp.concatenate(yps, axis=-1), png_ref[...])

        mixed = jnp.concatenate([y_ssm, y_pool], axis=-1).astype(_BF16)
        if with_ffn:
            ffn_tail()
        hbuf[...] = x + _dot(mixed, wout_ref[...])

    @pl.when(i == 0)
    def _():
        for c in fetch(0, 0):
            c.start()

    @pl.when(i + 1 < n_tiles)
    def _():
        for c in fetch(i + 1, oslot):
            c.start()

    @pl.when(i >= 3)
    def _():
        junk[...] = _dot(wout_ref[0:256, 0:256], wout_ref[0:256, 0:512])
        for c in writeback(i - 3, oslot):
            c.wait()

    @pl.when(i == 0)
    def _():
        build_weights()
        st_r[...] = jnp.zeros_like(st_r)
        st_i[...] = jnp.zeros_like(st_i)
        u_last[...] = jnp.zeros_like(u_last)
        pm = project(meta_ref[...])
        state_in, _ = s5_inputs(pm[:, :d_ssm], N_META // PHASES)
        for qq in range(n_quads // SCAN_QUADS):
            s5_states(state_in, N_META // PHASES, qq)
        halo[...] = pm[:, d_ssm:]

    @pl.when(i < n_tiles)
    def _():
        for c in fetch(i, slot):
            c.wait()

    @pl.when(i == 0)
    def _():
        step(with_mixer=True, with_ffn=False)

    @pl.when(jnp.logical_and(i > 0, i < n_tiles))
    def _():
        step(with_mixer=True, with_ffn=True)

    @pl.when(i == n_tiles)
    def _():
        step(with_mixer=False, with_ffn=True)

    @pl.when(i >= 1)
    def _():
        junk[...] = _dot(wout_ref[0:256, 0:256], wout_ref[0:256, 0:512])
        for c in writeback(i - 1, oslot):
            c.start()

    @pl.when(i == n_tiles)
    def _():
        @pl.when(n_tiles >= 2)
        def _():
            for c in writeback(i - 2, slot):
                c.wait()
        for c in writeback(i - 1, oslot):
            c.wait()


def _const_spec(a):
    nd = a.ndim
    return pl.BlockSpec(a.shape, lambda i: (0,) * nd, pipeline_mode=pl.Buffered(1))


def _block(x, consts, hbm_weights, n_state, d_ssm, d_pool, d_ff):
    B, S, D = x.shape
    rows = T_TILE * B
    any_spec = pl.BlockSpec(memory_space=pl.ANY)
    n_quad = d_ssm // (QUAD_GROUPS * SSM_GROUP)
    n_half = d_ssm // (HALF_GROUPS * SSM_GROUP)
    tile_k = PHASES * QUAD_GROUPS * SSM_GROUP
    quad_state = QUAD_GROUPS * SSM_STATE
    return pl.pallas_call(
        _block_kernel,
        grid=(S // T_TILE + 1,),
        in_specs=[any_spec] + [_const_spec(c) for c in consts] + [any_spec] * len(hbm_weights),
        out_specs=any_spec,
        out_shape=jax.ShapeDtypeStruct((B, S, D), _F32),
        scratch_shapes=[
            pltpu.VMEM((2, T_TILE, B, D), _F32),
            pltpu.SemaphoreType.DMA((2, BATCH_ROWS)),
            pltpu.VMEM((2, T_TILE, B, D), _F32),
            pltpu.SemaphoreType.DMA((2, BATCH_ROWS)),
            pltpu.VMEM((rows, D), _F32),
            pltpu.VMEM((rows // PHASES, n_state), _F32),
            pltpu.VMEM((rows // PHASES, n_state), _F32),
            pltpu.VMEM((B, n_state), _F32),
            pltpu.VMEM((B, n_state), _F32),
            pltpu.VMEM(((PHASES - 1) * B, d_ssm), _F32),
            pltpu.VMEM((N_META * B, d_pool), _F32),
            pltpu.VMEM((rows, d_ff), _BF16),
            pltpu.VMEM((D, D), _BF16),
            pltpu.VMEM((D, D), _BF16),
            pltpu.VMEM((D, d_ff), _BF16),
            pltpu.VMEM((D, d_ff), _BF16),
            pltpu.VMEM((d_ff, D), _BF16),
            pltpu.VMEM((2, n_quad, tile_k, quad_state), _BF16),
            pltpu.VMEM((2, n_quad, quad_state, tile_k), _BF16),
            pltpu.VMEM((n_quad, tile_k, tile_k), _BF16),
            pltpu.VMEM((n_half, HALF_GROUPS * SSM_GROUP, HALF_GROUPS * SSM_GROUP), _BF16),
            pltpu.VMEM((STAGE_SLOTS * NARROW_STAGE_ROWS, D), _F32),
            pltpu.VMEM((STAGE_SLOTS * WIDE_STAGE_ROWS, d_ff), _F32),
            pltpu.SemaphoreType.DMA((STAGE_SLOTS,)),
            pltpu.VMEM((256, 512), _F32),
        ],
        compiler_params=pltpu.CompilerParams(dimension_semantics=("arbitrary",),
                                             vmem_limit_bytes=VMEM_LIMIT),
        name="block",
    )(x, *consts, *hbm_weights)


def _lanes_by_group(blocks, per):
    G, r, c = blocks.shape
    return jnp.transpose(blocks.reshape(G // per, per, r, c), (0, 2, 1, 3)).reshape(G // per, r, per * c)


def _cmul(ar, ai, br, bi):
    return ar * br - ai * bi, ar * bi + ai * br


def _s5_params(lam_re, lam_im, log_step, b_re, b_im, c_re, c_im):
    lr = jnp.minimum(lam_re, -1e-4)
    li = lam_im
    step = jnp.exp(log_step)[:, None]
    mag = jnp.exp(lr * step)
    ang = li * step
    abr = mag * jnp.cos(ang)
    abi = mag * jnp.sin(ang)
    nr = abr - 1.0
    ni = abi
    den = lr * lr + li * li
    cr = ((nr * lr + ni * li) / den)[..., None]
    ci = ((ni * lr - nr * li) / den)[..., None]
    bbr = cr * b_re - ci * b_im
    bbi = cr * b_im + ci * b_re
    pw = [(jnp.ones_like(abr), jnp.zeros_like(abi))]
    for _ in range(PHASES):
        pw.append(_cmul(pw[-1][0], pw[-1][1], abr, abi))
    G = bbr.shape[0]
    Q = G // QUAD_GROUPS
    t = lambda a: jnp.swapaxes(a, 1, 2)

    def by_quad(blocks, lead):
        packed = _lanes_by_group(jnp.concatenate(blocks, axis=0), QUAD_GROUPS)
        return packed.reshape(lead + (Q,) + packed.shape[1:])

    ab = [_cmul(pr[..., None], pi[..., None], bbr, bbi) for pr, pi in pw[:PHASES]]
    ca = [_cmul(c_re, c_im, pr[:, None, :], pi[:, None, :]) for pr, pi in pw[:PHASES]]
    zc = by_quad([t(ab[PHASES - 1 - ph][ri]) for ri in range(2) for ph in range(PHASES)], (2, PHASES))
    sign = (1.0, -1.0)
    vc = by_quad([sign[ri] * t(ca[ph][ri]) for ri in range(2) for ph in range(PHASES)], (2, PHASES))
    loc = [jnp.einsum('ghp,gpk->ghk', ca[d][0], bbr) - jnp.einsum('ghp,gpk->ghk', ca[d][1], bbi)
           for d in range(PHASES - 1)]
    zero = jnp.zeros_like(loc[0])
    locc = by_quad([t(loc[p - j]) if 1 <= j <= p else zero for j in range(PHASES) for p in range(PHASES)],
                   (PHASES, PHASES))
    return pw[PHASES], zc, vc, locc


def kernel(x, meta_tokens, norm1_g, w_in, ssm_lambda_re, ssm_lambda_im, ssm_log_step, ssm_b_re, ssm_b_im, ssm_c_re, ssm_c_im, ssm_d, ssm_glu_w, ssm_glu_b, ssm_norm_g, pool_w, pool_scale, pool_norm_g, w_out, norm2_g, w_gate, w_up, w_down, final_norm_g):
    B, S, D = x.shape
    assert B == BATCH_ROWS and norm1_g.shape[0] == 1
    assert S % T_TILE == 0 and T_TILE % PHASES == 0 and N_META % PHASES == 0
    f32 = _F32
    (a4r, a4i), zc, vc, locc = _s5_params(
        ssm_lambda_re[0].astype(f32), ssm_lambda_im[0].astype(f32), ssm_log_step[0].astype(f32),
        ssm_b_re[0].astype(f32), ssm_b_im[0].astype(f32), ssm_c_re[0].astype(f32), ssm_c_im[0].astype(f32))
    row = lambda a: a.astype(f32).reshape(1, -1)
    consts = (
        jnp.repeat(meta_tokens.astype(f32), B, axis=0),
        row(norm1_g[0]),
        row(a4r), row(a4i),
        zc,
        vc,
        locc,
        row(ssm_d[0]),
        _lanes_by_group(ssm_glu_w[0].astype(f32), HALF_GROUPS),
        row(ssm_glu_b[0]),
        row(ssm_norm_g[0]),
        pool_w[0].astype(f32),
        row(pool_scale[0]),
        row(pool_norm_g[0]),
        row(norm2_g[0]),
        row(final_norm_g),
    )
    hbm_weights = tuple(w.astype(f32) for w in (w_in, w_out, w_gate, w_up, w_down))
    out = _block(x.astype(f32), consts, hbm_weights, n_state=a4r.size, d_ssm=ssm_d[0].size,
                 d_pool=pool_scale[0].size, d_ff=w_gate.shape[-1])
    return out.astype(x.dtype)
```

```python
import jax
import jax.numpy as jnp
from jax import lax
from jax.experimental import pallas as pl
from jax.experimental.pallas import tpu as pltpu

N_META = 16
SSM_GROUP = 16
SSM_STATE = 64
POOL_WINDOWS = (2, 4, 8, 16)
EPS = 1e-6

BATCH_ROWS = 8
HALF_GROUPS = 16
PHASES = 4
QUAD_GROUPS = 4
SCAN_QUADS = 2
T_TILE = 64
FF_CHUNK = 256
STAGE_SLOTS = 8
WIDE_STAGE_ROWS = 32
NARROW_STAGE_ROWS = 128
VMEM_LIMIT = 60 * 1024 * 1024

_BF16 = jnp.bfloat16
_F32 = jnp.float32


def _rms(x, g):
    return x * lax.rsqrt(jnp.mean(x * x, axis=-1, keepdims=True) + EPS) * g


def _dot(a, b):
    return jnp.dot(a, b, preferred_element_type=_F32)


def _split_phases(a, n_chunks):
    blk = lambda t: a[t * BATCH_ROWS:(t + 1) * BATCH_ROWS]
    return [jnp.concatenate([blk(PHASES * k + p) for k in range(n_chunks)], axis=0) for p in range(PHASES)]


def _merge_phases(parts, n_chunks):
    blk = lambda a, k: a[k * BATCH_ROWS:(k + 1) * BATCH_ROWS]
    return jnp.concatenate([blk(p, k) for k in range(n_chunks) for p in parts], axis=0)


def _tile_copies(hbm, vmem, sems, slot, tile, nt, to_vmem):
    copies = []
    for b in range(BATCH_ROWS):
        h = hbm.at[b, pl.ds(tile * nt, nt), :]
        v = vmem.at[slot, :, b, :]
        src, dst = (h, v) if to_vmem else (v, h)
        copies.append(pltpu.make_async_copy(src, dst, sems.at[slot, b]))
    return copies


def _load_cast(w_hbm, dst, stage, sems, chunk, convert=None):
    if convert is None:
        convert = lambda rows: rows.astype(dst.dtype)
    n = dst.shape[0] // chunk
    ahead = STAGE_SLOTS - 1

    def copy(c, s):
        return pltpu.make_async_copy(w_hbm.at[0, pl.ds(c * chunk, chunk), :],
                                     stage.at[pl.ds(s * chunk, chunk), :], sems.at[s])

    for c in range(min(ahead, n)):
        copy(c, c).start()

    def body(c, carry):
        s = lax.rem(c, STAGE_SLOTS)

        @pl.when(c + ahead < n)
        def _():
            copy(c + ahead, lax.rem(c + ahead, STAGE_SLOTS)).start()

        copy(c, s).wait()
        src = pl.multiple_of(s * chunk, chunk)
        row = pl.multiple_of(c * chunk, chunk)
        dst[pl.ds(row, chunk), :] = convert(stage[pl.ds(src, chunk), :])
        return carry

    lax.fori_loop(0, n, body, 0)


def _expand_block_diag(src, dst, block_rows, block_lanes, n_blocks):
    lane = lax.broadcasted_iota(jnp.int32, src.shape, 1)
    lane_group = lax.rem(lane, n_blocks * block_lanes) // block_lanes
    for g in range(n_blocks):
        dst[g * block_rows:(g + 1) * block_rows, :] = jnp.where(lane_group == g, src, 0.0).astype(dst.dtype)


def _block_kernel(x_hbm, meta_ref, n1g_ref, a4r_ref, a4i_ref, zc_ref, vc_ref, locc_ref, d_ref, gluc_ref,
                  gb_ref, sng_ref, pw_ref, ps_ref, png_ref, n2g_ref, fg_ref,
                  win_hbm, wout_hbm, wg_hbm, wu_hbm, wd_hbm, o_hbm,
                  xbuf, xsem, obuf, osem, hbuf, zr, zi, st_r, st_i, u_last, halo, act,
                  win_ref, wout_ref, wg_ref, wu_ref, wd_ref, wz_ref, v_ref, loc_ref, gw_ref,
                  narrow_stage, wide_stage, wsem):
    i = pl.program_id(0)
    n_tiles = pl.num_programs(0) - 1
    slot = lax.rem(i, 2)
    oslot = 1 - slot

    def fetch(tile, s):
        return _tile_copies(x_hbm, xbuf, xsem, s, tile, T_TILE, to_vmem=True)

    def writeback(tile, s):
        return _tile_copies(o_hbm, obuf, osem, s, tile, T_TILE, to_vmem=False)

    d_ssm = d_ref.shape[-1]
    half_in = HALF_GROUPS * SSM_GROUP
    quad_in = QUAD_GROUPS * SSM_GROUP
    quad_state = QUAD_GROUPS * SSM_STATE
    n_quads = d_ssm // quad_in
    n_halves = d_ssm // half_in
    halo_rows = halo.shape[0]
    n_ff_chunks = wg_ref.shape[-1] // FF_CHUNK

    def build_weights():
        gd = pw_ref.shape[-1]
        pool_maps = [pw_ref[k] * ps_ref[:, k * gd:(k + 1) * gd] for k in range(pw_ref.shape[0])]

        def fold_pool(rows):
            cols = [rows[:, :d_ssm]]
            for k, pm in enumerate(pool_maps):
                cols.append(jnp.dot(rows[:, d_ssm + k * gd:d_ssm + (k + 1) * gd], pm,
                                    precision=lax.Precision.HIGHEST, preferred_element_type=_F32))
            return jnp.concatenate(cols, axis=-1).astype(_BF16)

        _load_cast(win_hbm, win_ref, narrow_stage, wsem, NARROW_STAGE_ROWS, convert=fold_pool)
        _load_cast(wout_hbm, wout_ref, narrow_stage, wsem, NARROW_STAGE_ROWS)
        _load_cast(wd_hbm, wd_ref, narrow_stage, wsem, NARROW_STAGE_ROWS)
        _load_cast(wg_hbm, wg_ref, wide_stage, wsem, WIDE_STAGE_ROWS)
        _load_cast(wu_hbm, wu_ref, wide_stage, wsem, WIDE_STAGE_ROWS)
        ph_rows = QUAD_GROUPS * SSM_GROUP
        for q in range(n_quads):
            for ri in range(2):
                for ph in range(PHASES):
                    _expand_block_diag(zc_ref[ri, ph, q], wz_ref.at[ri, q, ph * ph_rows:(ph + 1) * ph_rows, :],
                                       SSM_GROUP, SSM_STATE, QUAD_GROUPS)
                by_phase = jnp.concatenate([vc_ref[ri, ph, q] for ph in range(PHASES)], axis=-1)
                _expand_block_diag(by_phase, v_ref.at[ri, q], SSM_STATE, SSM_GROUP, QUAD_GROUPS)
            for j in range(PHASES):
                by_phase = jnp.concatenate([locc_ref[j, p, q] for p in range(PHASES)], axis=-1)
                _expand_block_diag(by_phase, loc_ref.at[q, j * ph_rows:(j + 1) * ph_rows, :],
                                   SSM_GROUP, SSM_GROUP, QUAD_GROUPS)
        for hh in range(n_halves):
            _expand_block_diag(gluc_ref[hh], gw_ref.at[hh], SSM_GROUP, SSM_GROUP, HALF_GROUPS)

    def project(rows):
        n1 = _rms(rows, n1g_ref[...]).astype(_BF16)
        return _dot(n1, win_ref[...])

    def quad_lanes(parts, q):
        return jnp.concatenate([p[:, q * quad_in:(q + 1) * quad_in] for p in parts], axis=1)

    def s5_states(u_in, n_chunks, qq):
        rows = n_chunks * BATCH_ROWS
        for q in range(qq * SCAN_QUADS, (qq + 1) * SCAN_QUADS):
            qs = slice(q * quad_state, (q + 1) * quad_state)
            lhs = quad_lanes(u_in, q)
            zr[0:rows, qs] = _dot(lhs, wz_ref[0, q])
            zi[0:rows, qs] = _dot(lhs, wz_ref[1, q])
        cs = slice(qq * SCAN_QUADS * quad_state, (qq + 1) * SCAN_QUADS * quad_state)
        width = SCAN_QUADS * quad_state
        ar = jnp.broadcast_to(a4r_ref[:, cs], (BATCH_ROWS, width))
        ai = jnp.broadcast_to(a4i_ref[:, cs], (BATCH_ROWS, width))
        sr = st_r[:, cs]
        si = st_i[:, cs]
        for k in range(n_chunks):
            rs = slice(k * BATCH_ROWS, (k + 1) * BATCH_ROWS)
            sr, si = (ar * sr - ai * si + zr[rs, cs], ar * si + ai * sr + zi[rs, cs])
            zr[rs, cs] = sr
            zi[rs, cs] = si
        st_r[:, cs] = sr
        st_i[:, cs] = si

    def s5_inputs(u, n_chunks):
        parts = _split_phases(u, n_chunks)
        state_in = []
        for p in range(1, PHASES):
            carry = slice((p - 1) * BATCH_ROWS, p * BATCH_ROWS)
            state_in.append(jnp.concatenate([u_last[carry, :], parts[p][:-BATCH_ROWS]], axis=0).astype(_BF16))
            u_last[carry, :] = parts[p][-BATCH_ROWS:]
        local_in = [p.astype(_BF16) for p in parts]
        return state_in + [local_in[0]], local_in

    def ffn_chunk(n2, c):
        cs = slice(c * FF_CHUNK, (c + 1) * FF_CHUNK)
        gate = _dot(n2, wg_ref[:, cs])
        up = _dot(n2, wu_ref[:, cs])
        act[:, cs] = (gate * jax.nn.sigmoid(gate) * up).astype(_BF16)

    def step(with_mixer, with_ffn):
        rows = T_TILE * BATCH_ROWS
        n_chunks = T_TILE // PHASES
        crows = n_chunks * BATCH_ROWS
        ff = iter(range(n_ff_chunks if with_ffn else 0))

        def ffn_chunks(n):
            for _ in range(n):
                c = next(ff, None)
                if c is not None:
                    ffn_chunk(n2, c)

        def ffn_tail():
            ffn_chunks(n_ff_chunks)
            out = _rms(h + _dot(act[...], wd_ref[...]), fg_ref[...])
            obuf[oslot] = out.reshape(obuf.shape[1:])

        if with_ffn:
            h = hbuf[...]
        if not with_mixer:
            n2 = _rms(h, n2g_ref[...]).astype(_BF16)
            ffn_tail()
            return

        x = xbuf[slot].reshape(rows, xbuf.shape[-1])
        n1 = _rms(x, n1g_ref[...]).astype(_BF16)
        proj = _dot(n1, win_ref[...])
        if with_ffn:
            n2 = _rms(h, n2g_ref[...]).astype(_BF16)
        u = proj[:, :d_ssm]
        v = proj[:, d_ssm:]

        state_in, local_in = s5_inputs(u, n_chunks)
        for qq in range(n_quads // SCAN_QUADS):
            s5_states(state_in, n_chunks, qq)
            ffn_chunks(1)
        chunk_out = []
        for q in range(n_quads):
            qs = slice(q * quad_state, (q + 1) * quad_state)
            chunk_out.append(_dot(zr[0:crows, qs].astype(_BF16), v_ref[0, q])
                             + _dot(zi[0:crows, qs].astype(_BF16), v_ref[1, q])
                             + _dot(quad_lanes(local_in, q), loc_ref[q]))
            if q % 4 == 3:
                ffn_chunks(1)
        y_parts = [jnp.concatenate([c[:, p * quad_in:(p + 1) * quad_in] for c in chunk_out], axis=-1)
                   for p in range(PHASES)]
        y = _merge_phases(y_parts, n_chunks) + d_ref[...] * u
        g = jax.nn.gelu(y)
        gb = g.astype(_BF16)
        ffn_chunks(1)
        gate = jnp.concatenate([_dot(gb[:, hh * half_in:(hh + 1) * half_in], gw_ref[hh])
                                for hh in range(n_halves)], axis=-1) + gb_ref[...]
        y_ssm = _rms(g * jax.nn.sigmoid(gate), sng_ref[...])
        ffn_chunks(1)

        vext = jnp.concatenate([halo[...], v], axis=0)
        halo[...] = v[rows - halo_rows:, :]
        gd = v.shape[-1] // len(POOL_WINDOWS)
        yps = []
        for k, w in enumerate(POOL_WINDOWS):
            acc = vext[:, k * gd:(k + 1) * gd]
            span = 1
            while span < w:
                shift = span * BATCH_ROWS
                acc = acc[shift:, :] + acc[:-shift, :]
                span *= 2
            yps.append(acc[acc.shape[0] - rows:, :] * (1.0 / w) - v[:, k * gd:(k + 1) * gd])
        ffn_chunks(1)
        y_pool = _rms(jnp.concatenate(yps, axis=-1), png_ref[...])

        mixed = jnp.concatenate([y_ssm, y_pool], axis=-1).astype(_BF16)
        if with_ffn:
            ffn_tail()
        hbuf[...] = x + _dot(mixed, wout_ref[...])

    @pl.when(i == 0)
    def _():
        for c in fetch(0, 0):
            c.start()

    @pl.when(i + 1 < n_tiles)
    def _():
        for c in fetch(i + 1, oslot):
            c.start()

    @pl.when(i >= 3)
    def _():
        for c in writeback(i - 3, oslot):
            c.wait()

    @pl.when(jnp.logical_and(i > 0, i < n_tiles))
    def _():
        for c in fetch(i, slot):
            c.wait()
        step(with_mixer=True, with_ffn=True)
        for c in writeback(i - 1, oslot):
            c.start()

    @pl.when(i == 0)
    def _():
        build_weights()
        st_r[...] = jnp.zeros_like(st_r)
        st_i[...] = jnp.zeros_like(st_i)
        u_last[...] = jnp.zeros_like(u_last)
        pm = project(meta_ref[...])
        state_in, _ = s5_inputs(pm[:, :d_ssm], N_META // PHASES)
        for qq in range(n_quads // SCAN_QUADS):
            s5_states(state_in, N_META // PHASES, qq)
        halo[...] = pm[:, d_ssm:]
        for c in fetch(i, slot):
            c.wait()
        step(with_mixer=True, with_ffn=False)

    @pl.when(i == n_tiles)
    def _():
        step(with_mixer=False, with_ffn=True)
        for c in writeback(i - 1, oslot):
            c.start()

        @pl.when(n_tiles >= 2)
        def _():
            for c in writeback(i - 2, slot):
                c.wait()
        for c in writeback(i - 1, oslot):
            c.wait()


def _const_spec(a):
    nd = a.ndim
    return pl.BlockSpec(a.shape, lambda i: (0,) * nd, pipeline_mode=pl.Buffered(1))


def _block(x, consts, hbm_weights, n_state, d_ssm, d_pool, d_ff):
    B, S, D = x.shape
    rows = T_TILE * B
    any_spec = pl.BlockSpec(memory_space=pl.ANY)
    n_quad = d_ssm // (QUAD_GROUPS * SSM_GROUP)
    n_half = d_ssm // (HALF_GROUPS * SSM_GROUP)
    tile_k = PHASES * QUAD_GROUPS * SSM_GROUP
    quad_state = QUAD_GROUPS * SSM_STATE
    return pl.pallas_call(
        _block_kernel,
        grid=(S // T_TILE + 1,),
        in_specs=[any_spec] + [_const_spec(c) for c in consts] + [any_spec] * len(hbm_weights),
        out_specs=any_spec,
        out_shape=jax.ShapeDtypeStruct((B, S, D), _F32),
        scratch_shapes=[
            pltpu.VMEM((2, T_TILE, B, D), _F32),
            pltpu.SemaphoreType.DMA((2, BATCH_ROWS)),
            pltpu.VMEM((2, T_TILE, B, D), _F32),
            pltpu.SemaphoreType.DMA((2, BATCH_ROWS)),
            pltpu.VMEM((rows, D), _F32),
            pltpu.VMEM((rows // PHASES, n_state), _F32),
            pltpu.VMEM((rows // PHASES, n_state), _F32),
            pltpu.VMEM((B, n_state), _F32),
            pltpu.VMEM((B, n_state), _F32),
            pltpu.VMEM(((PHASES - 1) * B, d_ssm), _F32),
            pltpu.VMEM((N_META * B, d_pool), _F32),
            pltpu.VMEM((rows, d_ff), _BF16),
            pltpu.VMEM((D, D), _BF16),
            pltpu.VMEM((D, D), _BF16),
            pltpu.VMEM((D, d_ff), _BF16),
            pltpu.VMEM((D, d_ff), _BF16),
            pltpu.VMEM((d_ff, D), _BF16),
            pltpu.VMEM((2, n_quad, tile_k, quad_state), _BF16),
            pltpu.VMEM((2, n_quad, quad_state, tile_k), _BF16),
            pltpu.VMEM((n_quad, tile_k, tile_k), _BF16),
            pltpu.VMEM((n_half, HALF_GROUPS * SSM_GROUP, HALF_GROUPS * SSM_GROUP), _BF16),
            pltpu.VMEM((STAGE_SLOTS * NARROW_STAGE_ROWS, D), _F32),
            pltpu.VMEM((STAGE_SLOTS * WIDE_STAGE_ROWS, d_ff), _F32),
            pltpu.SemaphoreType.DMA((STAGE_SLOTS,)),
        ],
        compiler_params=pltpu.CompilerParams(dimension_semantics=("arbitrary",),
                                             vmem_limit_bytes=VMEM_LIMIT),
        name="block",
    )(x, *consts, *hbm_weights)


def _lanes_by_group(blocks, per):
    G, r, c = blocks.shape
    return jnp.transpose(blocks.reshape(G // per, per, r, c), (0, 2, 1, 3)).reshape(G // per, r, per * c)


def _cmul(ar, ai, br, bi):
    return ar * br - ai * bi, ar * bi + ai * br


def _s5_params(lam_re, lam_im, log_step, b_re, b_im, c_re, c_im):
    lr = jnp.minimum(lam_re, -1e-4)
    li = lam_im
    step = jnp.exp(log_step)[:, None]
    mag = jnp.exp(lr * step)
    ang = li * step
    abr = mag * jnp.cos(ang)
    abi = mag * jnp.sin(ang)
    nr = abr - 1.0
    ni = abi
    den = lr * lr + li * li
    cr = ((nr * lr + ni * li) / den)[..., None]
    ci = ((ni * lr - nr * li) / den)[..., None]
    bbr = cr * b_re - ci * b_im
    bbi = cr * b_im + ci * b_re
    pw = [(jnp.ones_like(abr), jnp.zeros_like(abi))]
    for _ in range(PHASES):
        pw.append(_cmul(pw[-1][0], pw[-1][1], abr, abi))
    G = bbr.shape[0]
    Q = G // QUAD_GROUPS
    t = lambda a: jnp.swapaxes(a, 1, 2)

    def by_quad(blocks, lead):
        packed = _lanes_by_group(jnp.concatenate(blocks, axis=0), QUAD_GROUPS)
        return packed.reshape(lead + (Q,) + packed.shape[1:])

    ab = [_cmul(pr[..., None], pi[..., None], bbr, bbi) for pr, pi in pw[:PHASES]]
    ca = [_cmul(c_re, c_im, pr[:, None, :], pi[:, None, :]) for pr, pi in pw[:PHASES]]
    zc = by_quad([t(ab[PHASES - 1 - ph][ri]) for ri in range(2) for ph in range(PHASES)], (2, PHASES))
    sign = (1.0, -1.0)
    vc = by_quad([sign[ri] * t(ca[ph][ri]) for ri in range(2) for ph in range(PHASES)], (2, PHASES))
    loc = [jnp.einsum('ghp,gpk->ghk', ca[d][0], bbr) - jnp.einsum('ghp,gpk->ghk', ca[d][1], bbi)
           for d in range(PHASES - 1)]
    zero = jnp.zeros_like(loc[0])
    locc = by_quad([t(loc[p - j]) if 1 <= j <= p else zero for j in range(PHASES) for p in range(PHASES)],
                   (PHASES, PHASES))
    return pw[PHASES], zc, vc, locc


def kernel(x, meta_tokens, norm1_g, w_in, ssm_lambda_re, ssm_lambda_im, ssm_log_step, ssm_b_re, ssm_b_im, ssm_c_re, ssm_c_im, ssm_d, ssm_glu_w, ssm_glu_b, ssm_norm_g, pool_w, pool_scale, pool_norm_g, w_out, norm2_g, w_gate, w_up, w_down, final_norm_g):
    B, S, D = x.shape
    assert B == BATCH_ROWS and norm1_g.shape[0] == 1
    assert S % T_TILE == 0 and T_TILE % PHASES == 0 and N_META % PHASES == 0
    f32 = _F32
    (a4r, a4i), zc, vc, locc = _s5_params(
        ssm_lambda_re[0].astype(f32), ssm_lambda_im[0].astype(f32), ssm_log_step[0].astype(f32),
        ssm_b_re[0].astype(f32), ssm_b_im[0].astype(f32), ssm_c_re[0].astype(f32), ssm_c_im[0].astype(f32))
    row = lambda a: a.astype(f32).reshape(1, -1)
    consts = (
        jnp.repeat(meta_tokens.astype(f32), B, axis=0),
        row(norm1_g[0]),
        row(a4r), row(a4i),
        zc,
        vc,
        locc,
        row(ssm_d[0]),
        _lanes_by_group(ssm_glu_w[0].astype(f32), HALF_GROUPS),
        row(ssm_glu_b[0]),
        row(ssm_norm_g[0]),
        pool_w[0].astype(f32),
        row(pool_scale[0]),
        row(pool_norm_g[0]),
        row(norm2_g[0]),
        row(final_norm_g),
    )
    hbm_weights = tuple(w.astype(f32) for w in (w_in, w_out, w_gate, w_up, w_down))
    out = _block(x.astype(f32), consts, hbm_weights, n_state=a4r.size, d_ssm=ssm_d[0].size,
                 d_pool=pool_scale[0].size, d_ff=w_gate.shape[-1])
    return out.astype(x.dtype)
```

```python
import jax
import jax.numpy as jnp
from jax import lax
from jax.experimental import pallas as pl
from jax.experimental.pallas import tpu as pltpu

N_META = 16
SSM_GROUP = 16
SSM_STATE = 64
POOL_WINDOWS = (2, 4, 8, 16)
EPS = 1e-6

BATCH_ROWS = 8
HALF_GROUPS = 16
PHASES = 4
QUAD_GROUPS = 4
SCAN_QUADS = 2
T_TILE = 64
FF_CHUNK = 256
STAGE_SLOTS = 8
WIDE_STAGE_ROWS = 32
NARROW_STAGE_ROWS = 128
VMEM_LIMIT = 60 * 1024 * 1024

_BF16 = jnp.bfloat16
_F32 = jnp.float32


def _rms(x, g):
    return x * lax.rsqrt(jnp.mean(x * x, axis=-1, keepdims=True) + EPS) * g


def _dot(a, b):
    return jnp.dot(a, b, preferred_element_type=_F32)


def _split_phases(a, n_chunks):
    blk = lambda t: a[t * BATCH_ROWS:(t + 1) * BATCH_ROWS]
    return [jnp.concatenate([blk(PHASES * k + p) for k in range(n_chunks)], axis=0) for p in range(PHASES)]


def _merge_phases(parts, n_chunks):
    blk = lambda a, k: a[k * BATCH_ROWS:(k + 1) * BATCH_ROWS]
    return jnp.concatenate([blk(p, k) for k in range(n_chunks) for p in parts], axis=0)


def _tile_copies(hbm, vmem, sems, slot, tile, nt, to_vmem):
    copies = []
    for b in range(BATCH_ROWS):
        h = hbm.at[b, pl.ds(tile * nt, nt), :]
        v = vmem.at[slot, :, b, :]
        src, dst = (h, v) if to_vmem else (v, h)
        copies.append(pltpu.make_async_copy(src, dst, sems.at[slot, b]))
    return copies


def _load_cast(w_hbm, dst, stage, sems, chunk, convert=None):
    if convert is None:
        convert = lambda rows: rows.astype(dst.dtype)
    n = dst.shape[0] // chunk
    ahead = STAGE_SLOTS - 1

    def copy(c, s):
        return pltpu.make_async_copy(w_hbm.at[0, pl.ds(c * chunk, chunk), :],
                                     stage.at[pl.ds(s * chunk, chunk), :], sems.at[s])

    for c in range(min(ahead, n)):
        copy(c, c).start()

    def body(c, carry):
        s = lax.rem(c, STAGE_SLOTS)

        @pl.when(c + ahead < n)
        def _():
            copy(c + ahead, lax.rem(c + ahead, STAGE_SLOTS)).start()

        copy(c, s).wait()
        src = pl.multiple_of(s * chunk, chunk)
        row = pl.multiple_of(c * chunk, chunk)
        dst[pl.ds(row, chunk), :] = convert(stage[pl.ds(src, chunk), :])
        return carry

    lax.fori_loop(0, n, body, 0)


def _expand_block_diag(src, dst, block_rows, block_lanes, n_blocks):
    lane = lax.broadcasted_iota(jnp.int32, src.shape, 1)
    lane_group = lax.rem(lane, n_blocks * block_lanes) // block_lanes
    for g in range(n_blocks):
        dst[g * block_rows:(g + 1) * block_rows, :] = jnp.where(lane_group == g, src, 0.0).astype(dst.dtype)


def _block_kernel(x_hbm, meta_ref, n1g_ref, a4r_ref, a4i_ref, zc_ref, vc_ref, locc_ref, d_ref, gluc_ref,
                  gb_ref, sng_ref, pw_ref, ps_ref, png_ref, n2g_ref, fg_ref,
                  win_hbm, wout_hbm, wg_hbm, wu_hbm, wd_hbm, o_hbm,
                  xbuf, xsem, obuf, osem, hbuf, zr, zi, st_r, st_i, u_last, halo, act,
                  win_ref, wout_ref, wg_ref, wu_ref, wd_ref, wz_ref, v_ref, loc_ref, gw_ref,
                  narrow_stage, wide_stage, wsem):
    i = pl.program_id(0)
    n_tiles = pl.num_programs(0) - 1
    slot = lax.rem(i, 2)
    oslot = 1 - slot

    def fetch(tile, s):
        return _tile_copies(x_hbm, xbuf, xsem, s, tile, T_TILE, to_vmem=True)

    def writeback(tile, s):
        return _tile_copies(o_hbm, obuf, osem, s, tile, T_TILE, to_vmem=False)

    d_ssm = d_ref.shape[-1]
    half_in = HALF_GROUPS * SSM_GROUP
    quad_in = QUAD_GROUPS * SSM_GROUP
    quad_state = QUAD_GROUPS * SSM_STATE
    n_quads = d_ssm // quad_in
    n_halves = d_ssm // half_in
    halo_rows = halo.shape[0]
    n_ff_chunks = wg_ref.shape[-1] // FF_CHUNK

    def build_weights():
        gd = pw_ref.shape[-1]
        pool_maps = [pw_ref[k] * ps_ref[:, k * gd:(k + 1) * gd] for k in range(pw_ref.shape[0])]

        def fold_pool(rows):
            cols = [rows[:, :d_ssm]]
            for k, pm in enumerate(pool_maps):
                cols.append(jnp.dot(rows[:, d_ssm + k * gd:d_ssm + (k + 1) * gd], pm,
                                    precision=lax.Precision.HIGHEST, preferred_element_type=_F32))
            return jnp.concatenate(cols, axis=-1).astype(_BF16)

        _load_cast(win_hbm, win_ref, narrow_stage, wsem, NARROW_STAGE_ROWS, convert=fold_pool)
        _load_cast(wout_hbm, wout_ref, narrow_stage, wsem, NARROW_STAGE_ROWS)
        _load_cast(wd_hbm, wd_ref, narrow_stage, wsem, NARROW_STAGE_ROWS)
        _load_cast(wg_hbm, wg_ref, wide_stage, wsem, WIDE_STAGE_ROWS)
        _load_cast(wu_hbm, wu_ref, wide_stage, wsem, WIDE_STAGE_ROWS)
        ph_rows = QUAD_GROUPS * SSM_GROUP
        for q in range(n_quads):
            for ri in range(2):
                for ph in range(PHASES):
                    _expand_block_diag(zc_ref[ri, ph, q], wz_ref.at[ri, q, ph * ph_rows:(ph + 1) * ph_rows, :],
                                       SSM_GROUP, SSM_STATE, QUAD_GROUPS)
                by_phase = jnp.concatenate([vc_ref[ri, ph, q] for ph in range(PHASES)], axis=-1)
                _expand_block_diag(by_phase, v_ref.at[ri, q], SSM_STATE, SSM_GROUP, QUAD_GROUPS)
            for j in range(PHASES):
                by_phase = jnp.concatenate([locc_ref[j, p, q] for p in range(PHASES)], axis=-1)
                _expand_block_diag(by_phase, loc_ref.at[q, j * ph_rows:(j + 1) * ph_rows, :],
                                   SSM_GROUP, SSM_GROUP, QUAD_GROUPS)
        for hh in range(n_halves):
            _expand_block_diag(gluc_ref[hh], gw_ref.at[hh], SSM_GROUP, SSM_GROUP, HALF_GROUPS)

    def project(rows):
        n1 = _rms(rows, n1g_ref[...]).astype(_BF16)
        return _dot(n1, win_ref[...])

    def quad_lanes(parts, q):
        return jnp.concatenate([p[:, q * quad_in:(q + 1) * quad_in] for p in parts], axis=1)

    def s5_states(u_in, n_chunks, qq):
        rows = n_chunks * BATCH_ROWS
        for q in range(qq * SCAN_QUADS, (qq + 1) * SCAN_QUADS):
            qs = slice(q * quad_state, (q + 1) * quad_state)
            lhs = quad_lanes(u_in, q)
            zr[0:rows, qs] = _dot(lhs, wz_ref[0, q])
            zi[0:rows, qs] = _dot(lhs, wz_ref[1, q])
        cs = slice(qq * SCAN_QUADS * quad_state, (qq + 1) * SCAN_QUADS * quad_state)
        width = SCAN_QUADS * quad_state
        ar = jnp.broadcast_to(a4r_ref[:, cs], (BATCH_ROWS, width))
        ai = jnp.broadcast_to(a4i_ref[:, cs], (BATCH_ROWS, width))
        sr = st_r[:, cs]
        si = st_i[:, cs]
        for k in range(n_chunks):
            rs = slice(k * BATCH_ROWS, (k + 1) * BATCH_ROWS)
            sr, si = (ar * sr - ai * si + zr[rs, cs], ar * si + ai * sr + zi[rs, cs])
            zr[rs, cs] = sr
            zi[rs, cs] = si
        st_r[:, cs] = sr
        st_i[:, cs] = si

    def s5_inputs(u, n_chunks):
        parts = _split_phases(u, n_chunks)
        state_in = []
        for p in range(1, PHASES):
            carry = slice((p - 1) * BATCH_ROWS, p * BATCH_ROWS)
            state_in.append(jnp.concatenate([u_last[carry, :], parts[p][:-BATCH_ROWS]], axis=0).astype(_BF16))
            u_last[carry, :] = parts[p][-BATCH_ROWS:]
        local_in = [p.astype(_BF16) for p in parts]
        return state_in + [local_in[0]], local_in

    def ffn_chunk(n2, c):
        cs = slice(c * FF_CHUNK, (c + 1) * FF_CHUNK)
        gate = _dot(n2, wg_ref[:, cs])
        up = _dot(n2, wu_ref[:, cs])
        act[:, cs] = (gate * jax.nn.sigmoid(gate) * up).astype(_BF16)

    def step(with_mixer, with_ffn):
        rows = T_TILE * BATCH_ROWS
        n_chunks = T_TILE // PHASES
        crows = n_chunks * BATCH_ROWS
        ff = iter(range(n_ff_chunks if with_ffn else 0))

        def ffn_chunks(n):
            for _ in range(n):
                c = next(ff, None)
                if c is not None:
                    ffn_chunk(n2, c)

        def ffn_tail():
            ffn_chunks(n_ff_chunks)
            out = _rms(h + _dot(act[...], wd_ref[...]), fg_ref[...])
            obuf[oslot] = out.reshape(obuf.shape[1:])

        if with_ffn:
            h = hbuf[...]
        if not with_mixer:
            n2 = _rms(h, n2g_ref[...]).astype(_BF16)
            ffn_tail()
            return

        x = xbuf[slot].reshape(rows, xbuf.shape[-1])
        n1 = _rms(x, n1g_ref[...]).astype(_BF16)
        proj = _dot(n1, win_ref[...])
        if with_ffn:
            n2 = _rms(h, n2g_ref[...]).astype(_BF16)
        u = proj[:, :d_ssm]
        v = proj[:, d_ssm:]

        state_in, local_in = s5_inputs(u, n_chunks)
        for qq in range(n_quads // SCAN_QUADS):
            s5_states(state_in, n_chunks, qq)
            ffn_chunks(1)
        chunk_out = []
        for q in range(n_quads):
            qs = slice(q * quad_state, (q + 1) * quad_state)
            chunk_out.append(_dot(zr[0:crows, qs].astype(_BF16), v_ref[0, q])
                             + _dot(zi[0:crows, qs].astype(_BF16), v_ref[1, q])
                             + _dot(quad_lanes(local_in, q), loc_ref[q]))
            if q % 4 == 3:
                ffn_chunks(1)
        y_parts = [jnp.concatenate([c[:, p * quad_in:(p + 1) * quad_in] for c in chunk_out], axis=-1)
                   for p in range(PHASES)]
        y = _merge_phases(y_parts, n_chunks) + d_ref[...] * u
        g = jax.nn.gelu(y)
        gb = g.astype(_BF16)
        ffn_chunks(1)
        gate = jnp.concatenate([_dot(gb[:, hh * half_in:(hh + 1) * half_in], gw_ref[hh])
                                for hh in range(n_halves)], axis=-1) + gb_ref[...]
        y_ssm = _rms(g * jax.nn.sigmoid(gate), sng_ref[...])
        ffn_chunks(1)

        vext = jnp.concatenate([halo[...], v], axis=0)
        halo[...] = v[rows - halo_rows:, :]
        gd = v.shape[-1] // len(POOL_WINDOWS)
        yps = []
        for k, w in enumerate(POOL_WINDOWS):
            acc = vext[:, k * gd:(k + 1) * gd]
            span = 1
            while span < w:
                shift = span * BATCH_ROWS
                acc = acc[shift:, :] + acc[:-shift, :]
                span *= 2
            yps.append(acc[acc.shape[0] - rows:, :] * (1.0 / w) - v[:, k * gd:(k + 1) * gd])
        ffn_chunks(1)
        y_pool = _rms(jnp.concatenate(yps, axis=-1), png_ref[...])

        mixed = jnp.concatenate([y_ssm, y_pool], axis=-1).astype(_BF16)
        if with_ffn:
            ffn_tail()
        hbuf[...] = x + _dot(mixed, wout_ref[...])

    @pl.when(i == 0)
    def _():
        for c in fetch(0, 0):
            c.start()

    @pl.when(i + 1 < n_tiles)
    def _():
        for c in fetch(i + 1, oslot):
            c.start()

    @pl.when(i >= 3)
    def _():
        for c in writeback(i - 3, oslot):
            c.wait()

    @pl.when(i == 0)
    def _():
        build_weights()
        st_r[...] = jnp.zeros_like(st_r)
        st_i[...] = jnp.zeros_like(st_i)
        u_last[...] = jnp.zeros_like(u_last)
        pm = project(meta_ref[...])
        state_in, _ = s5_inputs(pm[:, :d_ssm], N_META // PHASES)
        for qq in range(n_quads // SCAN_QUADS):
            s5_states(state_in, N_META // PHASES, qq)
        halo[...] = pm[:, d_ssm:]

    @pl.when(i < n_tiles)
    def _():
        for c in fetch(i, slot):
            c.wait()

    @pl.when(i == 0)
    def _():
        step(with_mixer=True, with_ffn=False)

    @pl.when(jnp.logical_and(i > 0, i < n_tiles))
    def _():
        step(with_mixer=True, with_ffn=True)

    @pl.when(i == n_tiles)
    def _():
        step(with_mixer=False, with_ffn=True)

    @pl.when(i >= 1)
    def _():
        for c in writeback(i - 1, oslot):
            c.start(priority=1)

    @pl.when(i == n_tiles)
    def _():
        @pl.when(n_tiles >= 2)
        def _():
            for c in writeback(i - 2, slot):
                c.wait()
        for c in writeback(i - 1, oslot):
            c.wait()


def _const_spec(a):
    nd = a.ndim
    return pl.BlockSpec(a.shape, lambda i: (0,) * nd, pipeline_mode=pl.Buffered(1))


def _block(x, consts, hbm_weights, n_state, d_ssm, d_pool, d_ff):
    B, S, D = x.shape
    rows = T_TILE * B
    any_spec = pl.BlockSpec(memory_space=pl.ANY)
    n_quad = d_ssm // (QUAD_GROUPS * SSM_GROUP)
    n_half = d_ssm // (HALF_GROUPS * SSM_GROUP)
    tile_k = PHASES * QUAD_GROUPS * SSM_GROUP
    quad_state = QUAD_GROUPS * SSM_STATE
    return pl.pallas_call(
        _block_kernel,
        grid=(S // T_TILE + 1,),
        in_specs=[any_spec] + [_const_spec(c) for c in consts] + [any_spec] * len(hbm_weights),
        out_specs=any_spec,
        out_shape=jax.ShapeDtypeStruct((B, S, D), _F32),
        scratch_shapes=[
            pltpu.VMEM((2, T_TILE, B, D), _F32),
            pltpu.SemaphoreType.DMA((2, BATCH_ROWS)),
            pltpu.VMEM((2, T_TILE, B, D), _F32),
            pltpu.SemaphoreType.DMA((2, BATCH_ROWS)),
            pltpu.VMEM((rows, D), _F32),
            pltpu.VMEM((rows // PHASES, n_state), _F32),
            pltpu.VMEM((rows // PHASES, n_state), _F32),
            pltpu.VMEM((B, n_state), _F32),
            pltpu.VMEM((B, n_state), _F32),
            pltpu.VMEM(((PHASES - 1) * B, d_ssm), _F32),
            pltpu.VMEM((N_META * B, d_pool), _F32),
            pltpu.VMEM((rows, d_ff), _BF16),
            pltpu.VMEM((D, D), _BF16),
            pltpu.VMEM((D, D), _BF16),
            pltpu.VMEM((D, d_ff), _BF16),
            pltpu.VMEM((D, d_ff), _BF16),
            pltpu.VMEM((d_ff, D), _BF16),
            pltpu.VMEM((2, n_quad, tile_k, quad_state), _BF16),
            pltpu.VMEM((2, n_quad, quad_state, tile_k), _BF16),
            pltpu.VMEM((n_quad, tile_k, tile_k), _BF16),
            pltpu.VMEM((n_half, HALF_GROUPS * SSM_GROUP, HALF_GROUPS * SSM_GROUP), _BF16),
            pltpu.VMEM((STAGE_SLOTS * NARROW_STAGE_ROWS, D), _F32),
            pltpu.VMEM((STAGE_SLOTS * WIDE_STAGE_ROWS, d_ff), _F32),
            pltpu.SemaphoreType.DMA((STAGE_SLOTS,)),
        ],
        compiler_params=pltpu.CompilerParams(dimension_semantics=("arbitrary",),
                                             vmem_limit_bytes=VMEM_LIMIT),
        name="block",
    )(x, *consts, *hbm_weights)


def _lanes_by_group(blocks, per):
    G, r, c = blocks.shape
    return jnp.transpose(blocks.reshape(G // per, per, r, c), (0, 2, 1, 3)).reshape(G // per, r, per * c)


def _cmul(ar, ai, br, bi):
    return ar * br - ai * bi, ar * bi + ai * br


def _s5_params(lam_re, lam_im, log_step, b_re, b_im, c_re, c_im):
    lr = jnp.minimum(lam_re, -1e-4)
    li = lam_im
    step = jnp.exp(log_step)[:, None]
    mag = jnp.exp(lr * step)
    ang = li * step
    abr = mag * jnp.cos(ang)
    abi = mag * jnp.sin(ang)
    nr = abr - 1.0
    ni = abi
    den = lr * lr + li * li
    cr = ((nr * lr + ni * li) / den)[..., None]
    ci = ((ni * lr - nr * li) / den)[..., None]
    bbr = cr * b_re - ci * b_im
    bbi = cr * b_im + ci * b_re
    pw = [(jnp.ones_like(abr), jnp.zeros_like(abi))]
    for _ in range(PHASES):
        pw.append(_cmul(pw[-1][0], pw[-1][1], abr, abi))
    G = bbr.shape[0]
    Q = G // QUAD_GROUPS
    t = lambda a: jnp.swapaxes(a, 1, 2)

    def by_quad(blocks, lead):
        packed = _lanes_by_group(jnp.concatenate(blocks, axis=0), QUAD_GROUPS)
        return packed.reshape(lead + (Q,) + packed.shape[1:])

    ab = [_cmul(pr[..., None], pi[..., None], bbr, bbi) for pr, pi in pw[:PHASES]]
    ca = [_cmul(c_re, c_im, pr[:, None, :], pi[:, None, :]) for pr, pi in pw[:PHASES]]
    zc = by_quad([t(ab[PHASES - 1 - ph][ri]) for ri in range(2) for ph in range(PHASES)], (2, PHASES))
    sign = (1.0, -1.0)
    vc = by_quad([sign[ri] * t(ca[ph][ri]) for ri in range(2) for ph in range(PHASES)], (2, PHASES))
    loc = [jnp.einsum('ghp,gpk->ghk', ca[d][0], bbr) - jnp.einsum('ghp,gpk->ghk', ca[d][1], bbi)
           for d in range(PHASES - 1)]
    zero = jnp.zeros_like(loc[0])
    locc = by_quad([t(loc[p - j]) if 1 <= j <= p else zero for j in range(PHASES) for p in range(PHASES)],
                   (PHASES, PHASES))
    return pw[PHASES], zc, vc, locc


def kernel(x, meta_tokens, norm1_g, w_in, ssm_lambda_re, ssm_lambda_im, ssm_log_step, ssm_b_re, ssm_b_im, ssm_c_re, ssm_c_im, ssm_d, ssm_glu_w, ssm_glu_b, ssm_norm_g, pool_w, pool_scale, pool_norm_g, w_out, norm2_g, w_gate, w_up, w_down, final_norm_g):
    B, S, D = x.shape
    assert B == BATCH_ROWS and norm1_g.shape[0] == 1
    assert S % T_TILE == 0 and T_TILE % PHASES == 0 and N_META % PHASES == 0
    f32 = _F32
    (a4r, a4i), zc, vc, locc = _s5_params(
        ssm_lambda_re[0].astype(f32), ssm_lambda_im[0].astype(f32), ssm_log_step[0].astype(f32),
        ssm_b_re[0].astype(f32), ssm_b_im[0].astype(f32), ssm_c_re[0].astype(f32), ssm_c_im[0].astype(f32))
    row = lambda a: a.astype(f32).reshape(1, -1)
    consts = (
        jnp.repeat(meta_tokens.astype(f32), B, axis=0),
        row(norm1_g[0]),
        row(a4r), row(a4i),
        zc,
        vc,
        locc,
        row(ssm_d[0]),
        _lanes_by_group(ssm_glu_w[0].astype(f32), HALF_GROUPS),
        row(ssm_glu_b[0]),
        row(ssm_norm_g[0]),
        pool_w[0].astype(f32),
        row(pool_scale[0]),
        row(pool_norm_g[0]),
        row(norm2_g[0]),
        row(final_norm_g),
    )
    hbm_weights = tuple(w.astype(f32) for w in (w_in, w_out, w_gate, w_up, w_down))
    out = _block(x.astype(f32), consts, hbm_weights, n_state=a4r.size, d_ssm=ssm_d[0].size,
                 d_pool=pool_scale[0].size, d_ff=w_gate.shape[-1])
    return out.astype(x.dtype)
```

```python
import jax
import jax.numpy as jnp
from jax import lax
from jax.experimental import pallas as pl
from jax.experimental.pallas import tpu as pltpu

N_META = 16
SSM_GROUP = 16
SSM_STATE = 64
POOL_WINDOWS = (2, 4, 8, 16)
EPS = 1e-6

BATCH_ROWS = 8
HALF_GROUPS = 16
PHASES = 4
QUAD_GROUPS = 4
SCAN_QUADS = 2
T_TILE = 64
FF_CHUNK = 256
NARROW_STAGE_SLOTS = 8
WIDE_STAGE_SLOTS = 12
WIDE_STAGE_ROWS = 32
NARROW_STAGE_ROWS = 128
VMEM_LIMIT = 60 * 1024 * 1024

_BF16 = jnp.bfloat16
_F32 = jnp.float32


def _rms(x, g):
    return x * lax.rsqrt(jnp.mean(x * x, axis=-1, keepdims=True) + EPS) * g


def _dot(a, b):
    return jnp.dot(a, b, preferred_element_type=_F32)


def _split_phases(a, n_chunks):
    blk = lambda t: a[t * BATCH_ROWS:(t + 1) * BATCH_ROWS]
    return [jnp.concatenate([blk(PHASES * k + p) for k in range(n_chunks)], axis=0) for p in range(PHASES)]


def _merge_phases(parts, n_chunks):
    blk = lambda a, k: a[k * BATCH_ROWS:(k + 1) * BATCH_ROWS]
    return jnp.concatenate([blk(p, k) for k in range(n_chunks) for p in parts], axis=0)


def _tile_copies(hbm, vmem, sems, slot, tile, nt, to_vmem):
    copies = []
    for b in range(BATCH_ROWS):
        h = hbm.at[b, pl.ds(tile * nt, nt), :]
        v = vmem.at[slot, :, b, :]
        src, dst = (h, v) if to_vmem else (v, h)
        copies.append(pltpu.make_async_copy(src, dst, sems.at[slot, b]))
    return copies


def _load_cast(w_hbm, dst, stage, sems, chunk, convert=None):
    if convert is None:
        convert = lambda rows: rows.astype(dst.dtype)
    n = dst.shape[0] // chunk
    slots = stage.shape[0] // chunk
    ahead = slots - 1

    def copy(c, s):
        return pltpu.make_async_copy(w_hbm.at[0, pl.ds(c * chunk, chunk), :],
                                     stage.at[pl.ds(s * chunk, chunk), :], sems.at[s])

    for c in range(min(ahead, n)):
        copy(c, c).start()

    def body(c, carry):
        s = lax.rem(c, slots)

        @pl.when(c + ahead < n)
        def _():
            copy(c + ahead, lax.rem(c + ahead, slots)).start()

        copy(c, s).wait()
        src = pl.multiple_of(s * chunk, chunk)
        row = pl.multiple_of(c * chunk, chunk)
        dst[pl.ds(row, chunk), :] = convert(stage[pl.ds(src, chunk), :])
        return carry

    lax.fori_loop(0, n, body, 0)


def _expand_block_diag(src, dst, block_rows, block_lanes, n_blocks):
    lane = lax.broadcasted_iota(jnp.int32, src.shape, 1)
    lane_group = lax.rem(lane, n_blocks * block_lanes) // block_lanes
    for g in range(n_blocks):
        dst[g * block_rows:(g + 1) * block_rows, :] = jnp.where(lane_group == g, src, 0.0).astype(dst.dtype)


def _block_kernel(x_hbm, meta_ref, n1g_ref, a4r_ref, a4i_ref, zc_ref, vc_ref, locc_ref, d_ref, gluc_ref,
                  gb_ref, sng_ref, pw_ref, ps_ref, png_ref, n2g_ref, fg_ref,
                  win_hbm, wout_hbm, wg_hbm, wu_hbm, wd_hbm, o_hbm,
                  xbuf, xsem, obuf, osem, hbuf, zr, zi, st_r, st_i, u_last, halo, act,
                  win_ref, wout_ref, wg_ref, wu_ref, wd_ref, wz_ref, v_ref, loc_ref, gw_ref,
                  narrow_stage, wide_stage, wsem):
    i = pl.program_id(0)
    n_tiles = pl.num_programs(0) - 1
    slot = lax.rem(i, 2)
    oslot = 1 - slot

    def fetch(tile, s):
        return _tile_copies(x_hbm, xbuf, xsem, s, tile, T_TILE, to_vmem=True)

    def writeback(tile, s):
        return _tile_copies(o_hbm, obuf, osem, s, tile, T_TILE, to_vmem=False)

    d_ssm = d_ref.shape[-1]
    half_in = HALF_GROUPS * SSM_GROUP
    quad_in = QUAD_GROUPS * SSM_GROUP
    quad_state = QUAD_GROUPS * SSM_STATE
    n_quads = d_ssm // quad_in
    n_halves = d_ssm // half_in
    halo_rows = halo.shape[0]
    n_ff_chunks = wg_ref.shape[-1] // FF_CHUNK

    def build_weights():
        gd = pw_ref.shape[-1]
        pool_maps = [pw_ref[k] * ps_ref[:, k * gd:(k + 1) * gd] for k in range(pw_ref.shape[0])]

        def fold_pool(rows):
            cols = [rows[:, :d_ssm]]
            for k, pm in enumerate(pool_maps):
                cols.append(jnp.dot(rows[:, d_ssm + k * gd:d_ssm + (k + 1) * gd], pm,
                                    precision=lax.Precision.HIGHEST, preferred_element_type=_F32))
            return jnp.concatenate(cols, axis=-1).astype(_BF16)

        _load_cast(win_hbm, win_ref, narrow_stage, wsem, NARROW_STAGE_ROWS, convert=fold_pool)
        _load_cast(wout_hbm, wout_ref, narrow_stage, wsem, NARROW_STAGE_ROWS)
        _load_cast(wd_hbm, wd_ref, narrow_stage, wsem, NARROW_STAGE_ROWS)
        _load_cast(wg_hbm, wg_ref, wide_stage, wsem, WIDE_STAGE_ROWS)
        _load_cast(wu_hbm, wu_ref, wide_stage, wsem, WIDE_STAGE_ROWS)
        ph_rows = QUAD_GROUPS * SSM_GROUP
        for q in range(n_quads):
            for ri in range(2):
                for ph in range(PHASES):
                    _expand_block_diag(zc_ref[ri, ph, q], wz_ref.at[ri, q, ph * ph_rows:(ph + 1) * ph_rows, :],
                                       SSM_GROUP, SSM_STATE, QUAD_GROUPS)
                by_phase = jnp.concatenate([vc_ref[ri, ph, q] for ph in range(PHASES)], axis=-1)
                _expand_block_diag(by_phase, v_ref.at[ri, q], SSM_STATE, SSM_GROUP, QUAD_GROUPS)
            for j in range(PHASES):
                by_phase = jnp.concatenate([locc_ref[j, p, q] for p in range(PHASES)], axis=-1)
                _expand_block_diag(by_phase, loc_ref.at[q, j * ph_rows:(j + 1) * ph_rows, :],
                                   SSM_GROUP, SSM_GROUP, QUAD_GROUPS)
        for hh in range(n_halves):
            _expand_block_diag(gluc_ref[hh], gw_ref.at[hh], SSM_GROUP, SSM_GROUP, HALF_GROUPS)

    def project(rows):
        n1 = _rms(rows, n1g_ref[...]).astype(_BF16)
        return _dot(n1, win_ref[...])

    def quad_lanes(parts, q):
        return jnp.concatenate([p[:, q * quad_in:(q + 1) * quad_in] for p in parts], axis=1)

    def s5_states(u_in, n_chunks, qq):
        rows = n_chunks * BATCH_ROWS
        for q in range(qq * SCAN_QUADS, (qq + 1) * SCAN_QUADS):
            qs = slice(q * quad_state, (q + 1) * quad_state)
            lhs = quad_lanes(u_in, q)
            zr[0:rows, qs] = _dot(lhs, wz_ref[0, q])
            zi[0:rows, qs] = _dot(lhs, wz_ref[1, q])
        cs = slice(qq * SCAN_QUADS * quad_state, (qq + 1) * SCAN_QUADS * quad_state)
        width = SCAN_QUADS * quad_state
        ar = jnp.broadcast_to(a4r_ref[:, cs], (BATCH_ROWS, width))
        ai = jnp.broadcast_to(a4i_ref[:, cs], (BATCH_ROWS, width))
        sr = st_r[:, cs]
        si = st_i[:, cs]
        for k in range(n_chunks):
            rs = slice(k * BATCH_ROWS, (k + 1) * BATCH_ROWS)
            sr, si = (ar * sr - ai * si + zr[rs, cs], ar * si + ai * sr + zi[rs, cs])
            zr[rs, cs] = sr
            zi[rs, cs] = si
        st_r[:, cs] = sr
        st_i[:, cs] = si

    def s5_inputs(u, n_chunks):
        parts = _split_phases(u, n_chunks)
        state_in = []
        for p in range(1, PHASES):
            carry = slice((p - 1) * BATCH_ROWS, p * BATCH_ROWS)
            state_in.append(jnp.concatenate([u_last[carry, :], parts[p][:-BATCH_ROWS]], axis=0).astype(_BF16))
            u_last[carry, :] = parts[p][-BATCH_ROWS:]
        local_in = [p.astype(_BF16) for p in parts]
        return state_in + [local_in[0]], local_in

    def ffn_chunk(n2, c):
        cs = slice(c * FF_CHUNK, (c + 1) * FF_CHUNK)
        gate = _dot(n2, wg_ref[:, cs])
        up = _dot(n2, wu_ref[:, cs])
        act[:, cs] = (gate * jax.nn.sigmoid(gate) * up).astype(_BF16)

    def step(with_mixer, with_ffn):
        rows = T_TILE * BATCH_ROWS
        n_chunks = T_TILE // PHASES
        crows = n_chunks * BATCH_ROWS
        ff = iter(range(n_ff_chunks if with_ffn else 0))

        def ffn_chunks(n):
            for _ in range(n):
                c = next(ff, None)
                if c is not None:
                    ffn_chunk(n2, c)

        def ffn_tail():
            ffn_chunks(n_ff_chunks)
            out = _rms(h + _dot(act[...], wd_ref[...]), fg_ref[...])
            obuf[oslot] = out.reshape(obuf.shape[1:])

        if with_ffn:
            h = hbuf[...]
        if not with_mixer:
            n2 = _rms(h, n2g_ref[...]).astype(_BF16)
            ffn_tail()
            return

        x = xbuf[slot].reshape(rows, xbuf.shape[-1])
        n1 = _rms(x, n1g_ref[...]).astype(_BF16)
        proj = _dot(n1, win_ref[...])
        if with_ffn:
            n2 = _rms(h, n2g_ref[...]).astype(_BF16)
        u = proj[:, :d_ssm]
        v = proj[:, d_ssm:]

        state_in, local_in = s5_inputs(u, n_chunks)
        for qq in range(n_quads // SCAN_QUADS):
            s5_states(state_in, n_chunks, qq)
            ffn_chunks(1)
        chunk_out = []
        for q in range(n_quads):
            qs = slice(q * quad_state, (q + 1) * quad_state)
            chunk_out.append(_dot(zr[0:crows, qs].astype(_BF16), v_ref[0, q])
                             + _dot(zi[0:crows, qs].astype(_BF16), v_ref[1, q])
                             + _dot(quad_lanes(local_in, q), loc_ref[q]))
            if q % 4 == 3:
                ffn_chunks(1)
        y_parts = [jnp.concatenate([c[:, p * quad_in:(p + 1) * quad_in] for c in chunk_out], axis=-1)
                   for p in range(PHASES)]
        y = _merge_phases(y_parts, n_chunks) + d_ref[...] * u
        g = jax.nn.gelu(y)
        gb = g.astype(_BF16)
        ffn_chunks(1)
        gate = jnp.concatenate([_dot(gb[:, hh * half_in:(hh + 1) * half_in], gw_ref[hh])
                                for hh in range(n_halves)], axis=-1) + gb_ref[...]
        y_ssm = _rms(g * jax.nn.sigmoid(gate), sng_ref[...])
        ffn_chunks(1)

        vext = jnp.concatenate([halo[...], v], axis=0)
        halo[...] = v[rows - halo_rows:, :]
        gd = v.shape[-1] // len(POOL_WINDOWS)
        yps = []
        for k, w in enumerate(POOL_WINDOWS):
            acc = vext[:, k * gd:(k + 1) * gd]
            span = 1
            while span < w:
                shift = span * BATCH_ROWS
                acc = acc[shift:, :] + acc[:-shift, :]
                span *= 2
            yps.append(acc[acc.shape[0] - rows:, :] * (1.0 / w) - v[:, k * gd:(k + 1) * gd])
        ffn_chunks(1)
        y_pool = _rms(jnp.concatenate(yps, axis=-1), png_ref[...])

        mixed = jnp.concatenate([y_ssm, y_pool], axis=-1).astype(_BF16)
        if with_ffn:
            ffn_tail()
        hbuf[...] = x + _dot(mixed, wout_ref[...])

    @pl.when(i == 0)
    def _():
        for c in fetch(0, 0):
            c.start()

    @pl.when(i + 1 < n_tiles)
    def _():
        for c in fetch(i + 1, oslot):
            c.start()

    @pl.when(i >= 3)
    def _():
        for c in writeback(i - 3, oslot):
            c.wait()

    @pl.when(i == 0)
    def _():
        build_weights()
        st_r[...] = jnp.zeros_like(st_r)
        st_i[...] = jnp.zeros_like(st_i)
        u_last[...] = jnp.zeros_like(u_last)
        pm = project(meta_ref[...])
        state_in, _ = s5_inputs(pm[:, :d_ssm], N_META // PHASES)
        for qq in range(n_quads // SCAN_QUADS):
            s5_states(state_in, N_META // PHASES, qq)
        halo[...] = pm[:, d_ssm:]

    @pl.when(i < n_tiles)
    def _():
        for c in fetch(i, slot):
            c.wait()

    @pl.when(i == 0)
    def _():
        step(with_mixer=True, with_ffn=False)

    @pl.when(jnp.logical_and(i > 0, i < n_tiles))
    def _():
        step(with_mixer=True, with_ffn=True)

    @pl.when(i == n_tiles)
    def _():
        step(with_mixer=False, with_ffn=True)

    @pl.when(i >= 1)
    def _():
        for c in writeback(i - 1, oslot):
            c.start()

    @pl.when(i == n_tiles)
    def _():
        @pl.when(n_tiles >= 2)
        def _():
            for c in writeback(i - 2, slot):
                c.wait()
        for c in writeback(i - 1, oslot):
            c.wait()


def _const_spec(a):
    nd = a.ndim
    return pl.BlockSpec(a.shape, lambda i: (0,) * nd, pipeline_mode=pl.Buffered(1))


def _block(x, consts, hbm_weights, n_state, d_ssm, d_pool, d_ff):
    B, S, D = x.shape
    rows = T_TILE * B
    any_spec = pl.BlockSpec(memory_space=pl.ANY)
    n_quad = d_ssm // (QUAD_GROUPS * SSM_GROUP)
    n_half = d_ssm // (HALF_GROUPS * SSM_GROUP)
    tile_k = PHASES * QUAD_GROUPS * SSM_GROUP
    quad_state = QUAD_GROUPS * SSM_STATE
    return pl.pallas_call(
        _block_kernel,
        grid=(S // T_TILE + 1,),
        in_specs=[any_spec] + [_const_spec(c) for c in consts] + [any_spec] * len(hbm_weights),
        out_specs=any_spec,
        out_shape=jax.ShapeDtypeStruct((B, S, D), _F32),
        scratch_shapes=[
            pltpu.VMEM((2, T_TILE, B, D), _F32),
            pltpu.SemaphoreType.DMA((2, BATCH_ROWS)),
            pltpu.VMEM((2, T_TILE, B, D), _F32),
            pltpu.SemaphoreType.DMA((2, BATCH_ROWS)),
            pltpu.VMEM((rows, D), _F32),
            pltpu.VMEM((rows // PHASES, n_state), _F32),
            pltpu.VMEM((rows // PHASES, n_state), _F32),
            pltpu.VMEM((B, n_state), _F32),
            pltpu.VMEM((B, n_state), _F32),
            pltpu.VMEM(((PHASES - 1) * B, d_ssm), _F32),
            pltpu.VMEM((N_META * B, d_pool), _F32),
            pltpu.VMEM((rows, d_ff), _BF16),
            pltpu.VMEM((D, D), _BF16),
            pltpu.VMEM((D, D), _BF16),
            pltpu.VMEM((D, d_ff), _BF16),
            pltpu.VMEM((D, d_ff), _BF16),
            pltpu.VMEM((d_ff, D), _BF16),
            pltpu.VMEM((2, n_quad, tile_k, quad_state), _BF16),
            pltpu.VMEM((2, n_quad, quad_state, tile_k), _BF16),
            pltpu.VMEM((n_quad, tile_k, tile_k), _BF16),
            pltpu.VMEM((n_half, HALF_GROUPS * SSM_GROUP, HALF_GROUPS * SSM_GROUP), _BF16),
            pltpu.VMEM((NARROW_STAGE_SLOTS * NARROW_STAGE_ROWS, D), _F32),
            pltpu.VMEM((WIDE_STAGE_SLOTS * WIDE_STAGE_ROWS, d_ff), _F32),
            pltpu.SemaphoreType.DMA((max(NARROW_STAGE_SLOTS, WIDE_STAGE_SLOTS),)),
        ],
        compiler_params=pltpu.CompilerParams(dimension_semantics=("arbitrary",),
                                             vmem_limit_bytes=VMEM_LIMIT),
        name="block",
    )(x, *consts, *hbm_weights)


def _lanes_by_group(blocks, per):
    G, r, c = blocks.shape
    return jnp.transpose(blocks.reshape(G // per, per, r, c), (0, 2, 1, 3)).reshape(G // per, r, per * c)


def _cmul(ar, ai, br, bi):
    return ar * br - ai * bi, ar * bi + ai * br


def _s5_params(lam_re, lam_im, log_step, b_re, b_im, c_re, c_im):
    lr = jnp.minimum(lam_re, -1e-4)
    li = lam_im
    step = jnp.exp(log_step)[:, None]
    mag = jnp.exp(lr * step)
    ang = li * step
    abr = mag * jnp.cos(ang)
    abi = mag * jnp.sin(ang)
    nr = abr - 1.0
    ni = abi
    den = lr * lr + li * li
    cr = ((nr * lr + ni * li) / den)[..., None]
    ci = ((ni * lr - nr * li) / den)[..., None]
    bbr = cr * b_re - ci * b_im
    bbi = cr * b_im + ci * b_re
    pw = [(jnp.ones_like(abr), jnp.zeros_like(abi))]
    for _ in range(PHASES):
        pw.append(_cmul(pw[-1][0], pw[-1][1], abr, abi))
    G = bbr.shape[0]
    Q = G // QUAD_GROUPS
    t = lambda a: jnp.swapaxes(a, 1, 2)

    def by_quad(blocks, lead):
        packed = _lanes_by_group(jnp.concatenate(blocks, axis=0), QUAD_GROUPS)
        return packed.reshape(lead + (Q,) + packed.shape[1:])

    ab = [_cmul(pr[..., None], pi[..., None], bbr, bbi) for pr, pi in pw[:PHASES]]
    ca = [_cmul(c_re, c_im, pr[:, None, :], pi[:, None, :]) for pr, pi in pw[:PHASES]]
    zc = by_quad([t(ab[PHASES - 1 - ph][ri]) for ri in range(2) for ph in range(PHASES)], (2, PHASES))
    sign = (1.0, -1.0)
    vc = by_quad([sign[ri] * t(ca[ph][ri]) for ri in range(2) for ph in range(PHASES)], (2, PHASES))
    loc = [jnp.einsum('ghp,gpk->ghk', ca[d][0], bbr) - jnp.einsum('ghp,gpk->ghk', ca[d][1], bbi)
           for d in range(PHASES - 1)]
    zero = jnp.zeros_like(loc[0])
    locc = by_quad([t(loc[p - j]) if 1 <= j <= p else zero for j in range(PHASES) for p in range(PHASES)],
                   (PHASES, PHASES))
    return pw[PHASES], zc, vc, locc


def kernel(x, meta_tokens, norm1_g, w_in, ssm_lambda_re, ssm_lambda_im, ssm_log_step, ssm_b_re, ssm_b_im, ssm_c_re, ssm_c_im, ssm_d, ssm_glu_w, ssm_glu_b, ssm_norm_g, pool_w, pool_scale, pool_norm_g, w_out, norm2_g, w_gate, w_up, w_down, final_norm_g):
    B, S, D = x.shape
    assert B == BATCH_ROWS and norm1_g.shape[0] == 1
    assert S % T_TILE == 0 and T_TILE % PHASES == 0 and N_META % PHASES == 0
    f32 = _F32
    (a4r, a4i), zc, vc, locc = _s5_params(
        ssm_lambda_re[0].astype(f32), ssm_lambda_im[0].astype(f32), ssm_log_step[0].astype(f32),
        ssm_b_re[0].astype(f32), ssm_b_im[0].astype(f32), ssm_c_re[0].astype(f32), ssm_c_im[0].astype(f32))
    row = lambda a: a.astype(f32).reshape(1, -1)
    consts = (
        jnp.repeat(meta_tokens.astype(f32), B, axis=0),
        row(norm1_g[0]),
        row(a4r), row(a4i),
        zc,
        vc,
        locc,
        row(ssm_d[0]),
        _lanes_by_group(ssm_glu_w[0].astype(f32), HALF_GROUPS),
        row(ssm_glu_b[0]),
        row(ssm_norm_g[0]),
        pool_w[0].astype(f32),
        row(pool_scale[0]),
        row(pool_norm_g[0]),
        row(norm2_g[0]),
        row(final_norm_g),
    )
    hbm_weights = tuple(w.astype(f32) for w in (w_in, w_out, w_gate, w_up, w_down))
    out = _block(x.astype(f32), consts, hbm_weights, n_state=a4r.size, d_ssm=ssm_d[0].size,
                 d_pool=pool_scale[0].size, d_ff=w_gate.shape[-1])
    return out.astype(x.dtype)
```

```python
import jax
import jax.numpy as jnp
from jax import lax
from jax.experimental import pallas as pl
from jax.experimental.pallas import tpu as pltpu

N_META = 16
SSM_GROUP = 16
SSM_STATE = 64
POOL_WINDOWS = (2, 4, 8, 16)
EPS = 1e-6

BATCH_ROWS = 8
HALF_GROUPS = 16
PHASES = 4
QUAD_GROUPS = 4
SCAN_QUADS = 2
T_TILE = 64
FF_CHUNK = 256
STAGE_SLOTS = 8
WIDE_STAGE_ROWS = 32
NARROW_STAGE_ROWS = 128
VMEM_LIMIT = 60 * 1024 * 1024

_BF16 = jnp.bfloat16
_F32 = jnp.float32


def _rms(x, g):
    return x * lax.rsqrt(jnp.mean(x * x, axis=-1, keepdims=True) + EPS) * g


def _dot(a, b):
    return jnp.dot(a, b, preferred_element_type=_F32)


def _split_phases(a, n_chunks):
    blk = lambda t: a[t * BATCH_ROWS:(t + 1) * BATCH_ROWS]
    return [jnp.concatenate([blk(PHASES * k + p) for k in range(n_chunks)], axis=0) for p in range(PHASES)]


def _merge_phases(parts, n_chunks):
    blk = lambda a, k: a[k * BATCH_ROWS:(k + 1) * BATCH_ROWS]
    return jnp.concatenate([blk(p, k) for k in range(n_chunks) for p in parts], axis=0)


def _tile_copies(hbm, vmem, sems, slot, tile, nt, to_vmem):
    copies = []
    for b in range(BATCH_ROWS):
        h = hbm.at[b, pl.ds(tile * nt, nt), :]
        v = vmem.at[slot, :, b, :]
        src, dst = (h, v) if to_vmem else (v, h)
        copies.append(pltpu.make_async_copy(src, dst, sems.at[slot, b]))
    return copies


def _load_cast(w_hbm, dst, stage, sems, chunk, convert=None):
    if convert is None:
        convert = lambda rows: rows.astype(dst.dtype)
    n = dst.shape[0] // chunk
    ahead = STAGE_SLOTS - 1

    def copy(c, s):
        return pltpu.make_async_copy(w_hbm.at[0, pl.ds(c * chunk, chunk), :],
                                     stage.at[pl.ds(s * chunk, chunk), :], sems.at[s])

    assert n % 2 == 0
    for c in range(min(ahead, n)):
        copy(c, c).start(priority=c % 2)

    def body(pair, carry):
        for k in range(2):
            c = 2 * pair + k
            s = lax.rem(c, STAGE_SLOTS)

            @pl.when(c + ahead < n)
            def _():
                copy(c + ahead, lax.rem(c + ahead, STAGE_SLOTS)).start(priority=(k + ahead) % 2)

            copy(c, s).wait()
            src = pl.multiple_of(s * chunk, chunk)
            row = pl.multiple_of(c * chunk, chunk)
            dst[pl.ds(row, chunk), :] = convert(stage[pl.ds(src, chunk), :])
        return carry

    lax.fori_loop(0, n // 2, body, 0)


def _expand_block_diag(src, dst, block_rows, block_lanes, n_blocks):
    lane = lax.broadcasted_iota(jnp.int32, src.shape, 1)
    lane_group = lax.rem(lane, n_blocks * block_lanes) // block_lanes
    for g in range(n_blocks):
        dst[g * block_rows:(g + 1) * block_rows, :] = jnp.where(lane_group == g, src, 0.0).astype(dst.dtype)


def _block_kernel(x_hbm, meta_ref, n1g_ref, a4r_ref, a4i_ref, zc_ref, vc_ref, locc_ref, d_ref, gluc_ref,
                  gb_ref, sng_ref, pw_ref, ps_ref, png_ref, n2g_ref, fg_ref,
                  win_hbm, wout_hbm, wg_hbm, wu_hbm, wd_hbm, o_hbm,
                  xbuf, xsem, obuf, osem, hbuf, zr, zi, st_r, st_i, u_last, halo, act,
                  win_ref, wout_ref, wg_ref, wu_ref, wd_ref, wz_ref, v_ref, loc_ref, gw_ref,
                  narrow_stage, wide_stage, wsem):
    i = pl.program_id(0)
    n_tiles = pl.num_programs(0) - 1
    slot = lax.rem(i, 2)
    oslot = 1 - slot

    def fetch(tile, s):
        return _tile_copies(x_hbm, xbuf, xsem, s, tile, T_TILE, to_vmem=True)

    def writeback(tile, s):
        return _tile_copies(o_hbm, obuf, osem, s, tile, T_TILE, to_vmem=False)

    d_ssm = d_ref.shape[-1]
    half_in = HALF_GROUPS * SSM_GROUP
    quad_in = QUAD_GROUPS * SSM_GROUP
    quad_state = QUAD_GROUPS * SSM_STATE
    n_quads = d_ssm // quad_in
    n_halves = d_ssm // half_in
    halo_rows = halo.shape[0]
    n_ff_chunks = wg_ref.shape[-1] // FF_CHUNK

    def build_weights():
        gd = pw_ref.shape[-1]
        pool_maps = [pw_ref[k] * ps_ref[:, k * gd:(k + 1) * gd] for k in range(pw_ref.shape[0])]

        def fold_pool(rows):
            cols = [rows[:, :d_ssm]]
            for k, pm in enumerate(pool_maps):
                cols.append(jnp.dot(rows[:, d_ssm + k * gd:d_ssm + (k + 1) * gd], pm,
                                    precision=lax.Precision.HIGHEST, preferred_element_type=_F32))
            return jnp.concatenate(cols, axis=-1).astype(_BF16)

        _load_cast(win_hbm, win_ref, narrow_stage, wsem, NARROW_STAGE_ROWS, convert=fold_pool)
        _load_cast(wout_hbm, wout_ref, narrow_stage, wsem, NARROW_STAGE_ROWS)
        _load_cast(wd_hbm, wd_ref, narrow_stage, wsem, NARROW_STAGE_ROWS)
        _load_cast(wg_hbm, wg_ref, wide_stage, wsem, WIDE_STAGE_ROWS)
        _load_cast(wu_hbm, wu_ref, wide_stage, wsem, WIDE_STAGE_ROWS)
        ph_rows = QUAD_GROUPS * SSM_GROUP
        for q in range(n_quads):
            for ri in range(2):
                for ph in range(PHASES):
                    _expand_block_diag(zc_ref[ri, ph, q], wz_ref.at[ri, q, ph * ph_rows:(ph + 1) * ph_rows, :],
                                       SSM_GROUP, SSM_STATE, QUAD_GROUPS)
                by_phase = jnp.concatenate([vc_ref[ri, ph, q] for ph in range(PHASES)], axis=-1)
                _expand_block_diag(by_phase, v_ref.at[ri, q], SSM_STATE, SSM_GROUP, QUAD_GROUPS)
            for j in range(PHASES):
                by_phase = jnp.concatenate([locc_ref[j, p, q] for p in range(PHASES)], axis=-1)
                _expand_block_diag(by_phase, loc_ref.at[q, j * ph_rows:(j + 1) * ph_rows, :],
                                   SSM_GROUP, SSM_GROUP, QUAD_GROUPS)
        for hh in range(n_halves):
            _expand_block_diag(gluc_ref[hh], gw_ref.at[hh], SSM_GROUP, SSM_GROUP, HALF_GROUPS)

    def project(rows):
        n1 = _rms(rows, n1g_ref[...]).astype(_BF16)
        return _dot(n1, win_ref[...])

    def quad_lanes(parts, q):
        return jnp.concatenate([p[:, q * quad_in:(q + 1) * quad_in] for p in parts], axis=1)

    def s5_states(u_in, n_chunks, qq):
        rows = n_chunks * BATCH_ROWS
        for q in range(qq * SCAN_QUADS, (qq + 1) * SCAN_QUADS):
            qs = slice(q * quad_state, (q + 1) * quad_state)
            lhs = quad_lanes(u_in, q)
            zr[0:rows, qs] = _dot(lhs, wz_ref[0, q])
            zi[0:rows, qs] = _dot(lhs, wz_ref[1, q])
        cs = slice(qq * SCAN_QUADS * quad_state, (qq + 1) * SCAN_QUADS * quad_state)
        width = SCAN_QUADS * quad_state
        ar = jnp.broadcast_to(a4r_ref[:, cs], (BATCH_ROWS, width))
        ai = jnp.broadcast_to(a4i_ref[:, cs], (BATCH_ROWS, width))
        sr = st_r[:, cs]
        si = st_i[:, cs]
        for k in range(n_chunks):
            rs = slice(k * BATCH_ROWS, (k + 1) * BATCH_ROWS)
            sr, si = (ar * sr - ai * si + zr[rs, cs], ar * si + ai * sr + zi[rs, cs])
            zr[rs, cs] = sr
            zi[rs, cs] = si
        st_r[:, cs] = sr
        st_i[:, cs] = si

    def s5_inputs(u, n_chunks):
        parts = _split_phases(u, n_chunks)
        state_in = []
        for p in range(1, PHASES):
            carry = slice((p - 1) * BATCH_ROWS, p * BATCH_ROWS)
            state_in.append(jnp.concatenate([u_last[carry, :], parts[p][:-BATCH_ROWS]], axis=0).astype(_BF16))
            u_last[carry, :] = parts[p][-BATCH_ROWS:]
        local_in = [p.astype(_BF16) for p in parts]
        return state_in + [local_in[0]], local_in

    def ffn_chunk(n2, c):
        cs = slice(c * FF_CHUNK, (c + 1) * FF_CHUNK)
        gate = _dot(n2, wg_ref[:, cs])
        up = _dot(n2, wu_ref[:, cs])
        act[:, cs] = (gate * jax.nn.sigmoid(gate) * up).astype(_BF16)

    def step(with_mixer, with_ffn):
        rows = T_TILE * BATCH_ROWS
        n_chunks = T_TILE // PHASES
        crows = n_chunks * BATCH_ROWS
        ff = iter(range(n_ff_chunks if with_ffn else 0))

        def ffn_chunks(n):
            for _ in range(n):
                c = next(ff, None)
                if c is not None:
                    ffn_chunk(n2, c)

        def ffn_tail():
            ffn_chunks(n_ff_chunks)
            out = _rms(h + _dot(act[...], wd_ref[...]), fg_ref[...])
            obuf[oslot] = out.reshape(obuf.shape[1:])

        if with_ffn:
            h = hbuf[...]
        if not with_mixer:
            n2 = _rms(h, n2g_ref[...]).astype(_BF16)
            ffn_tail()
            return

        x = xbuf[slot].reshape(rows, xbuf.shape[-1])
        n1 = _rms(x, n1g_ref[...]).astype(_BF16)
        proj = _dot(n1, win_ref[...])
        if with_ffn:
            n2 = _rms(h, n2g_ref[...]).astype(_BF16)
        u = proj[:, :d_ssm]
        v = proj[:, d_ssm:]

        state_in, local_in = s5_inputs(u, n_chunks)
        for qq in range(n_quads // SCAN_QUADS):
            s5_states(state_in, n_chunks, qq)
            ffn_chunks(1)
        chunk_out = []
        for q in range(n_quads):
            qs = slice(q * quad_state, (q + 1) * quad_state)
            chunk_out.append(_dot(zr[0:crows, qs].astype(_BF16), v_ref[0, q])
                             + _dot(zi[0:crows, qs].astype(_BF16), v_ref[1, q])
                             + _dot(quad_lanes(local_in, q), loc_ref[q]))
            if q % 4 == 3:
                ffn_chunks(1)
        y_parts = [jnp.concatenate([c[:, p * quad_in:(p + 1) * quad_in] for c in chunk_out], axis=-1)
                   for p in range(PHASES)]
        y = _merge_phases(y_parts, n_chunks) + d_ref[...] * u
        g = jax.nn.gelu(y)
        gb = g.astype(_BF16)
        ffn_chunks(1)
        gate = jnp.concatenate([_dot(gb[:, hh * half_in:(hh + 1) * half_in], gw_ref[hh])
                                for hh in range(n_halves)], axis=-1) + gb_ref[...]
        y_ssm = _rms(g * jax.nn.sigmoid(gate), sng_ref[...])
        ffn_chunks(1)

        vext = jnp.concatenate([halo[...], v], axis=0)
        halo[...] = v[rows - halo_rows:, :]
        gd = v.shape[-1] // len(POOL_WINDOWS)
        yps = []
        for k, w in enumerate(POOL_WINDOWS):
            acc = vext[:, k * gd:(k + 1) * gd]
            span = 1
            while span < w:
                shift = span * BATCH_ROWS
                acc = acc[shift:, :] + acc[:-shift, :]
                span *= 2
            yps.append(acc[acc.shape[0] - rows:, :] * (1.0 / w) - v[:, k * gd:(k + 1) * gd])
        ffn_chunks(1)
        y_pool = _rms(jnp.concatenate(yps, axis=-1), png_ref[...])

        mixed = jnp.concatenate([y_ssm, y_pool], axis=-1).astype(_BF16)
        if with_ffn:
            ffn_tail()
        hbuf[...] = x + _dot(mixed, wout_ref[...])

    @pl.when(i == 0)
    def _():
        for c in fetch(0, 0):
            c.start()

    @pl.when(i + 1 < n_tiles)
    def _():
        for c in fetch(i + 1, oslot):
            c.start()

    @pl.when(i >= 3)
    def _():
        for c in writeback(i - 3, oslot):
            c.wait()

    @pl.when(i == 0)
    def _():
        build_weights()
        st_r[...] = jnp.zeros_like(st_r)
        st_i[...] = jnp.zeros_like(st_i)
        u_last[...] = jnp.zeros_like(u_last)
        pm = project(meta_ref[...])
        state_in, _ = s5_inputs(pm[:, :d_ssm], N_META // PHASES)
        for qq in range(n_quads // SCAN_QUADS):
            s5_states(state_in, N_META // PHASES, qq)
        halo[...] = pm[:, d_ssm:]

    @pl.when(i < n_tiles)
    def _():
        for c in fetch(i, slot):
            c.wait()

    @pl.when(i == 0)
    def _():
        step(with_mixer=True, with_ffn=False)

    @pl.when(jnp.logical_and(i > 0, i < n_tiles))
    def _():
        step(with_mixer=True, with_ffn=True)

    @pl.when(i == n_tiles)
    def _():
        step(with_mixer=False, with_ffn=True)

    @pl.when(i >= 1)
    def _():
        for c in writeback(i - 1, oslot):
            c.start()

    @pl.when(i == n_tiles)
    def _():
        @pl.when(n_tiles >= 2)
        def _():
            for c in writeback(i - 2, slot):
                c.wait()
        for c in writeback(i - 1, oslot):
            c.wait()


def _const_spec(a):
    nd = a.ndim
    return pl.BlockSpec(a.shape, lambda i: (0,) * nd, pipeline_mode=pl.Buffered(1))


def _block(x, consts, hbm_weights, n_state, d_ssm, d_pool, d_ff):
    B, S, D = x.shape
    rows = T_TILE * B
    any_spec = pl.BlockSpec(memory_space=pl.ANY)
    n_quad = d_ssm // (QUAD_GROUPS * SSM_GROUP)
    n_half = d_ssm // (HALF_GROUPS * SSM_GROUP)
    tile_k = PHASES * QUAD_GROUPS * SSM_GROUP
    quad_state = QUAD_GROUPS * SSM_STATE
    return pl.pallas_call(
        _block_kernel,
        grid=(S // T_TILE + 1,),
        in_specs=[any_spec] + [_const_spec(c) for c in consts] + [any_spec] * len(hbm_weights),
        out_specs=any_spec,
        out_shape=jax.ShapeDtypeStruct((B, S, D), _F32),
        scratch_shapes=[
            pltpu.VMEM((2, T_TILE, B, D), _F32),
            pltpu.SemaphoreType.DMA((2, BATCH_ROWS)),
            pltpu.VMEM((2, T_TILE, B, D), _F32),
            pltpu.SemaphoreType.DMA((2, BATCH_ROWS)),
            pltpu.VMEM((rows, D), _F32),
            pltpu.VMEM((rows // PHASES, n_state), _F32),
            pltpu.VMEM((rows // PHASES, n_state), _F32),
            pltpu.VMEM((B, n_state), _F32),
            pltpu.VMEM((B, n_state), _F32),
            pltpu.VMEM(((PHASES - 1) * B, d_ssm), _F32),
            pltpu.VMEM((N_META * B, d_pool), _F32),
            pltpu.VMEM((rows, d_ff), _BF16),
            pltpu.VMEM((D, D), _BF16),
            pltpu.VMEM((D, D), _BF16),
            pltpu.VMEM((D, d_ff), _BF16),
            pltpu.VMEM((D, d_ff), _BF16),
            pltpu.VMEM((d_ff, D), _BF16),
            pltpu.VMEM((2, n_quad, tile_k, quad_state), _BF16),
            pltpu.VMEM((2, n_quad, quad_state, tile_k), _BF16),
            pltpu.VMEM((n_quad, tile_k, tile_k), _BF16),
            pltpu.VMEM((n_half, HALF_GROUPS * SSM_GROUP, HALF_GROUPS * SSM_GROUP), _BF16),
            pltpu.VMEM((STAGE_SLOTS * NARROW_STAGE_ROWS, D), _F32),
            pltpu.VMEM((STAGE_SLOTS * WIDE_STAGE_ROWS, d_ff), _F32),
            pltpu.SemaphoreType.DMA((STAGE_SLOTS,)),
        ],
        compiler_params=pltpu.CompilerParams(dimension_semantics=("arbitrary",),
                                             vmem_limit_bytes=VMEM_LIMIT),
        name="block",
    )(x, *consts, *hbm_weights)


def _lanes_by_group(blocks, per):
    G, r, c = blocks.shape
    return jnp.transpose(blocks.reshape(G // per, per, r, c), (0, 2, 1, 3)).reshape(G // per, r, per * c)


def _cmul(ar, ai, br, bi):
    return ar * br - ai * bi, ar * bi + ai * br


def _s5_params(lam_re, lam_im, log_step, b_re, b_im, c_re, c_im):
    lr = jnp.minimum(lam_re, -1e-4)
    li = lam_im
    step = jnp.exp(log_step)[:, None]
    mag = jnp.exp(lr * step)
    ang = li * step
    abr = mag * jnp.cos(ang)
    abi = mag * jnp.sin(ang)
    nr = abr - 1.0
    ni = abi
    den = lr * lr + li * li
    cr = ((nr * lr + ni * li) / den)[..., None]
    ci = ((ni * lr - nr * li) / den)[..., None]
    bbr = cr * b_re - ci * b_im
    bbi = cr * b_im + ci * b_re
    pw = [(jnp.ones_like(abr), jnp.zeros_like(abi))]
    for _ in range(PHASES):
        pw.append(_cmul(pw[-1][0], pw[-1][1], abr, abi))
    G = bbr.shape[0]
    Q = G // QUAD_GROUPS
    t = lambda a: jnp.swapaxes(a, 1, 2)

    def by_quad(blocks, lead):
        packed = _lanes_by_group(jnp.concatenate(blocks, axis=0), QUAD_GROUPS)
        return packed.reshape(lead + (Q,) + packed.shape[1:])

    ab = [_cmul(pr[..., None], pi[..., None], bbr, bbi) for pr, pi in pw[:PHASES]]
    ca = [_cmul(c_re, c_im, pr[:, None, :], pi[:, None, :]) for pr, pi in pw[:PHASES]]
    zc = by_quad([t(ab[PHASES - 1 - ph][ri]) for ri in range(2) for ph in range(PHASES)], (2, PHASES))
    sign = (1.0, -1.0)
    vc = by_quad([sign[ri] * t(ca[ph][ri]) for ri in range(2) for ph in range(PHASES)], (2, PHASES))
    loc = [jnp.einsum('ghp,gpk->ghk', ca[d][0], bbr) - jnp.einsum('ghp,gpk->ghk', ca[d][1], bbi)
           for d in range(PHASES - 1)]
    zero = jnp.zeros_like(loc[0])
    locc = by_quad([t(loc[p - j]) if 1 <= j <= p else zero for j in range(PHASES) for p in range(PHASES)],
                   (PHASES, PHASES))
    return pw[PHASES], zc, vc, locc


def kernel(x, meta_tokens, norm1_g, w_in, ssm_lambda_re, ssm_lambda_im, ssm_log_step, ssm_b_re, ssm_b_im, ssm_c_re, ssm_c_im, ssm_d, ssm_glu_w, ssm_glu_b, ssm_norm_g, pool_w, pool_scale, pool_norm_g, w_out, norm2_g, w_gate, w_up, w_down, final_norm_g):
    B, S, D = x.shape
    assert B == BATCH_ROWS and norm1_g.shape[0] == 1
    assert S % T_TILE == 0 and T_TILE % PHASES == 0 and N_META % PHASES == 0
    f32 = _F32
    (a4r, a4i), zc, vc, locc = _s5_params(
        ssm_lambda_re[0].astype(f32), ssm_lambda_im[0].astype(f32), ssm_log_step[0].astype(f32),
        ssm_b_re[0].astype(f32), ssm_b_im[0].astype(f32), ssm_c_re[0].astype(f32), ssm_c_im[0].astype(f32))
    row = lambda a: a.astype(f32).reshape(1, -1)
    consts = (
        jnp.repeat(meta_tokens.astype(f32), B, axis=0),
        row(norm1_g[0]),
        row(a4r), row(a4i),
        zc,
        vc,
        locc,
        row(ssm_d[0]),
        _lanes_by_group(ssm_glu_w[0].astype(f32), HALF_GROUPS),
        row(ssm_glu_b[0]),
        row(ssm_norm_g[0]),
        pool_w[0].astype(f32),
        row(pool_scale[0]),
        row(pool_norm_g[0]),
        row(norm2_g[0]),
        row(final_norm_g),
    )
    hbm_weights = tuple(w.astype(f32) for w in (w_in, w_out, w_gate, w_up, w_down))
    out = _block(x.astype(f32), consts, hbm_weights, n_state=a4r.size, d_ssm=ssm_d[0].size,
                 d_pool=pool_scale[0].size, d_ff=w_gate.shape[-1])
    return out.astype(x.dtype)
```

```python
import jax
import jax.numpy as jnp
from jax import lax
from jax.experimental import pallas as pl
from jax.experimental.pallas import tpu as pltpu

N_META = 16
SSM_GROUP = 16
SSM_STATE = 64
POOL_WINDOWS = (2, 4, 8, 16)
EPS = 1e-6

BATCH_ROWS = 8
HALF_GROUPS = 16
PHASES = 4
QUAD_GROUPS = 4
SCAN_QUADS = 2
T_TILE = 64
FF_CHUNK = 256
FFN_HEAD_CHUNKS = 2
STAGE_SLOTS = 8
WIDE_STAGE_ROWS = 32
NARROW_STAGE_ROWS = 128
VMEM_LIMIT = 60 * 1024 * 1024

_BF16 = jnp.bfloat16
_F32 = jnp.float32


def _rms(x, g):
    return x * lax.rsqrt(jnp.mean(x * x, axis=-1, keepdims=True) + EPS) * g


def _dot(a, b):
    return jnp.dot(a, b, preferred_element_type=_F32)


def _split_phases(a, n_chunks):
    blk = lambda t: a[t * BATCH_ROWS:(t + 1) * BATCH_ROWS]
    return [jnp.concatenate([blk(PHASES * k + p) for k in range(n_chunks)], axis=0) for p in range(PHASES)]


def _merge_phases(parts, n_chunks):
    blk = lambda a, k: a[k * BATCH_ROWS:(k + 1) * BATCH_ROWS]
    return jnp.concatenate([blk(p, k) for k in range(n_chunks) for p in parts], axis=0)


def _tile_copies(hbm, vmem, sems, slot, tile, nt, to_vmem):
    copies = []
    for b in range(BATCH_ROWS):
        h = hbm.at[b, pl.ds(tile * nt, nt), :]
        v = vmem.at[slot, :, b, :]
        src, dst = (h, v) if to_vmem else (v, h)
        copies.append(pltpu.make_async_copy(src, dst, sems.at[slot, b]))
    return copies


def _load_cast(w_hbm, dst, stage, sems, chunk, convert=None):
    if convert is None:
        convert = lambda rows: rows.astype(dst.dtype)
    n = dst.shape[0] // chunk
    ahead = STAGE_SLOTS - 1

    def copy(c, s):
        return pltpu.make_async_copy(w_hbm.at[0, pl.ds(c * chunk, chunk), :],
                                     stage.at[pl.ds(s * chunk, chunk), :], sems.at[s])

    for c in range(min(ahead, n)):
        copy(c, c).start()

    def body(c, carry):
        s = lax.rem(c, STAGE_SLOTS)

        @pl.when(c + ahead < n)
        def _():
            copy(c + ahead, lax.rem(c + ahead, STAGE_SLOTS)).start()

        copy(c, s).wait()
        src = pl.multiple_of(s * chunk, chunk)
        row = pl.multiple_of(c * chunk, chunk)
        dst[pl.ds(row, chunk), :] = convert(stage[pl.ds(src, chunk), :])
        return carry

    lax.fori_loop(0, n, body, 0)


def _expand_block_diag(src, dst, block_rows, block_lanes, n_blocks):
    lane = lax.broadcasted_iota(jnp.int32, src.shape, 1)
    lane_group = lax.rem(lane, n_blocks * block_lanes) // block_lanes
    for g in range(n_blocks):
        dst[g * block_rows:(g + 1) * block_rows, :] = jnp.where(lane_group == g, src, 0.0).astype(dst.dtype)


def _block_kernel(x_hbm, meta_ref, n1g_ref, a4r_ref, a4i_ref, zc_ref, vc_ref, locc_ref, d_ref, gluc_ref,
                  gb_ref, sng_ref, pw_ref, ps_ref, png_ref, n2g_ref, fg_ref,
                  win_hbm, wout_hbm, wg_hbm, wu_hbm, wd_hbm, o_hbm,
                  xbuf, xsem, obuf, osem, hbuf, n2buf, zr, zi, st_r, st_i, u_last, halo, act,
                  win_ref, wout_ref, wg_ref, wu_ref, wd_ref, wz_ref, v_ref, loc_ref, gw_ref,
                  narrow_stage, wide_stage, wsem):
    i = pl.program_id(0)
    n_tiles = pl.num_programs(0) - 1
    slot = lax.rem(i, 2)
    oslot = 1 - slot

    def fetch(tile, s):
        return _tile_copies(x_hbm, xbuf, xsem, s, tile, T_TILE, to_vmem=True)

    def writeback(tile, s):
        return _tile_copies(o_hbm, obuf, osem, s, tile, T_TILE, to_vmem=False)

    d_ssm = d_ref.shape[-1]
    half_in = HALF_GROUPS * SSM_GROUP
    quad_in = QUAD_GROUPS * SSM_GROUP
    quad_state = QUAD_GROUPS * SSM_STATE
    n_quads = d_ssm // quad_in
    n_halves = d_ssm // half_in
    halo_rows = halo.shape[0]
    n_ff_chunks = wg_ref.shape[-1] // FF_CHUNK

    def build_weights():
        gd = pw_ref.shape[-1]
        pool_maps = [pw_ref[k] * ps_ref[:, k * gd:(k + 1) * gd] for k in range(pw_ref.shape[0])]

        def fold_pool(rows):
            cols = [rows[:, :d_ssm]]
            for k, pm in enumerate(pool_maps):
                cols.append(jnp.dot(rows[:, d_ssm + k * gd:d_ssm + (k + 1) * gd], pm,
                                    precision=lax.Precision.HIGHEST, preferred_element_type=_F32))
            return jnp.concatenate(cols, axis=-1).astype(_BF16)

        _load_cast(win_hbm, win_ref, narrow_stage, wsem, NARROW_STAGE_ROWS, convert=fold_pool)
        _load_cast(wout_hbm, wout_ref, narrow_stage, wsem, NARROW_STAGE_ROWS)
        _load_cast(wd_hbm, wd_ref, narrow_stage, wsem, NARROW_STAGE_ROWS)
        _load_cast(wg_hbm, wg_ref, wide_stage, wsem, WIDE_STAGE_ROWS)
        _load_cast(wu_hbm, wu_ref, wide_stage, wsem, WIDE_STAGE_ROWS)
        ph_rows = QUAD_GROUPS * SSM_GROUP
        for q in range(n_quads):
            for ri in range(2):
                for ph in range(PHASES):
                    _expand_block_diag(zc_ref[ri, ph, q], wz_ref.at[ri, q, ph * ph_rows:(ph + 1) * ph_rows, :],
                                       SSM_GROUP, SSM_STATE, QUAD_GROUPS)
                by_phase = jnp.concatenate([vc_ref[ri, ph, q] for ph in range(PHASES)], axis=-1)
                _expand_block_diag(by_phase, v_ref.at[ri, q], SSM_STATE, SSM_GROUP, QUAD_GROUPS)
            for j in range(PHASES):
                by_phase = jnp.concatenate([locc_ref[j, p, q] for p in range(PHASES)], axis=-1)
                _expand_block_diag(by_phase, loc_ref.at[q, j * ph_rows:(j + 1) * ph_rows, :],
                                   SSM_GROUP, SSM_GROUP, QUAD_GROUPS)
        for hh in range(n_halves):
            _expand_block_diag(gluc_ref[hh], gw_ref.at[hh], SSM_GROUP, SSM_GROUP, HALF_GROUPS)

    def project(rows):
        n1 = _rms(rows, n1g_ref[...]).astype(_BF16)
        return _dot(n1, win_ref[...])

    def quad_lanes(parts, q):
        return jnp.concatenate([p[:, q * quad_in:(q + 1) * quad_in] for p in parts], axis=1)

    def s5_states(u_in, n_chunks, qq):
        rows = n_chunks * BATCH_ROWS
        for q in range(qq * SCAN_QUADS, (qq + 1) * SCAN_QUADS):
            qs = slice(q * quad_state, (q + 1) * quad_state)
            lhs = quad_lanes(u_in, q)
            zr[0:rows, qs] = _dot(lhs, wz_ref[0, q])
            zi[0:rows, qs] = _dot(lhs, wz_ref[1, q])
        cs = slice(qq * SCAN_QUADS * quad_state, (qq + 1) * SCAN_QUADS * quad_state)
        width = SCAN_QUADS * quad_state
        ar = jnp.broadcast_to(a4r_ref[:, cs], (BATCH_ROWS, width))
        ai = jnp.broadcast_to(a4i_ref[:, cs], (BATCH_ROWS, width))
        sr = st_r[:, cs]
        si = st_i[:, cs]
        for k in range(n_chunks):
            rs = slice(k * BATCH_ROWS, (k + 1) * BATCH_ROWS)
            sr, si = (ar * sr - ai * si + zr[rs, cs], ar * si + ai * sr + zi[rs, cs])
            zr[rs, cs] = sr
            zi[rs, cs] = si
        st_r[:, cs] = sr
        st_i[:, cs] = si

    def s5_inputs(u, n_chunks):
        parts = _split_phases(u, n_chunks)
        state_in = []
        for p in range(1, PHASES):
            carry = slice((p - 1) * BATCH_ROWS, p * BATCH_ROWS)
            state_in.append(jnp.concatenate([u_last[carry, :], parts[p][:-BATCH_ROWS]], axis=0).astype(_BF16))
            u_last[carry, :] = parts[p][-BATCH_ROWS:]
        local_in = [p.astype(_BF16) for p in parts]
        return state_in + [local_in[0]], local_in

    def ffn_chunk(n2, c):
        cs = slice(c * FF_CHUNK, (c + 1) * FF_CHUNK)
        gate = _dot(n2, wg_ref[:, cs])
        up = _dot(n2, wu_ref[:, cs])
        act[:, cs] = (gate * jax.nn.sigmoid(gate) * up).astype(_BF16)

    def step(with_prev, with_mixer):
        rows = T_TILE * BATCH_ROWS
        n_chunks = T_TILE // PHASES
        crows = n_chunks * BATCH_ROWS
        ff = iter(range(FFN_HEAD_CHUNKS, n_ff_chunks) if with_prev else ())

        def ffn_chunks(n):
            for _ in range(n):
                c = next(ff, None)
                if c is not None:
                    ffn_chunk(n2_prev, c)

        def finish_prev():
            ffn_chunks(n_ff_chunks)
            h2 = obuf[oslot].reshape(rows, obuf.shape[-1]) + _dot(act[...], wd_ref[...])
            obuf[oslot] = _rms(h2, fg_ref[...]).reshape(obuf.shape[1:])

        if with_prev:
            n2_prev = n2buf[...]
            obuf[oslot] = hbuf[...].reshape(obuf.shape[1:])
            ffn_chunks(1)
        if not with_mixer:
            finish_prev()
            return

        x = xbuf[slot].reshape(rows, xbuf.shape[-1])
        n1 = _rms(x, n1g_ref[...]).astype(_BF16)
        proj = _dot(n1, win_ref[...])
        u = proj[:, :d_ssm]
        v = proj[:, d_ssm:]

        state_in, local_in = s5_inputs(u, n_chunks)
        for qq in range(n_quads // SCAN_QUADS):
            s5_states(state_in, n_chunks, qq)
            ffn_chunks(1)
        chunk_out = []
        for q in range(n_quads):
            qs = slice(q * quad_state, (q + 1) * quad_state)
            chunk_out.append(_dot(zr[0:crows, qs].astype(_BF16), v_ref[0, q])
                             + _dot(zi[0:crows, qs].astype(_BF16), v_ref[1, q])
                             + _dot(quad_lanes(local_in, q), loc_ref[q]))
            if q % 4 == 3:
                ffn_chunks(1)
        y_parts = [jnp.concatenate([c[:, p * quad_in:(p + 1) * quad_in] for c in chunk_out], axis=-1)
                   for p in range(PHASES)]
        y = _merge_phases(y_parts, n_chunks) + d_ref[...] * u
        g = jax.nn.gelu(y)
        gb = g.astype(_BF16)
        ffn_chunks(1)
        gate = jnp.concatenate([_dot(gb[:, hh * half_in:(hh + 1) * half_in], gw_ref[hh])
                                for hh in range(n_halves)], axis=-1) + gb_ref[...]
        y_ssm = _rms(g * jax.nn.sigmoid(gate), sng_ref[...])
        ffn_chunks(1)

        vext = jnp.concatenate([halo[...], v], axis=0)
        halo[...] = v[rows - halo_rows:, :]
        gd = v.shape[-1] // len(POOL_WINDOWS)
        yps = []
        for k, w in enumerate(POOL_WINDOWS):
            acc = vext[:, k * gd:(k + 1) * gd]
            span = 1
            while span < w:
                shift = span * BATCH_ROWS
                acc = acc[shift:, :] + acc[:-shift, :]
                span *= 2
            yps.append(acc[acc.shape[0] - rows:, :] * (1.0 / w) - v[:, k * gd:(k + 1) * gd])
        y_pool = _rms(jnp.concatenate(yps, axis=-1), png_ref[...])

        mixed = jnp.concatenate([y_ssm, y_pool], axis=-1).astype(_BF16)
        h_next = x + _dot(mixed, wout_ref[...])
        hbuf[...] = h_next
        n2 = _rms(h_next, n2g_ref[...]).astype(_BF16)
        n2buf[...] = n2
        if with_prev:
            finish_prev()
        for c in range(FFN_HEAD_CHUNKS):
            ffn_chunk(n2, c)

    @pl.when(i == 0)
    def _():
        for c in fetch(0, 0):
            c.start()

    @pl.when(i + 1 < n_tiles)
    def _():
        for c in fetch(i + 1, oslot):
            c.start()

    @pl.when(i >= 3)
    def _():
        for c in writeback(i - 3, oslot):
            c.wait()

    @pl.when(i == 0)
    def _():
        build_weights()
        st_r[...] = jnp.zeros_like(st_r)
        st_i[...] = jnp.zeros_like(st_i)
        u_last[...] = jnp.zeros_like(u_last)
        pm = project(meta_ref[...])
        state_in, _ = s5_inputs(pm[:, :d_ssm], N_META // PHASES)
        for qq in range(n_quads // SCAN_QUADS):
            s5_states(state_in, N_META // PHASES, qq)
        halo[...] = pm[:, d_ssm:]

    @pl.when(i < n_tiles)
    def _():
        for c in fetch(i, slot):
            c.wait()

    @pl.when(i == 0)
    def _():
        step(with_prev=False, with_mixer=True)

    @pl.when(jnp.logical_and(i > 0, i < n_tiles))
    def _():
        step(with_prev=True, with_mixer=True)

    @pl.when(i == n_tiles)
    def _():
        step(with_prev=True, with_mixer=False)

    @pl.when(i >= 1)
    def _():
        for c in writeback(i - 1, oslot):
            c.start()

    @pl.when(i == n_tiles)
    def _():
        @pl.when(n_tiles >= 2)
        def _():
            for c in writeback(i - 2, slot):
                c.wait()
        for c in writeback(i - 1, oslot):
            c.wait()


def _const_spec(a):
    nd = a.ndim
    return pl.BlockSpec(a.shape, lambda i: (0,) * nd, pipeline_mode=pl.Buffered(1))


def _block(x, consts, hbm_weights, n_state, d_ssm, d_pool, d_ff):
    B, S, D = x.shape
    rows = T_TILE * B
    any_spec = pl.BlockSpec(memory_space=pl.ANY)
    n_quad = d_ssm // (QUAD_GROUPS * SSM_GROUP)
    n_half = d_ssm // (HALF_GROUPS * SSM_GROUP)
    tile_k = PHASES * QUAD_GROUPS * SSM_GROUP
    quad_state = QUAD_GROUPS * SSM_STATE
    return pl.pallas_call(
        _block_kernel,
        grid=(S // T_TILE + 1,),
        in_specs=[any_spec] + [_const_spec(c) for c in consts] + [any_spec] * len(hbm_weights),
        out_specs=any_spec,
        out_shape=jax.ShapeDtypeStruct((B, S, D), _F32),
        scratch_shapes=[
            pltpu.VMEM((2, T_TILE, B, D), _F32),
            pltpu.SemaphoreType.DMA((2, BATCH_ROWS)),
            pltpu.VMEM((2, T_TILE, B, D), _F32),
            pltpu.SemaphoreType.DMA((2, BATCH_ROWS)),
            pltpu.VMEM((rows, D), _F32),
            pltpu.VMEM((rows, D), _BF16),
            pltpu.VMEM((rows // PHASES, n_state), _F32),
            pltpu.VMEM((rows // PHASES, n_state), _F32),
            pltpu.VMEM((B, n_state), _F32),
            pltpu.VMEM((B, n_state), _F32),
            pltpu.VMEM(((PHASES - 1) * B, d_ssm), _F32),
            pltpu.VMEM((N_META * B, d_pool), _F32),
            pltpu.VMEM((rows, d_ff), _BF16),
            pltpu.VMEM((D, D), _BF16),
            pltpu.VMEM((D, D), _BF16),
            pltpu.VMEM((D, d_ff), _BF16),
            pltpu.VMEM((D, d_ff), _BF16),
            pltpu.VMEM((d_ff, D), _BF16),
            pltpu.VMEM((2, n_quad, tile_k, quad_state), _BF16),
            pltpu.VMEM((2, n_quad, quad_state, tile_k), _BF16),
            pltpu.VMEM((n_quad, tile_k, tile_k), _BF16),
            pltpu.VMEM((n_half, HALF_GROUPS * SSM_GROUP, HALF_GROUPS * SSM_GROUP), _BF16),
            pltpu.VMEM((STAGE_SLOTS * NARROW_STAGE_ROWS, D), _F32),
            pltpu.VMEM((STAGE_SLOTS * WIDE_STAGE_ROWS, d_ff), _F32),
            pltpu.SemaphoreType.DMA((STAGE_SLOTS,)),
        ],
        compiler_params=pltpu.CompilerParams(dimension_semantics=("arbitrary",),
                                             vmem_limit_bytes=VMEM_LIMIT),
        name="block",
    )(x, *consts, *hbm_weights)


def _lanes_by_group(blocks, per):
    G, r, c = blocks.shape
    return jnp.transpose(blocks.reshape(G // per, per, r, c), (0, 2, 1, 3)).reshape(G // per, r, per * c)


def _cmul(ar, ai, br, bi):
    return ar * br - ai * bi, ar * bi + ai * br


def _s5_params(lam_re, lam_im, log_step, b_re, b_im, c_re, c_im):
    lr = jnp.minimum(lam_re, -1e-4)
    li = lam_im
    step = jnp.exp(log_step)[:, None]
    mag = jnp.exp(lr * step)
    ang = li * step
    abr = mag * jnp.cos(ang)
    abi = mag * jnp.sin(ang)
    nr = abr - 1.0
    ni = abi
    den = lr * lr + li * li
    cr = ((nr * lr + ni * li) / den)[..., None]
    ci = ((ni * lr - nr * li) / den)[..., None]
    bbr = cr * b_re - ci * b_im
    bbi = cr * b_im + ci * b_re
    pw = [(jnp.ones_like(abr), jnp.zeros_like(abi))]
    for _ in range(PHASES):
        pw.append(_cmul(pw[-1][0], pw[-1][1], abr, abi))
    G = bbr.shape[0]
    Q = G // QUAD_GROUPS
    t = lambda a: jnp.swapaxes(a, 1, 2)

    def by_quad(blocks, lead):
        packed = _lanes_by_group(jnp.concatenate(blocks, axis=0), QUAD_GROUPS)
        return packed.reshape(lead + (Q,) + packed.shape[1:])

    ab = [_cmul(pr[..., None], pi[..., None], bbr, bbi) for pr, pi in pw[:PHASES]]
    ca = [_cmul(c_re, c_im, pr[:, None, :], pi[:, None, :]) for pr, pi in pw[:PHASES]]
    zc = by_quad([t(ab[PHASES - 1 - ph][ri]) for ri in range(2) for ph in range(PHASES)], (2, PHASES))
    sign = (1.0, -1.0)
    vc = by_quad([sign[ri] * t(ca[ph][ri]) for ri in range(2) for ph in range(PHASES)], (2, PHASES))
    loc = [jnp.einsum('ghp,gpk->ghk', ca[d][0], bbr) - jnp.einsum('ghp,gpk->ghk', ca[d][1], bbi)
           for d in range(PHASES - 1)]
    zero = jnp.zeros_like(loc[0])
    locc = by_quad([t(loc[p - j]) if 1 <= j <= p else zero for j in range(PHASES) for p in range(PHASES)],
                   (PHASES, PHASES))
    return pw[PHASES], zc, vc, locc


def kernel(x, meta_tokens, norm1_g, w_in, ssm_lambda_re, ssm_lambda_im, ssm_log_step, ssm_b_re, ssm_b_im, ssm_c_re, ssm_c_im, ssm_d, ssm_glu_w, ssm_glu_b, ssm_norm_g, pool_w, pool_scale, pool_norm_g, w_out, norm2_g, w_gate, w_up, w_down, final_norm_g):
    B, S, D = x.shape
    assert B == BATCH_ROWS and norm1_g.shape[0] == 1
    assert S % T_TILE == 0 and T_TILE % PHASES == 0 and N_META % PHASES == 0
    f32 = _F32
    (a4r, a4i), zc, vc, locc = _s5_params(
        ssm_lambda_re[0].astype(f32), ssm_lambda_im[0].astype(f32), ssm_log_step[0].astype(f32),
        ssm_b_re[0].astype(f32), ssm_b_im[0].astype(f32), ssm_c_re[0].astype(f32), ssm_c_im[0].astype(f32))
    row = lambda a: a.astype(f32).reshape(1, -1)
    consts = (
        jnp.repeat(meta_tokens.astype(f32), B, axis=0),
        row(norm1_g[0]),
        row(a4r), row(a4i),
        zc,
        vc,
        locc,
        row(ssm_d[0]),
        _lanes_by_group(ssm_glu_w[0].astype(f32), HALF_GROUPS),
        row(ssm_glu_b[0]),
        row(ssm_norm_g[0]),
        pool_w[0].astype(f32),
        row(pool_scale[0]),
        row(pool_norm_g[0]),
        row(norm2_g[0]),
        row(final_norm_g),
    )
    hbm_weights = tuple(w.astype(f32) for w in (w_in, w_out, w_gate, w_up, w_down))
    out = _block(x.astype(f32), consts, hbm_weights, n_state=a4r.size, d_ssm=ssm_d[0].size,
                 d_pool=pool_scale[0].size, d_ff=w_gate.shape[-1])
    return out.astype(x.dtype)
```

```python
import jax
import jax.numpy as jnp
from jax import lax
from jax.experimental import pallas as pl
from jax.experimental.pallas import tpu as pltpu

N_META = 16
SSM_GROUP = 16
SSM_STATE = 64
POOL_WINDOWS = (2, 4, 8, 16)
EPS = 1e-6

BATCH_ROWS = 8
HALF_GROUPS = 16
PHASES = 4
QUAD_GROUPS = 4
SCAN_QUADS = 2
T_TILE = 64
FF_CHUNK = 256
FFN_HEAD_CHUNKS = 1
STAGE_SLOTS = 8
WIDE_STAGE_ROWS = 32
NARROW_STAGE_ROWS = 128
VMEM_LIMIT = 60 * 1024 * 1024

_BF16 = jnp.bfloat16
_F32 = jnp.float32


def _rms(x, g):
    return x * lax.rsqrt(jnp.mean(x * x, axis=-1, keepdims=True) + EPS) * g


def _dot(a, b):
    return jnp.dot(a, b, preferred_element_type=_F32)


def _split_phases(a, n_chunks):
    blk = lambda t: a[t * BATCH_ROWS:(t + 1) * BATCH_ROWS]
    return [jnp.concatenate([blk(PHASES * k + p) for k in range(n_chunks)], axis=0) for p in range(PHASES)]


def _merge_phases(parts, n_chunks):
    blk = lambda a, k: a[k * BATCH_ROWS:(k + 1) * BATCH_ROWS]
    return jnp.concatenate([blk(p, k) for k in range(n_chunks) for p in parts], axis=0)


def _tile_copies(hbm, vmem, sems, slot, tile, nt, to_vmem):
    copies = []
    for b in range(BATCH_ROWS):
        h = hbm.at[b, pl.ds(tile * nt, nt), :]
        v = vmem.at[slot, :, b, :]
        src, dst = (h, v) if to_vmem else (v, h)
        copies.append(pltpu.make_async_copy(src, dst, sems.at[slot, b]))
    return copies


def _load_cast(w_hbm, dst, stage, sems, chunk, convert=None):
    if convert is None:
        convert = lambda rows: rows.astype(dst.dtype)
    n = dst.shape[0] // chunk
    ahead = STAGE_SLOTS - 1

    def copy(c, s):
        return pltpu.make_async_copy(w_hbm.at[0, pl.ds(c * chunk, chunk), :],
                                     stage.at[pl.ds(s * chunk, chunk), :], sems.at[s])

    for c in range(min(ahead, n)):
        copy(c, c).start()

    def body(c, carry):
        s = lax.rem(c, STAGE_SLOTS)

        @pl.when(c + ahead < n)
        def _():
            copy(c + ahead, lax.rem(c + ahead, STAGE_SLOTS)).start()

        copy(c, s).wait()
        src = pl.multiple_of(s * chunk, chunk)
        row = pl.multiple_of(c * chunk, chunk)
        dst[pl.ds(row, chunk), :] = convert(stage[pl.ds(src, chunk), :])
        return carry

    lax.fori_loop(0, n, body, 0)


def _expand_block_diag(src, dst, block_rows, block_lanes, n_blocks):
    lane = lax.broadcasted_iota(jnp.int32, src.shape, 1)
    lane_group = lax.rem(lane, n_blocks * block_lanes) // block_lanes
    for g in range(n_blocks):
        dst[g * block_rows:(g + 1) * block_rows, :] = jnp.where(lane_group == g, src, 0.0).astype(dst.dtype)


def _block_kernel(x_hbm, meta_ref, n1g_ref, a4r_ref, a4i_ref, zc_ref, vc_ref, locc_ref, d_ref, gluc_ref,
                  gb_ref, sng_ref, pw_ref, ps_ref, png_ref, n2g_ref, fg_ref,
                  win_hbm, wout_hbm, wg_hbm, wu_hbm, wd_hbm, o_hbm,
                  xbuf, xsem, obuf, osem, hbuf, n2buf, zr, zi, st_r, st_i, u_last, halo, act,
                  win_ref, wout_ref, wg_ref, wu_ref, wd_ref, wz_ref, v_ref, loc_ref, gw_ref,
                  narrow_stage, wide_stage, wsem):
    i = pl.program_id(0)
    n_tiles = pl.num_programs(0) - 1
    slot = lax.rem(i, 2)
    oslot = 1 - slot

    def fetch(tile, s):
        return _tile_copies(x_hbm, xbuf, xsem, s, tile, T_TILE, to_vmem=True)

    def writeback(tile, s):
        return _tile_copies(o_hbm, obuf, osem, s, tile, T_TILE, to_vmem=False)

    d_ssm = d_ref.shape[-1]
    half_in = HALF_GROUPS * SSM_GROUP
    quad_in = QUAD_GROUPS * SSM_GROUP
    quad_state = QUAD_GROUPS * SSM_STATE
    n_quads = d_ssm // quad_in
    n_halves = d_ssm // half_in
    halo_rows = halo.shape[0]
    n_ff_chunks = wg_ref.shape[-1] // FF_CHUNK

    def build_weights():
        gd = pw_ref.shape[-1]
        pool_maps = [pw_ref[k] * ps_ref[:, k * gd:(k + 1) * gd] for k in range(pw_ref.shape[0])]

        def fold_pool(rows):
            cols = [rows[:, :d_ssm]]
            for k, pm in enumerate(pool_maps):
                cols.append(jnp.dot(rows[:, d_ssm + k * gd:d_ssm + (k + 1) * gd], pm,
                                    precision=lax.Precision.HIGHEST, preferred_element_type=_F32))
            return jnp.concatenate(cols, axis=-1).astype(_BF16)

        _load_cast(win_hbm, win_ref, narrow_stage, wsem, NARROW_STAGE_ROWS, convert=fold_pool)
        _load_cast(wout_hbm, wout_ref, narrow_stage, wsem, NARROW_STAGE_ROWS)
        _load_cast(wd_hbm, wd_ref, narrow_stage, wsem, NARROW_STAGE_ROWS)
        _load_cast(wg_hbm, wg_ref, wide_stage, wsem, WIDE_STAGE_ROWS)
        _load_cast(wu_hbm, wu_ref, wide_stage, wsem, WIDE_STAGE_ROWS)
        ph_rows = QUAD_GROUPS * SSM_GROUP
        for q in range(n_quads):
            for ri in range(2):
                for ph in range(PHASES):
                    _expand_block_diag(zc_ref[ri, ph, q], wz_ref.at[ri, q, ph * ph_rows:(ph + 1) * ph_rows, :],
                                       SSM_GROUP, SSM_STATE, QUAD_GROUPS)
                by_phase = jnp.concatenate([vc_ref[ri, ph, q] for ph in range(PHASES)], axis=-1)
                _expand_block_diag(by_phase, v_ref.at[ri, q], SSM_STATE, SSM_GROUP, QUAD_GROUPS)
            for j in range(PHASES):
                by_phase = jnp.concatenate([locc_ref[j, p, q] for p in range(PHASES)], axis=-1)
                _expand_block_diag(by_phase, loc_ref.at[q, j * ph_rows:(j + 1) * ph_rows, :],
                                   SSM_GROUP, SSM_GROUP, QUAD_GROUPS)
        for hh in range(n_halves):
            _expand_block_diag(gluc_ref[hh], gw_ref.at[hh], SSM_GROUP, SSM_GROUP, HALF_GROUPS)

    def project(rows):
        n1 = _rms(rows, n1g_ref[...]).astype(_BF16)
        return _dot(n1, win_ref[...])

    def quad_lanes(parts, q):
        return jnp.concatenate([p[:, q * quad_in:(q + 1) * quad_in] for p in parts], axis=1)

    def s5_states(u_in, n_chunks, qq):
        rows = n_chunks * BATCH_ROWS
        for q in range(qq * SCAN_QUADS, (qq + 1) * SCAN_QUADS):
            qs = slice(q * quad_state, (q + 1) * quad_state)
            lhs = quad_lanes(u_in, q)
            zr[0:rows, qs] = _dot(lhs, wz_ref[0, q])
            zi[0:rows, qs] = _dot(lhs, wz_ref[1, q])
        cs = slice(qq * SCAN_QUADS * quad_state, (qq + 1) * SCAN_QUADS * quad_state)
        width = SCAN_QUADS * quad_state
        ar = jnp.broadcast_to(a4r_ref[:, cs], (BATCH_ROWS, width))
        ai = jnp.broadcast_to(a4i_ref[:, cs], (BATCH_ROWS, width))
        sr = st_r[:, cs]
        si = st_i[:, cs]
        for k in range(n_chunks):
            rs = slice(k * BATCH_ROWS, (k + 1) * BATCH_ROWS)
            sr, si = (ar * sr - ai * si + zr[rs, cs], ar * si + ai * sr + zi[rs, cs])
            zr[rs, cs] = sr
            zi[rs, cs] = si
        st_r[:, cs] = sr
        st_i[:, cs] = si

    def s5_inputs(u, n_chunks):
        parts = _split_phases(u, n_chunks)
        state_in = []
        for p in range(1, PHASES):
            carry = slice((p - 1) * BATCH_ROWS, p * BATCH_ROWS)
            state_in.append(jnp.concatenate([u_last[carry, :], parts[p][:-BATCH_ROWS]], axis=0).astype(_BF16))
            u_last[carry, :] = parts[p][-BATCH_ROWS:]
        local_in = [p.astype(_BF16) for p in parts]
        return state_in + [local_in[0]], local_in

    def ffn_chunk(n2, c):
        cs = slice(c * FF_CHUNK, (c + 1) * FF_CHUNK)
        gate = _dot(n2, wg_ref[:, cs])
        up = _dot(n2, wu_ref[:, cs])
        act[:, cs] = (gate * jax.nn.sigmoid(gate) * up).astype(_BF16)

    def step(with_prev, with_mixer):
        rows = T_TILE * BATCH_ROWS
        n_chunks = T_TILE // PHASES
        crows = n_chunks * BATCH_ROWS
        ff = iter(range(FFN_HEAD_CHUNKS, n_ff_chunks) if with_prev else ())

        def ffn_chunks(n):
            for _ in range(n):
                c = next(ff, None)
                if c is not None:
                    ffn_chunk(n2_prev, c)

        def finish_prev():
            ffn_chunks(n_ff_chunks)
            h2 = obuf[oslot].reshape(rows, obuf.shape[-1]) + _dot(act[...], wd_ref[...])
            obuf[oslot] = _rms(h2, fg_ref[...]).reshape(obuf.shape[1:])

        if with_prev:
            n2_prev = n2buf[...]
            obuf[oslot] = hbuf[...].reshape(obuf.shape[1:])
            ffn_chunks(1)
        if not with_mixer:
            finish_prev()
            return

        x = xbuf[slot].reshape(rows, xbuf.shape[-1])
        n1 = _rms(x, n1g_ref[...]).astype(_BF16)
        proj = _dot(n1, win_ref[...])
        u = proj[:, :d_ssm]
        v = proj[:, d_ssm:]

        state_in, local_in = s5_inputs(u, n_chunks)
        for qq in range(n_quads // SCAN_QUADS):
            s5_states(state_in, n_chunks, qq)
            ffn_chunks(1)
        chunk_out = []
        for q in range(n_quads):
            qs = slice(q * quad_state, (q + 1) * quad_state)
            chunk_out.append(_dot(zr[0:crows, qs].astype(_BF16), v_ref[0, q])
                             + _dot(zi[0:crows, qs].astype(_BF16), v_ref[1, q])
                             + _dot(quad_lanes(local_in, q), loc_ref[q]))
            if q % 4 == 3:
                ffn_chunks(1)
        y_parts = [jnp.concatenate([c[:, p * quad_in:(p + 1) * quad_in] for c in chunk_out], axis=-1)
                   for p in range(PHASES)]
        y = _merge_phases(y_parts, n_chunks) + d_ref[...] * u
        g = jax.nn.gelu(y)
        gb = g.astype(_BF16)
        ffn_chunks(1)
        gate = jnp.concatenate([_dot(gb[:, hh * half_in:(hh + 1) * half_in], gw_ref[hh])
                                for hh in range(n_halves)], axis=-1) + gb_ref[...]
        y_ssm = _rms(g * jax.nn.sigmoid(gate), sng_ref[...])
        ffn_chunks(1)

        vext = jnp.concatenate([halo[...], v], axis=0)
        halo[...] = v[rows - halo_rows:, :]
        gd = v.shape[-1] // len(POOL_WINDOWS)
        yps = []
        for k, w in enumerate(POOL_WINDOWS):
            acc = vext[:, k * gd:(k + 1) * gd]
            span = 1
            while span < w:
                shift = span * BATCH_ROWS
                acc = acc[shift:, :] + acc[:-shift, :]
                span *= 2
            yps.append(acc[acc.shape[0] - rows:, :] * (1.0 / w) - v[:, k * gd:(k + 1) * gd])
        y_pool = _rms(jnp.concatenate(yps, axis=-1), png_ref[...])

        mixed = jnp.concatenate([y_ssm, y_pool], axis=-1).astype(_BF16)
        h_next = x + _dot(mixed, wout_ref[...])
        hbuf[...] = h_next
        n2 = _rms(h_next, n2g_ref[...]).astype(_BF16)
        n2buf[...] = n2
        if with_prev:
            finish_prev()
        for c in range(FFN_HEAD_CHUNKS):
            ffn_chunk(n2, c)

    @pl.when(i == 0)
    def _():
        for c in fetch(0, 0):
            c.start()

    @pl.when(i + 1 < n_tiles)
    def _():
        for c in fetch(i + 1, oslot):
            c.start()

    @pl.when(i >= 3)
    def _():
        for c in writeback(i - 3, oslot):
            c.wait()

    @pl.when(i == 0)
    def _():
        build_weights()
        st_r[...] = jnp.zeros_like(st_r)
        st_i[...] = jnp.zeros_like(st_i)
        u_last[...] = jnp.zeros_like(u_last)
        pm = project(meta_ref[...])
        state_in, _ = s5_inputs(pm[:, :d_ssm], N_META // PHASES)
        for qq in range(n_quads // SCAN_QUADS):
            s5_states(state_in, N_META // PHASES, qq)
        halo[...] = pm[:, d_ssm:]

    @pl.when(i < n_tiles)
    def _():
        for c in fetch(i, slot):
            c.wait()

    @pl.when(i == 0)
    def _():
        step(with_prev=False, with_mixer=True)

    @pl.when(jnp.logical_and(i > 0, i < n_tiles))
    def _():
        step(with_prev=True, with_mixer=True)

    @pl.when(i == n_tiles)
    def _():
        step(with_prev=True, with_mixer=False)

    @pl.when(i >= 1)
    def _():
        for c in writeback(i - 1, oslot):
            c.start()

    @pl.when(i == n_tiles)
    def _():
        @pl.when(n_tiles >= 2)
        def _():
            for c in writeback(i - 2, slot):
                c.wait()
        for c in writeback(i - 1, oslot):
            c.wait()


def _const_spec(a):
    nd = a.ndim
    return pl.BlockSpec(a.shape, lambda i: (0,) * nd, pipeline_mode=pl.Buffered(1))


def _block(x, consts, hbm_weights, n_state, d_ssm, d_pool, d_ff):
    B, S, D = x.shape
    rows = T_TILE * B
    any_spec = pl.BlockSpec(memory_space=pl.ANY)
    n_quad = d_ssm // (QUAD_GROUPS * SSM_GROUP)
    n_half = d_ssm // (HALF_GROUPS * SSM_GROUP)
    tile_k = PHASES * QUAD_GROUPS * SSM_GROUP
    quad_state = QUAD_GROUPS * SSM_STATE
    return pl.pallas_call(
        _block_kernel,
        grid=(S // T_TILE + 1,),
        in_specs=[any_spec] + [_const_spec(c) for c in consts] + [any_spec] * len(hbm_weights),
        out_specs=any_spec,
        out_shape=jax.ShapeDtypeStruct((B, S, D), _F32),
        scratch_shapes=[
            pltpu.VMEM((2, T_TILE, B, D), _F32),
            pltpu.SemaphoreType.DMA((2, BATCH_ROWS)),
            pltpu.VMEM((2, T_TILE, B, D), _F32),
            pltpu.SemaphoreType.DMA((2, BATCH_ROWS)),
            pltpu.VMEM((rows, D), _F32),
            pltpu.VMEM((rows, D), _BF16),
            pltpu.VMEM((rows // PHASES, n_state), _F32),
            pltpu.VMEM((rows // PHASES, n_state), _F32),
            pltpu.VMEM((B, n_state), _F32),
            pltpu.VMEM((B, n_state), _F32),
            pltpu.VMEM(((PHASES - 1) * B, d_ssm), _F32),
            pltpu.VMEM((N_META * B, d_pool), _F32),
            pltpu.VMEM((rows, d_ff), _BF16),
            pltpu.VMEM((D, D), _BF16),
            pltpu.VMEM((D, D), _BF16),
            pltpu.VMEM((D, d_ff), _BF16),
            pltpu.VMEM((D, d_ff), _BF16),
            pltpu.VMEM((d_ff, D), _BF16),
            pltpu.VMEM((2, n_quad, tile_k, quad_state), _BF16),
            pltpu.VMEM((2, n_quad, quad_state, tile_k), _BF16),
            pltpu.VMEM((n_quad, tile_k, tile_k), _BF16),
            pltpu.VMEM((n_half, HALF_GROUPS * SSM_GROUP, HALF_GROUPS * SSM_GROUP), _BF16),
            pltpu.VMEM((STAGE_SLOTS * NARROW_STAGE_ROWS, D), _F32),
            pltpu.VMEM((STAGE_SLOTS * WIDE_STAGE_ROWS, d_ff), _F32),
            pltpu.SemaphoreType.DMA((STAGE_SLOTS,)),
        ],
        compiler_params=pltpu.CompilerParams(dimension_semantics=("arbitrary",),
                                             vmem_limit_bytes=VMEM_LIMIT),
        name="block",
    )(x, *consts, *hbm_weights)


def _lanes_by_group(blocks, per):
    G, r, c = blocks.shape
    return jnp.transpose(blocks.reshape(G // per, per, r, c), (0, 2, 1, 3)).reshape(G // per, r, per * c)


def _cmul(ar, ai, br, bi):
    return ar * br - ai * bi, ar * bi + ai * br


def _s5_params(lam_re, lam_im, log_step, b_re, b_im, c_re, c_im):
    lr = jnp.minimum(lam_re, -1e-4)
    li = lam_im
    step = jnp.exp(log_step)[:, None]
    mag = jnp.exp(lr * step)
    ang = li * step
    abr = mag * jnp.cos(ang)
    abi = mag * jnp.sin(ang)
    nr = abr - 1.0
    ni = abi
    den = lr * lr + li * li
    cr = ((nr * lr + ni * li) / den)[..., None]
    ci = ((ni * lr - nr * li) / den)[..., None]
    bbr = cr * b_re - ci * b_im
    bbi = cr * b_im + ci * b_re
    pw = [(jnp.ones_like(abr), jnp.zeros_like(abi))]
    for _ in range(PHASES):
        pw.append(_cmul(pw[-1][0], pw[-1][1], abr, abi))
    G = bbr.shape[0]
    Q = G // QUAD_GROUPS
    t = lambda a: jnp.swapaxes(a, 1, 2)

    def by_quad(blocks, lead):
        packed = _lanes_by_group(jnp.concatenate(blocks, axis=0), QUAD_GROUPS)
        return packed.reshape(lead + (Q,) + packed.shape[1:])

    ab = [_cmul(pr[..., None], pi[..., None], bbr, bbi) for pr, pi in pw[:PHASES]]
    ca = [_cmul(c_re, c_im, pr[:, None, :], pi[:, None, :]) for pr, pi in pw[:PHASES]]
    zc = by_quad([t(ab[PHASES - 1 - ph][ri]) for ri in range(2) for ph in range(PHASES)], (2, PHASES))
    sign = (1.0, -1.0)
    vc = by_quad([sign[ri] * t(ca[ph][ri]) for ri in range(2) for ph in range(PHASES)], (2, PHASES))
    loc = [jnp.einsum('ghp,gpk->ghk', ca[d][0], bbr) - jnp.einsum('ghp,gpk->ghk', ca[d][1], bbi)
           for d in range(PHASES - 1)]
    zero = jnp.zeros_like(loc[0])
    locc = by_quad([t(loc[p - j]) if 1 <= j <= p else zero for j in range(PHASES) for p in range(PHASES)],
                   (PHASES, PHASES))
    return pw[PHASES], zc, vc, locc


def kernel(x, meta_tokens, norm1_g, w_in, ssm_lambda_re, ssm_lambda_im, ssm_log_step, ssm_b_re, ssm_b_im, ssm_c_re, ssm_c_im, ssm_d, ssm_glu_w, ssm_glu_b, ssm_norm_g, pool_w, pool_scale, pool_norm_g, w_out, norm2_g, w_gate, w_up, w_down, final_norm_g):
    B, S, D = x.shape
    assert B == BATCH_ROWS and norm1_g.shape[0] == 1
    assert S % T_TILE == 0 and T_TILE % PHASES == 0 and N_META % PHASES == 0
    f32 = _F32
    (a4r, a4i), zc, vc, locc = _s5_params(
        ssm_lambda_re[0].astype(f32), ssm_lambda_im[0].astype(f32), ssm_log_step[0].astype(f32),
        ssm_b_re[0].astype(f32), ssm_b_im[0].astype(f32), ssm_c_re[0].astype(f32), ssm_c_im[0].astype(f32))
    row = lambda a: a.astype(f32).reshape(1, -1)
    consts = (
        jnp.repeat(meta_tokens.astype(f32), B, axis=0),
        row(norm1_g[0]),
        row(a4r), row(a4i),
        zc,
        vc,
        locc,
        row(ssm_d[0]),
        _lanes_by_group(ssm_glu_w[0].astype(f32), HALF_GROUPS),
        row(ssm_glu_b[0]),
        row(ssm_norm_g[0]),
        pool_w[0].astype(f32),
        row(pool_scale[0]),
        row(pool_norm_g[0]),
        row(norm2_g[0]),
        row(final_norm_g),
    )
    hbm_weights = tuple(w.astype(f32) for w in (w_in, w_out, w_gate, w_up, w_down))
    out = _block(x.astype(f32), consts, hbm_weights, n_state=a4r.size, d_ssm=ssm_d[0].size,
                 d_pool=pool_scale[0].size, d_ff=w_gate.shape[-1])
    return out.astype(x.dtype)
```

```python
import jax
import jax.numpy as jnp
from jax import lax
from jax.experimental import pallas as pl
from jax.experimental.pallas import tpu as pltpu

N_META = 16
SSM_GROUP = 16
SSM_STATE = 64
POOL_WINDOWS = (2, 4, 8, 16)
EPS = 1e-6

BATCH_ROWS = 8
HALF_GROUPS = 16
PHASES = 4
QUAD_GROUPS = 4
SCAN_QUADS = 2
T_TILE = 64
FF_CHUNK = 256
FFN_HEAD_CHUNKS = 1
STAGE_SLOTS = 8
WIDE_STAGE_ROWS = 32
NARROW_STAGE_ROWS = 128
VMEM_LIMIT = 60 * 1024 * 1024

_BF16 = jnp.bfloat16
_F32 = jnp.float32


def _rms(x, g):
    return x * lax.rsqrt(jnp.mean(x * x, axis=-1, keepdims=True) + EPS) * g


def _dot(a, b):
    return jnp.dot(a, b, preferred_element_type=_F32)


def _split_phases(a, n_chunks):
    blk = lambda t: a[t * BATCH_ROWS:(t + 1) * BATCH_ROWS]
    return [jnp.concatenate([blk(PHASES * k + p) for k in range(n_chunks)], axis=0) for p in range(PHASES)]


def _merge_phases(parts, n_chunks):
    blk = lambda a, k: a[k * BATCH_ROWS:(k + 1) * BATCH_ROWS]
    return jnp.concatenate([blk(p, k) for k in range(n_chunks) for p in parts], axis=0)


def _tile_copies(hbm, vmem, sems, slot, tile, nt, to_vmem):
    copies = []
    for b in range(BATCH_ROWS):
        h = hbm.at[b, pl.ds(tile * nt, nt), :]
        v = vmem.at[slot, :, b, :]
        src, dst = (h, v) if to_vmem else (v, h)
        copies.append(pltpu.make_async_copy(src, dst, sems.at[slot, b]))
    return copies


def _load_cast(w_hbm, dst, stage, sems, chunk, convert=None):
    if convert is None:
        convert = lambda rows: rows.astype(dst.dtype)
    n = dst.shape[0] // chunk
    ahead = STAGE_SLOTS - 1

    def copy(c, s):
        return pltpu.make_async_copy(w_hbm.at[0, pl.ds(c * chunk, chunk), :],
                                     stage.at[pl.ds(s * chunk, chunk), :], sems.at[s])

    for c in range(min(ahead, n)):
        copy(c, c).start()

    def body(c, carry):
        s = lax.rem(c, STAGE_SLOTS)

        @pl.when(c + ahead < n)
        def _():
            copy(c + ahead, lax.rem(c + ahead, STAGE_SLOTS)).start()

        copy(c, s).wait()
        src = pl.multiple_of(s * chunk, chunk)
        row = pl.multiple_of(c * chunk, chunk)
        dst[pl.ds(row, chunk), :] = convert(stage[pl.ds(src, chunk), :])
        return carry

    lax.fori_loop(0, n, body, 0)


def _expand_block_diag(src, dst, block_rows, block_lanes, n_blocks):
    lane = lax.broadcasted_iota(jnp.int32, src.shape, 1)
    lane_group = lax.rem(lane, n_blocks * block_lanes) // block_lanes
    for g in range(n_blocks):
        dst[g * block_rows:(g + 1) * block_rows, :] = jnp.where(lane_group == g, src, 0.0).astype(dst.dtype)


def _block_kernel(x_hbm, meta_ref, n1g_ref, a4r_ref, a4i_ref, zc_ref, vc_ref, locc_ref, d_ref, gluc_ref,
                  gb_ref, sng_ref, pw_ref, ps_ref, png_ref, n2g_ref, fg_ref,
                  win_hbm, wout_hbm, wg_hbm, wu_hbm, wd_hbm, o_hbm,
                  xbuf, xsem, obuf, osem, hbuf, n2buf, head_raw, zr, zi, st_r, st_i, u_last, halo, act,
                  win_ref, wout_ref, wg_ref, wu_ref, wd_ref, wz_ref, v_ref, loc_ref, gw_ref,
                  narrow_stage, wide_stage, wsem):
    i = pl.program_id(0)
    n_tiles = pl.num_programs(0) - 1
    slot = lax.rem(i, 2)
    oslot = 1 - slot

    def fetch(tile, s):
        return _tile_copies(x_hbm, xbuf, xsem, s, tile, T_TILE, to_vmem=True)

    def writeback(tile, s):
        return _tile_copies(o_hbm, obuf, osem, s, tile, T_TILE, to_vmem=False)

    d_ssm = d_ref.shape[-1]
    half_in = HALF_GROUPS * SSM_GROUP
    quad_in = QUAD_GROUPS * SSM_GROUP
    quad_state = QUAD_GROUPS * SSM_STATE
    n_quads = d_ssm // quad_in
    n_halves = d_ssm // half_in
    halo_rows = halo.shape[0]
    n_ff_chunks = wg_ref.shape[-1] // FF_CHUNK

    def build_weights():
        gd = pw_ref.shape[-1]
        pool_maps = [pw_ref[k] * ps_ref[:, k * gd:(k + 1) * gd] for k in range(pw_ref.shape[0])]

        def fold_pool(rows):
            cols = [rows[:, :d_ssm]]
            for k, pm in enumerate(pool_maps):
                cols.append(jnp.dot(rows[:, d_ssm + k * gd:d_ssm + (k + 1) * gd], pm,
                                    precision=lax.Precision.HIGHEST, preferred_element_type=_F32))
            return jnp.concatenate(cols, axis=-1).astype(_BF16)

        _load_cast(win_hbm, win_ref, narrow_stage, wsem, NARROW_STAGE_ROWS, convert=fold_pool)
        _load_cast(wout_hbm, wout_ref, narrow_stage, wsem, NARROW_STAGE_ROWS)
        _load_cast(wd_hbm, wd_ref, narrow_stage, wsem, NARROW_STAGE_ROWS)
        _load_cast(wg_hbm, wg_ref, wide_stage, wsem, WIDE_STAGE_ROWS)
        _load_cast(wu_hbm, wu_ref, wide_stage, wsem, WIDE_STAGE_ROWS)
        ph_rows = QUAD_GROUPS * SSM_GROUP
        for q in range(n_quads):
            for ri in range(2):
                for ph in range(PHASES):
                    _expand_block_diag(zc_ref[ri, ph, q], wz_ref.at[ri, q, ph * ph_rows:(ph + 1) * ph_rows, :],
                                       SSM_GROUP, SSM_STATE, QUAD_GROUPS)
                by_phase = jnp.concatenate([vc_ref[ri, ph, q] for ph in range(PHASES)], axis=-1)
                _expand_block_diag(by_phase, v_ref.at[ri, q], SSM_STATE, SSM_GROUP, QUAD_GROUPS)
            for j in range(PHASES):
                by_phase = jnp.concatenate([locc_ref[j, p, q] for p in range(PHASES)], axis=-1)
                _expand_block_diag(by_phase, loc_ref.at[q, j * ph_rows:(j + 1) * ph_rows, :],
                                   SSM_GROUP, SSM_GROUP, QUAD_GROUPS)
        for hh in range(n_halves):
            _expand_block_diag(gluc_ref[hh], gw_ref.at[hh], SSM_GROUP, SSM_GROUP, HALF_GROUPS)

    def project(rows):
        n1 = _rms(rows, n1g_ref[...]).astype(_BF16)
        return _dot(n1, win_ref[...])

    def quad_lanes(parts, q):
        return jnp.concatenate([p[:, q * quad_in:(q + 1) * quad_in] for p in parts], axis=1)

    def s5_states(u_in, n_chunks, qq):
        rows = n_chunks * BATCH_ROWS
        for q in range(qq * SCAN_QUADS, (qq + 1) * SCAN_QUADS):
            qs = slice(q * quad_state, (q + 1) * quad_state)
            lhs = quad_lanes(u_in, q)
            zr[0:rows, qs] = _dot(lhs, wz_ref[0, q])
            zi[0:rows, qs] = _dot(lhs, wz_ref[1, q])
        cs = slice(qq * SCAN_QUADS * quad_state, (qq + 1) * SCAN_QUADS * quad_state)
        width = SCAN_QUADS * quad_state
        ar = jnp.broadcast_to(a4r_ref[:, cs], (BATCH_ROWS, width))
        ai = jnp.broadcast_to(a4i_ref[:, cs], (BATCH_ROWS, width))
        sr = st_r[:, cs]
        si = st_i[:, cs]
        for k in range(n_chunks):
            rs = slice(k * BATCH_ROWS, (k + 1) * BATCH_ROWS)
            sr, si = (ar * sr - ai * si + zr[rs, cs], ar * si + ai * sr + zi[rs, cs])
            zr[rs, cs] = sr
            zi[rs, cs] = si
        st_r[:, cs] = sr
        st_i[:, cs] = si

    def s5_inputs(u, n_chunks):
        parts = _split_phases(u, n_chunks)
        state_in = []
        for p in range(1, PHASES):
            carry = slice((p - 1) * BATCH_ROWS, p * BATCH_ROWS)
            state_in.append(jnp.concatenate([u_last[carry, :], parts[p][:-BATCH_ROWS]], axis=0).astype(_BF16))
            u_last[carry, :] = parts[p][-BATCH_ROWS:]
        local_in = [p.astype(_BF16) for p in parts]
        return state_in + [local_in[0]], local_in

    def swiglu(gate, up):
        return (gate * jax.nn.sigmoid(gate) * up).astype(_BF16)

    def ffn_chunk(n2, c):
        cs = slice(c * FF_CHUNK, (c + 1) * FF_CHUNK)
        act[:, cs] = swiglu(_dot(n2, wg_ref[:, cs]), _dot(n2, wu_ref[:, cs]))

    def step(with_prev, with_mixer):
        rows = T_TILE * BATCH_ROWS
        n_chunks = T_TILE // PHASES
        crows = n_chunks * BATCH_ROWS
        ff = iter(range(FFN_HEAD_CHUNKS, n_ff_chunks) if with_prev else ())

        def ffn_chunks(n):
            for _ in range(n):
                c = next(ff, None)
                if c is not None:
                    ffn_chunk(n2_prev, c)

        def finish_prev():
            ffn_chunks(n_ff_chunks)
            h2 = obuf[oslot].reshape(rows, obuf.shape[-1]) + _dot(act[...], wd_ref[...])
            obuf[oslot] = _rms(h2, fg_ref[...]).reshape(obuf.shape[1:])

        if with_prev:
            n2_prev = n2buf[...]
            obuf[oslot] = hbuf[...].reshape(obuf.shape[1:])
            act[:, 0:FF_CHUNK] = swiglu(head_raw[0], head_raw[1])
            ffn_chunks(1)
        if not with_mixer:
            finish_prev()
            return

        x = xbuf[slot].reshape(rows, xbuf.shape[-1])
        n1 = _rms(x, n1g_ref[...]).astype(_BF16)
        proj = _dot(n1, win_ref[...])
        u = proj[:, :d_ssm]
        v = proj[:, d_ssm:]

        state_in, local_in = s5_inputs(u, n_chunks)
        for qq in range(n_quads // SCAN_QUADS):
            s5_states(state_in, n_chunks, qq)
            ffn_chunks(1)
        chunk_out = []
        for q in range(n_quads):
            qs = slice(q * quad_state, (q + 1) * quad_state)
            chunk_out.append(_dot(zr[0:crows, qs].astype(_BF16), v_ref[0, q])
                             + _dot(zi[0:crows, qs].astype(_BF16), v_ref[1, q])
                             + _dot(quad_lanes(local_in, q), loc_ref[q]))
            if q % 4 == 3:
                ffn_chunks(1)
        y_parts = [jnp.concatenate([c[:, p * quad_in:(p + 1) * quad_in] for c in chunk_out], axis=-1)
                   for p in range(PHASES)]
        y = _merge_phases(y_parts, n_chunks) + d_ref[...] * u
        g = jax.nn.gelu(y)
        gb = g.astype(_BF16)
        ffn_chunks(1)
        gate = jnp.concatenate([_dot(gb[:, hh * half_in:(hh + 1) * half_in], gw_ref[hh])
                                for hh in range(n_halves)], axis=-1) + gb_ref[...]
        y_ssm = _rms(g * jax.nn.sigmoid(gate), sng_ref[...])
        ffn_chunks(1)

        vext = jnp.concatenate([halo[...], v], axis=0)
        halo[...] = v[rows - halo_rows:, :]
        gd = v.shape[-1] // len(POOL_WINDOWS)
        yps = []
        for k, w in enumerate(POOL_WINDOWS):
            acc = vext[:, k * gd:(k + 1) * gd]
            span = 1
            while span < w:
                shift = span * BATCH_ROWS
                acc = acc[shift:, :] + acc[:-shift, :]
                span *= 2
            yps.append(acc[acc.shape[0] - rows:, :] * (1.0 / w) - v[:, k * gd:(k + 1) * gd])
        y_pool = _rms(jnp.concatenate(yps, axis=-1), png_ref[...])

        mixed = jnp.concatenate([y_ssm, y_pool], axis=-1).astype(_BF16)
        h_next = x + _dot(mixed, wout_ref[...])
        hbuf[...] = h_next
        n2 = _rms(h_next, n2g_ref[...]).astype(_BF16)
        n2buf[...] = n2
        if with_prev:
            finish_prev()
        head = slice(0, FF_CHUNK)
        head_raw[0] = _dot(n2, wg_ref[:, head])
        head_raw[1] = _dot(n2, wu_ref[:, head])

    @pl.when(i == 0)
    def _():
        for c in fetch(0, 0):
            c.start()

    @pl.when(i + 1 < n_tiles)
    def _():
        for c in fetch(i + 1, oslot):
            c.start()

    @pl.when(i >= 3)
    def _():
        for c in writeback(i - 3, oslot):
            c.wait()

    @pl.when(i == 0)
    def _():
        build_weights()
        st_r[...] = jnp.zeros_like(st_r)
        st_i[...] = jnp.zeros_like(st_i)
        u_last[...] = jnp.zeros_like(u_last)
        pm = project(meta_ref[...])
        state_in, _ = s5_inputs(pm[:, :d_ssm], N_META // PHASES)
        for qq in range(n_quads // SCAN_QUADS):
            s5_states(state_in, N_META // PHASES, qq)
        halo[...] = pm[:, d_ssm:]

    @pl.when(i < n_tiles)
    def _():
        for c in fetch(i, slot):
            c.wait()

    @pl.when(i == 0)
    def _():
        step(with_prev=False, with_mixer=True)

    @pl.when(jnp.logical_and(i > 0, i < n_tiles))
    def _():
        step(with_prev=True, with_mixer=True)

    @pl.when(i == n_tiles)
    def _():
        step(with_prev=True, with_mixer=False)

    @pl.when(i >= 1)
    def _():
        for c in writeback(i - 1, oslot):
            c.start()

    @pl.when(i == n_tiles)
    def _():
        @pl.when(n_tiles >= 2)
        def _():
            for c in writeback(i - 2, slot):
                c.wait()
        for c in writeback(i - 1, oslot):
            c.wait()


def _const_spec(a):
    nd = a.ndim
    return pl.BlockSpec(a.shape, lambda i: (0,) * nd, pipeline_mode=pl.Buffered(1))


def _block(x, consts, hbm_weights, n_state, d_ssm, d_pool, d_ff):
    B, S, D = x.shape
    rows = T_TILE * B
    any_spec = pl.BlockSpec(memory_space=pl.ANY)
    n_quad = d_ssm // (QUAD_GROUPS * SSM_GROUP)
    n_half = d_ssm // (HALF_GROUPS * SSM_GROUP)
    tile_k = PHASES * QUAD_GROUPS * SSM_GROUP
    quad_state = QUAD_GROUPS * SSM_STATE
    return pl.pallas_call(
        _block_kernel,
        grid=(S // T_TILE + 1,),
        in_specs=[any_spec] + [_const_spec(c) for c in consts] + [any_spec] * len(hbm_weights),
        out_specs=any_spec,
        out_shape=jax.ShapeDtypeStruct((B, S, D), _F32),
        scratch_shapes=[
            pltpu.VMEM((2, T_TILE, B, D), _F32),
            pltpu.SemaphoreType.DMA((2, BATCH_ROWS)),
            pltpu.VMEM((2, T_TILE, B, D), _F32),
            pltpu.SemaphoreType.DMA((2, BATCH_ROWS)),
            pltpu.VMEM((rows, D), _F32),
            pltpu.VMEM((rows, D), _BF16),
            pltpu.VMEM((2, rows, FF_CHUNK), _F32),
            pltpu.VMEM((rows // PHASES, n_state), _F32),
            pltpu.VMEM((rows // PHASES, n_state), _F32),
            pltpu.VMEM((B, n_state), _F32),
            pltpu.VMEM((B, n_state), _F32),
            pltpu.VMEM(((PHASES - 1) * B, d_ssm), _F32),
            pltpu.VMEM((N_META * B, d_pool), _F32),
            pltpu.VMEM((rows, d_ff), _BF16),
            pltpu.VMEM((D, D), _BF16),
            pltpu.VMEM((D, D), _BF16),
            pltpu.VMEM((D, d_ff), _BF16),
            pltpu.VMEM((D, d_ff), _BF16),
            pltpu.VMEM((d_ff, D), _BF16),
            pltpu.VMEM((2, n_quad, tile_k, quad_state), _BF16),
            pltpu.VMEM((2, n_quad, quad_state, tile_k), _BF16),
            pltpu.VMEM((n_quad, tile_k, tile_k), _BF16),
            pltpu.VMEM((n_half, HALF_GROUPS * SSM_GROUP, HALF_GROUPS * SSM_GROUP), _BF16),
            pltpu.VMEM((STAGE_SLOTS * NARROW_STAGE_ROWS, D), _F32),
            pltpu.VMEM((STAGE_SLOTS * WIDE_STAGE_ROWS, d_ff), _F32),
            pltpu.SemaphoreType.DMA((STAGE_SLOTS,)),
        ],
        compiler_params=pltpu.CompilerParams(dimension_semantics=("arbitrary",),
                                             vmem_limit_bytes=VMEM_LIMIT),
        name="block",
    )(x, *consts, *hbm_weights)


def _lanes_by_group(blocks, per):
    G, r, c = blocks.shape
    return jnp.transpose(blocks.reshape(G // per, per, r, c), (0, 2, 1, 3)).reshape(G // per, r, per * c)


def _cmul(ar, ai, br, bi):
    return ar * br - ai * bi, ar * bi + ai * br


def _s5_params(lam_re, lam_im, log_step, b_re, b_im, c_re, c_im):
    lr = jnp.minimum(lam_re, -1e-4)
    li = lam_im
    step = jnp.exp(log_step)[:, None]
    mag = jnp.exp(lr * step)
    ang = li * step
    abr = mag * jnp.cos(ang)
    abi = mag * jnp.sin(ang)
    nr = abr - 1.0
    ni = abi
    den = lr * lr + li * li
    cr = ((nr * lr + ni * li) / den)[..., None]
    ci = ((ni * lr - nr * li) / den)[..., None]
    bbr = cr * b_re - ci * b_im
    bbi = cr * b_im + ci * b_re
    pw = [(jnp.ones_like(abr), jnp.zeros_like(abi))]
    for _ in range(PHASES):
        pw.append(_cmul(pw[-1][0], pw[-1][1], abr, abi))
    G = bbr.shape[0]
    Q = G // QUAD_GROUPS
    t = lambda a: jnp.swapaxes(a, 1, 2)

    def by_quad(blocks, lead):
        packed = _lanes_by_group(jnp.concatenate(blocks, axis=0), QUAD_GROUPS)
        return packed.reshape(lead + (Q,) + packed.shape[1:])

    ab = [_cmul(pr[..., None], pi[..., None], bbr, bbi) for pr, pi in pw[:PHASES]]
    ca = [_cmul(c_re, c_im, pr[:, None, :], pi[:, None, :]) for pr, pi in pw[:PHASES]]
    zc = by_quad([t(ab[PHASES - 1 - ph][ri]) for ri in range(2) for ph in range(PHASES)], (2, PHASES))
    sign = (1.0, -1.0)
    vc = by_quad([sign[ri] * t(ca[ph][ri]) for ri in range(2) for ph in range(PHASES)], (2, PHASES))
    loc = [jnp.einsum('ghp,gpk->ghk', ca[d][0], bbr) - jnp.einsum('ghp,gpk->ghk', ca[d][1], bbi)
           for d in range(PHASES - 1)]
    zero = jnp.zeros_like(loc[0])
    locc = by_quad([t(loc[p - j]) if 1 <= j <= p else zero for j in range(PHASES) for p in range(PHASES)],
                   (PHASES, PHASES))
    return pw[PHASES], zc, vc, locc


def kernel(x, meta_tokens, norm1_g, w_in, ssm_lambda_re, ssm_lambda_im, ssm_log_step, ssm_b_re, ssm_b_im, ssm_c_re, ssm_c_im, ssm_d, ssm_glu_w, ssm_glu_b, ssm_norm_g, pool_w, pool_scale, pool_norm_g, w_out, norm2_g, w_gate, w_up, w_down, final_norm_g):
    B, S, D = x.shape
    assert B == BATCH_ROWS and norm1_g.shape[0] == 1
    assert S % T_TILE == 0 and T_TILE % PHASES == 0 and N_META % PHASES == 0
    f32 = _F32
    (a4r, a4i), zc, vc, locc = _s5_params(
        ssm_lambda_re[0].astype(f32), ssm_lambda_im[0].astype(f32), ssm_log_step[0].astype(f32),
        ssm_b_re[0].astype(f32), ssm_b_im[0].astype(f32), ssm_c_re[0].astype(f32), ssm_c_im[0].astype(f32))
    row = lambda a: a.astype(f32).reshape(1, -1)
    consts = (
        jnp.repeat(meta_tokens.astype(f32), B, axis=0),
        row(norm1_g[0]),
        row(a4r), row(a4i),
        zc,
        vc,
        locc,
        row(ssm_d[0]),
        _lanes_by_group(ssm_glu_w[0].astype(f32), HALF_GROUPS),
        row(ssm_glu_b[0]),
        row(ssm_norm_g[0]),
        pool_w[0].astype(f32),
        row(pool_scale[0]),
        row(pool_norm_g[0]),
        row(norm2_g[0]),
        row(final_norm_g),
    )
    hbm_weights = tuple(w.astype(f32) for w in (w_in, w_out, w_gate, w_up, w_down))
    out = _block(x.astype(f32), consts, hbm_weights, n_state=a4r.size, d_ssm=ssm_d[0].size,
                 d_pool=pool_scale[0].size, d_ff=w_gate.shape[-1])
    return out.astype(x.dtype)
```

```python
import jax
import jax.numpy as jnp
from jax import lax
from jax.experimental import pallas as pl
from jax.experimental.pallas import tpu as pltpu

N_META = 16
SSM_GROUP = 16
SSM_STATE = 64
POOL_WINDOWS = (2, 4, 8, 16)
EPS = 1e-6

BATCH_ROWS = 8
HALF_GROUPS = 16
PHASES = 4
QUAD_GROUPS = 4
SCAN_QUADS = 2
T_TILE = 64
FF_CHUNK = 256
FFN_HEAD_CHUNKS = 1
STAGE_SLOTS = 8
WIDE_STAGE_ROWS = 32
NARROW_STAGE_ROWS = 128
VMEM_LIMIT = 60 * 1024 * 1024

_BF16 = jnp.bfloat16
_F32 = jnp.float32


def _rms(x, g):
    return x * lax.rsqrt(jnp.mean(x * x, axis=-1, keepdims=True) + EPS) * g


def _dot(a, b):
    return jnp.dot(a, b, preferred_element_type=_F32)


def _split_phases(a, n_chunks):
    blk = lambda t: a[t * BATCH_ROWS:(t + 1) * BATCH_ROWS]
    return [jnp.concatenate([blk(PHASES * k + p) for k in range(n_chunks)], axis=0) for p in range(PHASES)]


def _merge_phases(parts, n_chunks):
    blk = lambda a, k: a[k * BATCH_ROWS:(k + 1) * BATCH_ROWS]
    return jnp.concatenate([blk(p, k) for k in range(n_chunks) for p in parts], axis=0)


def _tile_copies(hbm, vmem, sems, slot, tile, nt, to_vmem):
    copies = []
    for b in range(BATCH_ROWS):
        h = hbm.at[b, pl.ds(tile * nt, nt), :]
        v = vmem.at[slot, :, b, :]
        src, dst = (h, v) if to_vmem else (v, h)
        copies.append(pltpu.make_async_copy(src, dst, sems.at[slot, b]))
    return copies


class _WeightStream:
    def __init__(self, w_hbm, dst, stage, sems, chunk, convert=None):
        self.args = (w_hbm, dst, stage, sems, chunk,
                     convert if convert is not None else (lambda rows: rows.astype(dst.dtype)))

    def _copy(self, c, s):
        w_hbm, _, stage, sems, chunk, _ = self.args
        return pltpu.make_async_copy(w_hbm.at[0, pl.ds(c * chunk, chunk), :],
                                     stage.at[pl.ds(s * chunk, chunk), :], sems.at[s])

    def prime(self):
        _, dst, _, _, chunk, _ = self.args
        for c in range(min(STAGE_SLOTS - 1, dst.shape[0] // chunk)):
            self._copy(c, c).start()

    def drain(self):
        _load_drain(self)


def _load_drain(stream):
    _, dst, stage, _, chunk, convert = stream.args
    copy = stream._copy
    n = dst.shape[0] // chunk
    ahead = STAGE_SLOTS - 1

    def body(c, carry):
        s = lax.rem(c, STAGE_SLOTS)

        @pl.when(c + ahead < n)
        def _():
            copy(c + ahead, lax.rem(c + ahead, STAGE_SLOTS)).start()

        copy(c, s).wait()
        src = pl.multiple_of(s * chunk, chunk)
        row = pl.multiple_of(c * chunk, chunk)
        dst[pl.ds(row, chunk), :] = convert(stage[pl.ds(src, chunk), :])
        return carry

    lax.fori_loop(0, n, body, 0)


def _expand_block_diag(src, dst, block_rows, block_lanes, n_blocks):
    lane = lax.broadcasted_iota(jnp.int32, src.shape, 1)
    lane_group = lax.rem(lane, n_blocks * block_lanes) // block_lanes
    for g in range(n_blocks):
        dst[g * block_rows:(g + 1) * block_rows, :] = jnp.where(lane_group == g, src, 0.0).astype(dst.dtype)


def _block_kernel(x_hbm, meta_ref, n1g_ref, a4r_ref, a4i_ref, zc_ref, vc_ref, locc_ref, d_ref, gluc_ref,
                  gb_ref, sng_ref, pw_ref, ps_ref, png_ref, n2g_ref, fg_ref,
                  win_hbm, wout_hbm, wg_hbm, wu_hbm, wd_hbm, o_hbm,
                  xbuf, xsem, obuf, osem, hbuf, n2buf, zr, zi, st_r, st_i, u_last, halo, act,
                  win_ref, wout_ref, wg_ref, wu_ref, wd_ref, wz_ref, v_ref, loc_ref, gw_ref,
                  narrow_stage, wide_stage, nsem, wsem):
    i = pl.program_id(0)
    n_tiles = pl.num_programs(0) - 1
    slot = lax.rem(i, 2)
    oslot = 1 - slot

    def fetch(tile, s):
        return _tile_copies(x_hbm, xbuf, xsem, s, tile, T_TILE, to_vmem=True)

    def writeback(tile, s):
        return _tile_copies(o_hbm, obuf, osem, s, tile, T_TILE, to_vmem=False)

    d_ssm = d_ref.shape[-1]
    half_in = HALF_GROUPS * SSM_GROUP
    quad_in = QUAD_GROUPS * SSM_GROUP
    quad_state = QUAD_GROUPS * SSM_STATE
    n_quads = d_ssm // quad_in
    n_halves = d_ssm // half_in
    halo_rows = halo.shape[0]
    n_ff_chunks = wg_ref.shape[-1] // FF_CHUNK

    def build_weights():
        gd = pw_ref.shape[-1]
        pool_maps = [pw_ref[k] * ps_ref[:, k * gd:(k + 1) * gd] for k in range(pw_ref.shape[0])]

        def fold_pool(rows):
            cols = [rows[:, :d_ssm]]
            for k, pm in enumerate(pool_maps):
                cols.append(jnp.dot(rows[:, d_ssm + k * gd:d_ssm + (k + 1) * gd], pm,
                                    precision=lax.Precision.HIGHEST, preferred_element_type=_F32))
            return jnp.concatenate(cols, axis=-1).astype(_BF16)

        w_in_s = _WeightStream(win_hbm, win_ref, narrow_stage, nsem, NARROW_STAGE_ROWS, convert=fold_pool)
        w_out_s = _WeightStream(wout_hbm, wout_ref, narrow_stage, nsem, NARROW_STAGE_ROWS)
        w_down_s = _WeightStream(wd_hbm, wd_ref, narrow_stage, nsem, NARROW_STAGE_ROWS)
        w_gate_s = _WeightStream(wg_hbm, wg_ref, wide_stage, wsem, WIDE_STAGE_ROWS)
        w_up_s = _WeightStream(wu_hbm, wu_ref, wide_stage, wsem, WIDE_STAGE_ROWS)
        w_in_s.prime()
        w_gate_s.prime()
        expand_group_maps()
        w_in_s.drain()
        w_out_s.prime()
        w_gate_s.drain()
        w_up_s.prime()
        w_out_s.drain()
        w_down_s.prime()
        w_up_s.drain()
        w_down_s.drain()

    def expand_group_maps():
        ph_rows = QUAD_GROUPS * SSM_GROUP
        for q in range(n_quads):
            for ri in range(2):
                for ph in range(PHASES):
                    _expand_block_diag(zc_ref[ri, ph, q], wz_ref.at[ri, q, ph * ph_rows:(ph + 1) * ph_rows, :],
                                       SSM_GROUP, SSM_STATE, QUAD_GROUPS)
                by_phase = jnp.concatenate([vc_ref[ri, ph, q] for ph in range(PHASES)], axis=-1)
                _expand_block_diag(by_phase, v_ref.at[ri, q], SSM_STATE, SSM_GROUP, QUAD_GROUPS)
            for j in range(PHASES):
                by_phase = jnp.concatenate([locc_ref[j, p, q] for p in range(PHASES)], axis=-1)
                _expand_block_diag(by_phase, loc_ref.at[q, j * ph_rows:(j + 1) * ph_rows, :],
                                   SSM_GROUP, SSM_GROUP, QUAD_GROUPS)
        for hh in range(n_halves):
            _expand_block_diag(gluc_ref[hh], gw_ref.at[hh], SSM_GROUP, SSM_GROUP, HALF_GROUPS)

    def project(rows):
        n1 = _rms(rows, n1g_ref[...]).astype(_BF16)
        return _dot(n1, win_ref[...])

    def quad_lanes(parts, q):
        return jnp.concatenate([p[:, q * quad_in:(q + 1) * quad_in] for p in parts], axis=1)

    def s5_states(u_in, n_chunks, qq):
        rows = n_chunks * BATCH_ROWS
        for q in range(qq * SCAN_QUADS, (qq + 1) * SCAN_QUADS):
            qs = slice(q * quad_state, (q + 1) * quad_state)
            lhs = quad_lanes(u_in, q)
            zr[0:rows, qs] = _dot(lhs, wz_ref[0, q])
            zi[0:rows, qs] = _dot(lhs, wz_ref[1, q])
        cs = slice(qq * SCAN_QUADS * quad_state, (qq + 1) * SCAN_QUADS * quad_state)
        width = SCAN_QUADS * quad_state
        ar = jnp.broadcast_to(a4r_ref[:, cs], (BATCH_ROWS, width))
        ai = jnp.broadcast_to(a4i_ref[:, cs], (BATCH_ROWS, width))
        sr = st_r[:, cs]
        si = st_i[:, cs]
        for k in range(n_chunks):
            rs = slice(k * BATCH_ROWS, (k + 1) * BATCH_ROWS)
            sr, si = (ar * sr - ai * si + zr[rs, cs], ar * si + ai * sr + zi[rs, cs])
            zr[rs, cs] = sr
            zi[rs, cs] = si
        st_r[:, cs] = sr
        st_i[:, cs] = si

    def s5_inputs(u, n_chunks):
        parts = _split_phases(u, n_chunks)
        state_in = []
        for p in range(1, PHASES):
            carry = slice((p - 1) * BATCH_ROWS, p * BATCH_ROWS)
            state_in.append(jnp.concatenate([u_last[carry, :], parts[p][:-BATCH_ROWS]], axis=0).astype(_BF16))
            u_last[carry, :] = parts[p][-BATCH_ROWS:]
        local_in = [p.astype(_BF16) for p in parts]
        return state_in + [local_in[0]], local_in

    def ffn_chunk(n2, c):
        cs = slice(c * FF_CHUNK, (c + 1) * FF_CHUNK)
        gate = _dot(n2, wg_ref[:, cs])
        up = _dot(n2, wu_ref[:, cs])
        act[:, cs] = (gate * jax.nn.sigmoid(gate) * up).astype(_BF16)

    def step(with_prev, with_mixer):
        rows = T_TILE * BATCH_ROWS
        n_chunks = T_TILE // PHASES
        crows = n_chunks * BATCH_ROWS
        ff = iter(range(FFN_HEAD_CHUNKS, n_ff_chunks) if with_prev else ())

        def ffn_chunks(n):
            for _ in range(n):
                c = next(ff, None)
                if c is not None:
                    ffn_chunk(n2_prev, c)

        def finish_prev():
            ffn_chunks(n_ff_chunks)
            h2 = obuf[oslot].reshape(rows, obuf.shape[-1]) + _dot(act[...], wd_ref[...])
            obuf[oslot] = _rms(h2, fg_ref[...]).reshape(obuf.shape[1:])

        if with_prev:
            n2_prev = n2buf[...]
            obuf[oslot] = hbuf[...].reshape(obuf.shape[1:])
            ffn_chunks(1)
        if not with_mixer:
            finish_prev()
            return

        x = xbuf[slot].reshape(rows, xbuf.shape[-1])
        n1 = _rms(x, n1g_ref[...]).astype(_BF16)
        proj = _dot(n1, win_ref[...])
        u = proj[:, :d_ssm]
        v = proj[:, d_ssm:]

        state_in, local_in = s5_inputs(u, n_chunks)
        for qq in range(n_quads // SCAN_QUADS):
            s5_states(state_in, n_chunks, qq)
            ffn_chunks(1)
        chunk_out = []
        for q in range(n_quads):
            qs = slice(q * quad_state, (q + 1) * quad_state)
            chunk_out.append(_dot(zr[0:crows, qs].astype(_BF16), v_ref[0, q])
                             + _dot(zi[0:crows, qs].astype(_BF16), v_ref[1, q])
                             + _dot(quad_lanes(local_in, q), loc_ref[q]))
            if q % 4 == 3:
                ffn_chunks(1)
        y_parts = [jnp.concatenate([c[:, p * quad_in:(p + 1) * quad_in] for c in chunk_out], axis=-1)
                   for p in range(PHASES)]
        y = _merge_phases(y_parts, n_chunks) + d_ref[...] * u
        g = jax.nn.gelu(y)
        gb = g.astype(_BF16)
        ffn_chunks(1)
        gate = jnp.concatenate([_dot(gb[:, hh * half_in:(hh + 1) * half_in], gw_ref[hh])
                                for hh in range(n_halves)], axis=-1) + gb_ref[...]
        y_ssm = _rms(g * jax.nn.sigmoid(gate), sng_ref[...])
        ffn_chunks(1)

        vext = jnp.concatenate([halo[...], v], axis=0)
        halo[...] = v[rows - halo_rows:, :]
        gd = v.shape[-1] // len(POOL_WINDOWS)
        yps = []
        for k, w in enumerate(POOL_WINDOWS):
            acc = vext[:, k * gd:(k + 1) * gd]
            span = 1
            while span < w:
                shift = span * BATCH_ROWS
                acc = acc[shift:, :] + acc[:-shift, :]
                span *= 2
            yps.append(acc[acc.shape[0] - rows:, :] * (1.0 / w) - v[:, k * gd:(k + 1) * gd])
        y_pool = _rms(jnp.concatenate(yps, axis=-1), png_ref[...])

        mixed = jnp.concatenate([y_ssm, y_pool], axis=-1).astype(_BF16)
        h_next = x + _dot(mixed, wout_ref[...])
        hbuf[...] = h_next
        n2 = _rms(h_next, n2g_ref[...]).astype(_BF16)
        n2buf[...] = n2
        if with_prev:
            finish_prev()
        for c in range(FFN_HEAD_CHUNKS):
            ffn_chunk(n2, c)

    @pl.when(i == 0)
    def _():
        for c in fetch(0, 0):
            c.start()

    @pl.when(i + 1 < n_tiles)
    def _():
        for c in fetch(i + 1, oslot):
            c.start()

    @pl.when(i >= 3)
    def _():
        for c in writeback(i - 3, oslot):
            c.wait()

    @pl.when(i == 0)
    def _():
        build_weights()
        st_r[...] = jnp.zeros_like(st_r)
        st_i[...] = jnp.zeros_like(st_i)
        u_last[...] = jnp.zeros_like(u_last)
        pm = project(meta_ref[...])
        state_in, _ = s5_inputs(pm[:, :d_ssm], N_META // PHASES)
        for qq in range(n_quads // SCAN_QUADS):
            s5_states(state_in, N_META // PHASES, qq)
        halo[...] = pm[:, d_ssm:]

    @pl.when(i < n_tiles)
    def _():
        for c in fetch(i, slot):
            c.wait()

    @pl.when(i == 0)
    def _():
        step(with_prev=False, with_mixer=True)

    @pl.when(jnp.logical_and(i > 0, i < n_tiles))
    def _():
        step(with_prev=True, with_mixer=True)

    @pl.when(i == n_tiles)
    def _():
        step(with_prev=True, with_mixer=False)

    @pl.when(i >= 1)
    def _():
        for c in writeback(i - 1, oslot):
            c.start()

    @pl.when(i == n_tiles)
    def _():
        @pl.when(n_tiles >= 2)
        def _():
            for c in writeback(i - 2, slot):
                c.wait()
        for c in writeback(i - 1, oslot):
            c.wait()


def _const_spec(a):
    nd = a.ndim
    return pl.BlockSpec(a.shape, lambda i: (0,) * nd, pipeline_mode=pl.Buffered(1))


def _block(x, consts, hbm_weights, n_state, d_ssm, d_pool, d_ff):
    B, S, D = x.shape
    rows = T_TILE * B
    any_spec = pl.BlockSpec(memory_space=pl.ANY)
    n_quad = d_ssm // (QUAD_GROUPS * SSM_GROUP)
    n_half = d_ssm // (HALF_GROUPS * SSM_GROUP)
    tile_k = PHASES * QUAD_GROUPS * SSM_GROUP
    quad_state = QUAD_GROUPS * SSM_STATE
    return pl.pallas_call(
        _block_kernel,
        grid=(S // T_TILE + 1,),
        in_specs=[any_spec] + [_const_spec(c) for c in consts] + [any_spec] * len(hbm_weights),
        out_specs=any_spec,
        out_shape=jax.ShapeDtypeStruct((B, S, D), _F32),
        scratch_shapes=[
            pltpu.VMEM((2, T_TILE, B, D), _F32),
            pltpu.SemaphoreType.DMA((2, BATCH_ROWS)),
            pltpu.VMEM((2, T_TILE, B, D), _F32),
            pltpu.SemaphoreType.DMA((2, BATCH_ROWS)),
            pltpu.VMEM((rows, D), _F32),
            pltpu.VMEM((rows, D), _BF16),
            pltpu.VMEM((rows // PHASES, n_state), _F32),
            pltpu.VMEM((rows // PHASES, n_state), _F32),
            pltpu.VMEM((B, n_state), _F32),
            pltpu.VMEM((B, n_state), _F32),
            pltpu.VMEM(((PHASES - 1) * B, d_ssm), _F32),
            pltpu.VMEM((N_META * B, d_pool), _F32),
            pltpu.VMEM((rows, d_ff), _BF16),
            pltpu.VMEM((D, D), _BF16),
            pltpu.VMEM((D, D), _BF16),
            pltpu.VMEM((D, d_ff), _BF16),
            pltpu.VMEM((D, d_ff), _BF16),
            pltpu.VMEM((d_ff, D), _BF16),
            pltpu.VMEM((2, n_quad, tile_k, quad_state), _BF16),
            pltpu.VMEM((2, n_quad, quad_state, tile_k), _BF16),
            pltpu.VMEM((n_quad, tile_k, tile_k), _BF16),
            pltpu.VMEM((n_half, HALF_GROUPS * SSM_GROUP, HALF_GROUPS * SSM_GROUP), _BF16),
            pltpu.VMEM((STAGE_SLOTS * NARROW_STAGE_ROWS, D), _F32),
            pltpu.VMEM((STAGE_SLOTS * WIDE_STAGE_ROWS, d_ff), _F32),
            pltpu.SemaphoreType.DMA((STAGE_SLOTS,)),
            pltpu.SemaphoreType.DMA((STAGE_SLOTS,)),
        ],
        compiler_params=pltpu.CompilerParams(dimension_semantics=("arbitrary",),
                                             vmem_limit_bytes=VMEM_LIMIT),
        name="block",
    )(x, *consts, *hbm_weights)


def _lanes_by_group(blocks, per):
    G, r, c = blocks.shape
    return jnp.transpose(blocks.reshape(G // per, per, r, c), (0, 2, 1, 3)).reshape(G // per, r, per * c)


def _cmul(ar, ai, br, bi):
    return ar * br - ai * bi, ar * bi + ai * br


def _s5_params(lam_re, lam_im, log_step, b_re, b_im, c_re, c_im):
    lr = jnp.minimum(lam_re, -1e-4)
    li = lam_im
    step = jnp.exp(log_step)[:, None]
    mag = jnp.exp(lr * step)
    ang = li * step
    abr = mag * jnp.cos(ang)
    abi = mag * jnp.sin(ang)
    nr = abr - 1.0
    ni = abi
    den = lr * lr + li * li
    cr = ((nr * lr + ni * li) / den)[..., None]
    ci = ((ni * lr - nr * li) / den)[..., None]
    bbr = cr * b_re - ci * b_im
    bbi = cr * b_im + ci * b_re
    pw = [(jnp.ones_like(abr), jnp.zeros_like(abi))]
    for _ in range(PHASES):
        pw.append(_cmul(pw[-1][0], pw[-1][1], abr, abi))
    G = bbr.shape[0]
    Q = G // QUAD_GROUPS
    t = lambda a: jnp.swapaxes(a, 1, 2)

    def by_quad(blocks, lead):
        packed = _lanes_by_group(jnp.concatenate(blocks, axis=0), QUAD_GROUPS)
        return packed.reshape(lead + (Q,) + packed.shape[1:])

    ab = [_cmul(pr[..., None], pi[..., None], bbr, bbi) for pr, pi in pw[:PHASES]]
    ca = [_cmul(c_re, c_im, pr[:, None, :], pi[:, None, :]) for pr, pi in pw[:PHASES]]
    zc = by_quad([t(ab[PHASES - 1 - ph][ri]) for ri in range(2) for ph in range(PHASES)], (2, PHASES))
    sign = (1.0, -1.0)
    vc = by_quad([sign[ri] * t(ca[ph][ri]) for ri in range(2) for ph in range(PHASES)], (2, PHASES))
    loc = [jnp.einsum('ghp,gpk->ghk', ca[d][0], bbr) - jnp.einsum('ghp,gpk->ghk', ca[d][1], bbi)
           for d in range(PHASES - 1)]
    zero = jnp.zeros_like(loc[0])
    locc = by_quad([t(loc[p - j]) if 1 <= j <= p else zero for j in range(PHASES) for p in range(PHASES)],
                   (PHASES, PHASES))
    return pw[PHASES], zc, vc, locc


def kernel(x, meta_tokens, norm1_g, w_in, ssm_lambda_re, ssm_lambda_im, ssm_log_step, ssm_b_re, ssm_b_im, ssm_c_re, ssm_c_im, ssm_d, ssm_glu_w, ssm_glu_b, ssm_norm_g, pool_w, pool_scale, pool_norm_g, w_out, norm2_g, w_gate, w_up, w_down, final_norm_g):
    B, S, D = x.shape
    assert B == BATCH_ROWS and norm1_g.shape[0] == 1
    assert S % T_TILE == 0 and T_TILE % PHASES == 0 and N_META % PHASES == 0
    f32 = _F32
    (a4r, a4i), zc, vc, locc = _s5_params(
        ssm_lambda_re[0].astype(f32), ssm_lambda_im[0].astype(f32), ssm_log_step[0].astype(f32),
        ssm_b_re[0].astype(f32), ssm_b_im[0].astype(f32), ssm_c_re[0].astype(f32), ssm_c_im[0].astype(f32))
    row = lambda a: a.astype(f32).reshape(1, -1)
    consts = (
        jnp.repeat(meta_tokens.astype(f32), B, axis=0),
        row(norm1_g[0]),
        row(a4r), row(a4i),
        zc,
        vc,
        locc,
        row(ssm_d[0]),
        _lanes_by_group(ssm_glu_w[0].astype(f32), HALF_GROUPS),
        row(ssm_glu_b[0]),
        row(ssm_norm_g[0]),
        pool_w[0].astype(f32),
        row(pool_scale[0]),
        row(pool_norm_g[0]),
        row(norm2_g[0]),
        row(final_norm_g),
    )
    hbm_weights = tuple(w.astype(f32) for w in (w_in, w_out, w_gate, w_up, w_down))
    out = _block(x.astype(f32), consts, hbm_weights, n_state=a4r.size, d_ssm=ssm_d[0].size,
                 d_pool=pool_scale[0].size, d_ff=w_gate.shape[-1])
    return out.astype(x.dtype)
```

```python
import jax
import jax.numpy as jnp
from jax import lax
from jax.experimental import pallas as pl
from jax.experimental.pallas import tpu as pltpu

N_META = 16
SSM_GROUP = 16
SSM_STATE = 64
POOL_WINDOWS = (2, 4, 8, 16)
EPS = 1e-6

BATCH_ROWS = 8
HALF_GROUPS = 16
PHASES = 4
QUAD_GROUPS = 4
SCAN_QUADS = 2
T_TILE = 64
FF_CHUNK = 256
FFN_HEAD_CHUNKS = 1
STAGE_SLOTS = 8
WIDE_STAGE_ROWS = 32
NARROW_STAGE_ROWS = 128
VMEM_LIMIT = 60 * 1024 * 1024

_BF16 = jnp.bfloat16
_F32 = jnp.float32


def _rms(x, g):
    return x * lax.rsqrt(jnp.mean(x * x, axis=-1, keepdims=True) + EPS) * g


def _dot(a, b):
    return jnp.dot(a, b, preferred_element_type=_F32)


def _split_phases(a, n_chunks):
    blk = lambda t: a[t * BATCH_ROWS:(t + 1) * BATCH_ROWS]
    return [jnp.concatenate([blk(PHASES * k + p) for k in range(n_chunks)], axis=0) for p in range(PHASES)]


def _merge_phases(parts, n_chunks):
    blk = lambda a, k: a[k * BATCH_ROWS:(k + 1) * BATCH_ROWS]
    return jnp.concatenate([blk(p, k) for k in range(n_chunks) for p in parts], axis=0)


def _tile_copies(hbm, vmem, sems, slot, tile, nt, to_vmem):
    copies = []
    for b in range(BATCH_ROWS):
        h = hbm.at[b, pl.ds(tile * nt, nt), :]
        v = vmem.at[slot, :, b, :]
        src, dst = (h, v) if to_vmem else (v, h)
        copies.append(pltpu.make_async_copy(src, dst, sems.at[slot, b]))
    return copies


class _WeightStream:
    def __init__(self, w_hbm, dst, stage, sems, chunk, convert=None):
        self.args = (w_hbm, dst, stage, sems, chunk,
                     convert if convert is not None else (lambda rows: rows.astype(dst.dtype)))

    def _copy(self, c, s):
        w_hbm, _, stage, sems, chunk, _ = self.args
        return pltpu.make_async_copy(w_hbm.at[0, pl.ds(c * chunk, chunk), :],
                                     stage.at[pl.ds(s * chunk, chunk), :], sems.at[s])

    def prime(self):
        _, dst, _, _, chunk, _ = self.args
        for c in range(min(STAGE_SLOTS - 1, dst.shape[0] // chunk)):
            self._copy(c, c).start()

    def drain(self):
        _load_drain(self)


def _load_drain(stream):
    _, dst, stage, _, chunk, convert = stream.args
    copy = stream._copy
    n = dst.shape[0] // chunk
    ahead = STAGE_SLOTS - 1

    def body(c, carry):
        s = lax.rem(c, STAGE_SLOTS)

        @pl.when(c + ahead < n)
        def _():
            copy(c + ahead, lax.rem(c + ahead, STAGE_SLOTS)).start()

        copy(c, s).wait()
        src = pl.multiple_of(s * chunk, chunk)
        row = pl.multiple_of(c * chunk, chunk)
        dst[pl.ds(row, chunk), :] = convert(stage[pl.ds(src, chunk), :])
        return carry

    lax.fori_loop(0, n, body, 0)


def _expand_block_diag(src, dst, block_rows, block_lanes, n_blocks):
    lane = lax.broadcasted_iota(jnp.int32, src.shape, 1)
    lane_group = lax.rem(lane, n_blocks * block_lanes) // block_lanes
    for g in range(n_blocks):
        dst[g * block_rows:(g + 1) * block_rows, :] = jnp.where(lane_group == g, src, 0.0).astype(dst.dtype)


def _block_kernel(x_hbm, meta_ref, n1g_ref, a4r_ref, a4i_ref, zc_ref, vc_ref, locc_ref, d_ref, gluc_ref,
                  gb_ref, sng_ref, pw_ref, ps_ref, png_ref, n2g_ref, fg_ref,
                  win_hbm, wout_hbm, wg_hbm, wu_hbm, wd_hbm, o_hbm,
                  xbuf, xsem, obuf, osem, hbuf, n2buf, zr, zi, st_r, st_i, u_last, halo, act,
                  win_ref, wout_ref, wg_ref, wu_ref, wd_ref, wz_ref, v_ref, loc_ref, gw_ref,
                  narrow_stage, wide_stage, nsem, wsem):
    i = pl.program_id(0)
    n_tiles = pl.num_programs(0) - 1
    slot = lax.rem(i, 2)
    oslot = 1 - slot

    def fetch(tile, s):
        return _tile_copies(x_hbm, xbuf, xsem, s, tile, T_TILE, to_vmem=True)

    def writeback(tile, s):
        return _tile_copies(o_hbm, obuf, osem, s, tile, T_TILE, to_vmem=False)

    d_ssm = d_ref.shape[-1]
    half_in = HALF_GROUPS * SSM_GROUP
    quad_in = QUAD_GROUPS * SSM_GROUP
    quad_state = QUAD_GROUPS * SSM_STATE
    n_quads = d_ssm // quad_in
    n_halves = d_ssm // half_in
    halo_rows = halo.shape[0]
    n_ff_chunks = wg_ref.shape[-1] // FF_CHUNK

    def build_weights():
        gd = pw_ref.shape[-1]
        pool_maps = [pw_ref[k] * ps_ref[:, k * gd:(k + 1) * gd] for k in range(pw_ref.shape[0])]

        def fold_pool(rows):
            cols = [rows[:, :d_ssm]]
            for k, pm in enumerate(pool_maps):
                cols.append(jnp.dot(rows[:, d_ssm + k * gd:d_ssm + (k + 1) * gd], pm,
                                    precision=lax.Precision.HIGHEST, preferred_element_type=_F32))
            return jnp.concatenate(cols, axis=-1).astype(_BF16)

        w_in_s = _WeightStream(win_hbm, win_ref, narrow_stage, nsem, NARROW_STAGE_ROWS, convert=fold_pool)
        w_out_s = _WeightStream(wout_hbm, wout_ref, narrow_stage, nsem, NARROW_STAGE_ROWS)
        w_down_s = _WeightStream(wd_hbm, wd_ref, narrow_stage, nsem, NARROW_STAGE_ROWS)
        w_gate_s = _WeightStream(wg_hbm, wg_ref, wide_stage, wsem, WIDE_STAGE_ROWS)
        w_up_s = _WeightStream(wu_hbm, wu_ref, wide_stage, wsem, WIDE_STAGE_ROWS)
        w_in_s.prime()
        w_gate_s.prime()
        expand_group_maps()
        w_in_s.drain()
        w_out_s.prime()
        w_gate_s.drain()
        w_up_s.prime()
        w_out_s.drain()
        w_down_s.prime()
        w_up_s.drain()
        w_down_s.drain()

    def expand_group_maps():
        ph_rows = QUAD_GROUPS * SSM_GROUP
        for q in range(n_quads):
            for ri in range(2):
                for ph in range(PHASES):
                    _expand_block_diag(zc_ref[ri, ph, q], wz_ref.at[ri, q, ph * ph_rows:(ph + 1) * ph_rows, :],
                                       SSM_GROUP, SSM_STATE, QUAD_GROUPS)
                by_phase = jnp.concatenate([vc_ref[ri, ph, q] for ph in range(PHASES)], axis=-1)
                _expand_block_diag(by_phase, v_ref.at[ri, q], SSM_STATE, SSM_GROUP, QUAD_GROUPS)
            for j in range(PHASES):
                by_phase = jnp.concatenate([locc_ref[j, p, q] for p in range(PHASES)], axis=-1)
                _expand_block_diag(by_phase, loc_ref.at[q, j * ph_rows:(j + 1) * ph_rows, :],
                                   SSM_GROUP, SSM_GROUP, QUAD_GROUPS)
        for hh in range(n_halves):
            _expand_block_diag(gluc_ref[hh], gw_ref.at[hh], SSM_GROUP, SSM_GROUP, HALF_GROUPS)

    def project(rows):
        n1 = _rms(rows, n1g_ref[...]).astype(_BF16)
        return _dot(n1, win_ref[...])

    def quad_lanes(parts, q):
        return jnp.concatenate([p[:, q * quad_in:(q + 1) * quad_in] for p in parts], axis=1)

    def s5_states(u_in, n_chunks, qq):
        rows = n_chunks * BATCH_ROWS
        for q in range(qq * SCAN_QUADS, (qq + 1) * SCAN_QUADS):
            qs = slice(q * quad_state, (q + 1) * quad_state)
            lhs = quad_lanes(u_in, q)
            zr[0:rows, qs] = _dot(lhs, wz_ref[0, q])
            zi[0:rows, qs] = _dot(lhs, wz_ref[1, q])
        cs = slice(qq * SCAN_QUADS * quad_state, (qq + 1) * SCAN_QUADS * quad_state)
        width = SCAN_QUADS * quad_state
        ar = jnp.broadcast_to(a4r_ref[:, cs], (BATCH_ROWS, width))
        ai = jnp.broadcast_to(a4i_ref[:, cs], (BATCH_ROWS, width))
        sr = st_r[:, cs]
        si = st_i[:, cs]
        for k in range(n_chunks):
            rs = slice(k * BATCH_ROWS, (k + 1) * BATCH_ROWS)
            sr, si = (ar * sr - ai * si + zr[rs, cs], ar * si + ai * sr + zi[rs, cs])
            zr[rs, cs] = sr
            zi[rs, cs] = si
        st_r[:, cs] = sr
        st_i[:, cs] = si

    def s5_inputs(u, n_chunks):
        parts = _split_phases(u, n_chunks)
        state_in = []
        for p in range(1, PHASES):
            carry = slice((p - 1) * BATCH_ROWS, p * BATCH_ROWS)
            state_in.append(jnp.concatenate([u_last[carry, :], parts[p][:-BATCH_ROWS]], axis=0).astype(_BF16))
            u_last[carry, :] = parts[p][-BATCH_ROWS:]
        local_in = [p.astype(_BF16) for p in parts]
        return state_in + [local_in[0]], local_in

    def ffn_chunk(n2, c):
        cs = slice(c * FF_CHUNK, (c + 1) * FF_CHUNK)
        gate = _dot(n2, wg_ref[:, cs])
        up = _dot(n2, wu_ref[:, cs])
        act[:, cs] = (gate * jax.nn.sigmoid(gate) * up).astype(_BF16)

    def step(with_prev, with_mixer):
        rows = T_TILE * BATCH_ROWS
        n_chunks = T_TILE // PHASES
        crows = n_chunks * BATCH_ROWS
        ff = iter(range(FFN_HEAD_CHUNKS, n_ff_chunks) if with_prev else ())

        def ffn_chunks(n):
            for _ in range(n):
                c = next(ff, None)
                if c is not None:
                    ffn_chunk(n2_prev, c)

        def finish_prev():
            ffn_chunks(n_ff_chunks)
            h2 = obuf[oslot].reshape(rows, obuf.shape[-1]) + _dot(act[...], wd_ref[...])
            obuf[oslot] = _rms(h2, fg_ref[...]).reshape(obuf.shape[1:])

        if with_prev:
            n2_prev = n2buf[...]
            obuf[oslot] = hbuf[...].reshape(obuf.shape[1:])
            ffn_chunks(1)
        if not with_mixer:
            finish_prev()
            return

        x = xbuf[slot].reshape(rows, xbuf.shape[-1])
        n1 = _rms(x, n1g_ref[...]).astype(_BF16)
        proj = _dot(n1, win_ref[...])
        u = proj[:, :d_ssm]
        v = proj[:, d_ssm:]

        state_in, local_in = s5_inputs(u, n_chunks)
        for qq in range(n_quads // SCAN_QUADS):
            s5_states(state_in, n_chunks, qq)
            ffn_chunks(1)
        chunk_out = []
        for q in range(n_quads):
            qs = slice(q * quad_state, (q + 1) * quad_state)
            chunk_out.append(_dot(zr[0:crows, qs].astype(_BF16), v_ref[0, q])
                             + _dot(zi[0:crows, qs].astype(_BF16), v_ref[1, q])
                             + _dot(quad_lanes(local_in, q), loc_ref[q]))
            if q % 4 == 3:
                ffn_chunks(1)
        y_parts = [jnp.concatenate([c[:, p * quad_in:(p + 1) * quad_in] for c in chunk_out], axis=-1)
                   for p in range(PHASES)]
        y = _merge_phases(y_parts, n_chunks) + d_ref[...] * u
        g = jax.nn.gelu(y)
        gb = g.astype(_BF16)
        ffn_chunks(1)
        gate = jnp.concatenate([_dot(gb[:, hh * half_in:(hh + 1) * half_in], gw_ref[hh])
                                for hh in range(n_halves)], axis=-1) + gb_ref[...]
        y_ssm = _rms(g * jax.nn.sigmoid(gate), sng_ref[...])
        ffn_chunks(1)

        vext = jnp.concatenate([halo[...], v], axis=0)
        halo[...] = v[rows - halo_rows:, :]
        gd = v.shape[-1] // len(POOL_WINDOWS)
        yps = []
        for k, w in enumerate(POOL_WINDOWS):
            acc = vext[:, k * gd:(k + 1) * gd]
            span = 1
            while span < w:
                shift = span * BATCH_ROWS
                acc = acc[shift:, :] + acc[:-shift, :]
                span *= 2
            yps.append(acc[acc.shape[0] - rows:, :] * (1.0 / w) - v[:, k * gd:(k + 1) * gd])
        y_pool = _rms(jnp.concatenate(yps, axis=-1), png_ref[...])

        mixed = jnp.concatenate([y_ssm, y_pool], axis=-1).astype(_BF16)
        h_next = x + _dot(mixed, wout_ref[...])
        hbuf[...] = h_next
        n2 = _rms(h_next, n2g_ref[...]).astype(_BF16)
        n2buf[...] = n2
        if with_prev:
            finish_prev()
        for c in range(FFN_HEAD_CHUNKS):
            ffn_chunk(n2, c)

    @pl.when(i == 0)
    def _():
        for c in fetch(0, 0):
            c.start()

    @pl.when(i + 1 < n_tiles)
    def _():
        for c in fetch(i + 1, oslot):
            c.start()

    @pl.when(i >= 3)
    def _():
        for c in writeback(i - 3, oslot):
            c.wait()

    @pl.when(i == 0)
    def _():
        build_weights()
        st_r[...] = jnp.zeros_like(st_r)
        st_i[...] = jnp.zeros_like(st_i)
        u_last[...] = jnp.zeros_like(u_last)
        pm = project(meta_ref[...])
        state_in, _ = s5_inputs(pm[:, :d_ssm], N_META // PHASES)
        for qq in range(n_quads // SCAN_QUADS):
            s5_states(state_in, N_META // PHASES, qq)
        halo[...] = pm[:, d_ssm:]

    @pl.when(i < n_tiles)
    def _():
        for c in fetch(i, slot):
            c.wait()

    @pl.when(i == 0)
    def _():
        step(with_prev=False, with_mixer=True)

    @pl.when(jnp.logical_and(i > 0, i < n_tiles))
    def _():
        step(with_prev=True, with_mixer=True)

    @pl.when(i == n_tiles)
    def _():
        step(with_prev=True, with_mixer=False)

    @pl.when(i >= 1)
    def _():
        for c in writeback(i - 1, oslot):
            c.start()

    @pl.when(i == n_tiles)
    def _():
        @pl.when(n_tiles >= 2)
        def _():
            for c in writeback(i - 2, slot):
                c.wait()
        for c in writeback(i - 1, oslot):
            c.wait()


def _const_spec(a):
    nd = a.ndim
    return pl.BlockSpec(a.shape, lambda i: (0,) * nd, pipeline_mode=pl.Buffered(1))


def _block(x, consts, hbm_weights, n_state, d_ssm, d_pool, d_ff):
    B, S, D = x.shape
    rows = T_TILE * B
    any_spec = pl.BlockSpec(memory_space=pl.ANY)
    n_quad = d_ssm // (QUAD_GROUPS * SSM_GROUP)
    n_half = d_ssm // (HALF_GROUPS * SSM_GROUP)
    tile_k = PHASES * QUAD_GROUPS * SSM_GROUP
    quad_state = QUAD_GROUPS * SSM_STATE
    return pl.pallas_call(
        _block_kernel,
        grid=(S // T_TILE + 1,),
        in_specs=[any_spec] + [_const_spec(c) for c in consts] + [any_spec] * len(hbm_weights),
        out_specs=any_spec,
        out_shape=jax.ShapeDtypeStruct((B, S, D), _F32),
        scratch_shapes=[
            pltpu.VMEM((2, T_TILE, B, D), _F32),
            pltpu.SemaphoreType.DMA((2, BATCH_ROWS)),
            pltpu.VMEM((2, T_TILE, B, D), _F32),
            pltpu.SemaphoreType.DMA((2, BATCH_ROWS)),
            pltpu.VMEM((rows, D), _F32),
            pltpu.VMEM((rows, D), _BF16),
            pltpu.VMEM((rows // PHASES, n_state), _F32),
            pltpu.VMEM((rows // PHASES, n_state), _F32),
            pltpu.VMEM((B, n_state), _F32),
            pltpu.VMEM((B, n_state), _F32),
            pltpu.VMEM(((PHASES - 1) * B, d_ssm), _F32),
            pltpu.VMEM((N_META * B, d_pool), _F32),
            pltpu.VMEM((rows, d_ff), _BF16),
            pltpu.VMEM((D, D), _BF16),
            pltpu.VMEM((D, D), _BF16),
            pltpu.VMEM((D, d_ff), _BF16),
            pltpu.VMEM((D, d_ff), _BF16),
            pltpu.VMEM((d_ff, D), _BF16),
            pltpu.VMEM((2, n_quad, tile_k, quad_state), _BF16),
            pltpu.VMEM((2, n_quad, quad_state, tile_k), _BF16),
            pltpu.VMEM((n_quad, tile_k, tile_k), _BF16),
            pltpu.VMEM((n_half, HALF_GROUPS * SSM_GROUP, HALF_GROUPS * SSM_GROUP), _BF16),
            pltpu.VMEM((STAGE_SLOTS * NARROW_STAGE_ROWS, D), _F32),
            pltpu.VMEM((STAGE_SLOTS * WIDE_STAGE_ROWS, d_ff), _F32),
            pltpu.SemaphoreType.DMA((STAGE_SLOTS,)),
            pltpu.SemaphoreType.DMA((STAGE_SLOTS,)),
        ],
        compiler_params=pltpu.CompilerParams(dimension_semantics=("arbitrary",),
                                             vmem_limit_bytes=VMEM_LIMIT),
        name="block",
    )(x, *consts, *hbm_weights)


def _lanes_by_group(blocks, per):
    G, r, c = blocks.shape
    return jnp.transpose(blocks.reshape(G // per, per, r, c), (0, 2, 1, 3)).reshape(G // per, r, per * c)


def _cmul(ar, ai, br, bi):
    return ar * br - ai * bi, ar * bi + ai * br


def _s5_params(lam_re, lam_im, log_step, b_re, b_im, c_re, c_im):
    lr = jnp.minimum(lam_re, -1e-4)
    li = lam_im
    step = jnp.exp(log_step)[:, None]
    mag = jnp.exp(lr * step)
    ang = li * step
    abr = mag * jnp.cos(ang)
    abi = mag * jnp.sin(ang)
    nr = abr - 1.0
    ni = abi
    den = lr * lr + li * li
    cr = ((nr * lr + ni * li) / den)[..., None]
    ci = ((ni * lr - nr * li) / den)[..., None]
    bbr = cr * b_re - ci * b_im
    bbi = cr * b_im + ci * b_re
    pw = [(jnp.ones_like(abr), jnp.zeros_like(abi))]
    for _ in range(PHASES):
        pw.append(_cmul(pw[-1][0], pw[-1][1], abr, abi))
    G = bbr.shape[0]
    Q = G // QUAD_GROUPS
    t = lambda a: jnp.swapaxes(a, 1, 2)

    def by_quad(blocks, lead):
        packed = _lanes_by_group(jnp.concatenate(blocks, axis=0), QUAD_GROUPS)
        return packed.reshape(lead + (Q,) + packed.shape[1:])

    P, H = bbr.shape[1], bbr.shape[2]
    b_q = by_quad([t(b_re), t(b_im)], (2,))
    c_q = by_quad([t(c_re), t(c_im)], (2,))
    lane_row = lambda a: a.reshape(Q, 1, QUAD_GROUPS * P)
    p_rows = lambda a: jnp.repeat(jnp.swapaxes(a.reshape(Q, QUAD_GROUPS, P), 1, 2), H, axis=2)
    bb_q = _cmul(lane_row(cr[..., 0]), lane_row(ci[..., 0]), b_q[0], b_q[1])
    a_rows = (p_rows(abr), p_rows(abi))
    pw_rows = [(jnp.ones_like(a_rows[0]), jnp.zeros_like(a_rows[1]))]
    for _ in range(PHASES - 1):
        pw_rows.append(_cmul(pw_rows[-1][0], pw_rows[-1][1], a_rows[0], a_rows[1]))
    ab_q = [_cmul(lane_row(pr), lane_row(pi), bb_q[0], bb_q[1]) for pr, pi in pw[:PHASES]]
    ca_q = [_cmul(c_q[0], c_q[1], pr, pi) for pr, pi in pw_rows]
    zc = jnp.stack([jnp.stack([ab_q[PHASES - 1 - ph][ri] for ph in range(PHASES)]) for ri in range(2)])
    vc = jnp.stack([jnp.stack([ca_q[ph][0] for ph in range(PHASES)]),
                    jnp.stack([-ca_q[ph][1] for ph in range(PHASES)])])
    ca = [_cmul(c_re, c_im, pr[:, None, :], pi[:, None, :]) for pr, pi in pw[:PHASES - 1]]
    loc = [jnp.einsum('ghp,gpk->ghk', ca[d][0], bbr) - jnp.einsum('ghp,gpk->ghk', ca[d][1], bbi)
           for d in range(PHASES - 1)]
    zero = jnp.zeros_like(loc[0])
    locc = by_quad([t(loc[p - j]) if 1 <= j <= p else zero for j in range(PHASES) for p in range(PHASES)],
                   (PHASES, PHASES))
    return pw[PHASES], zc, vc, locc


def kernel(x, meta_tokens, norm1_g, w_in, ssm_lambda_re, ssm_lambda_im, ssm_log_step, ssm_b_re, ssm_b_im, ssm_c_re, ssm_c_im, ssm_d, ssm_glu_w, ssm_glu_b, ssm_norm_g, pool_w, pool_scale, pool_norm_g, w_out, norm2_g, w_gate, w_up, w_down, final_norm_g):
    B, S, D = x.shape
    assert B == BATCH_ROWS and norm1_g.shape[0] == 1
    assert S % T_TILE == 0 and T_TILE % PHASES == 0 and N_META % PHASES == 0
    f32 = _F32
    (a4r, a4i), zc, vc, locc = _s5_params(
        ssm_lambda_re[0].astype(f32), ssm_lambda_im[0].astype(f32), ssm_log_step[0].astype(f32),
        ssm_b_re[0].astype(f32), ssm_b_im[0].astype(f32), ssm_c_re[0].astype(f32), ssm_c_im[0].astype(f32))
    row = lambda a: a.astype(f32).reshape(1, -1)
    consts = (
        jnp.repeat(meta_tokens.astype(f32), B, axis=0),
        row(norm1_g[0]),
        row(a4r), row(a4i),
        zc,
        vc,
        locc,
        row(ssm_d[0]),
        _lanes_by_group(ssm_glu_w[0].astype(f32), HALF_GROUPS),
        row(ssm_glu_b[0]),
        row(ssm_norm_g[0]),
        pool_w[0].astype(f32),
        row(pool_scale[0]),
        row(pool_norm_g[0]),
        row(norm2_g[0]),
        row(final_norm_g),
    )
    hbm_weights = tuple(w.astype(f32) for w in (w_in, w_out, w_gate, w_up, w_down))
    out = _block(x.astype(f32), consts, hbm_weights, n_state=a4r.size, d_ssm=ssm_d[0].size,
                 d_pool=pool_scale[0].size, d_ff=w_gate.shape[-1])
    return out.astype(x.dtype)
```

```python
import jax
import jax.numpy as jnp
from jax import lax
from jax.experimental import pallas as pl
from jax.experimental.pallas import tpu as pltpu

N_META = 16
SSM_GROUP = 16
SSM_STATE = 64
POOL_WINDOWS = (2, 4, 8, 16)
EPS = 1e-6

BATCH_ROWS = 8
HALF_GROUPS = 16
PHASES = 4
QUAD_GROUPS = 4
SCAN_QUADS = 2
T_TILE = 64
FF_CHUNK = 256
FFN_HEAD_CHUNKS = 1
STAGE_SLOTS = 8
WIDE_STAGE_ROWS = 32
NARROW_STAGE_ROWS = 128
VMEM_LIMIT = 60 * 1024 * 1024

_BF16 = jnp.bfloat16
_F32 = jnp.float32


def _rms(x, g):
    return x * lax.rsqrt(jnp.mean(x * x, axis=-1, keepdims=True) + EPS) * g


def _dot(a, b):
    return jnp.dot(a, b, preferred_element_type=_F32)


def _split_phases(a, n_chunks):
    blk = lambda t: a[t * BATCH_ROWS:(t + 1) * BATCH_ROWS]
    return [jnp.concatenate([blk(PHASES * k + p) for k in range(n_chunks)], axis=0) for p in range(PHASES)]


def _merge_phases(parts, n_chunks):
    blk = lambda a, k: a[k * BATCH_ROWS:(k + 1) * BATCH_ROWS]
    return jnp.concatenate([blk(p, k) for k in range(n_chunks) for p in parts], axis=0)


def _tile_copies(hbm, vmem, sems, slot, tile, nt, to_vmem):
    copies = []
    for b in range(BATCH_ROWS):
        h = hbm.at[b, pl.ds(tile * nt, nt), :]
        v = vmem.at[slot, :, b, :]
        src, dst = (h, v) if to_vmem else (v, h)
        copies.append(pltpu.make_async_copy(src, dst, sems.at[slot, b]))
    return copies


class _WeightStream:
    def __init__(self, w_hbm, dst, stage, sems, chunk, convert=None):
        self.args = (w_hbm, dst, stage, sems, chunk,
                     convert if convert is not None else (lambda rows: rows.astype(dst.dtype)))

    def _copy(self, c, s):
        w_hbm, _, stage, sems, chunk, _ = self.args
        return pltpu.make_async_copy(w_hbm.at[0, pl.ds(c * chunk, chunk), :],
                                     stage.at[pl.ds(s * chunk, chunk), :], sems.at[s])

    def prime(self):
        _, dst, _, _, chunk, _ = self.args
        for c in range(min(STAGE_SLOTS - 1, dst.shape[0] // chunk)):
            self._copy(c, c).start()

    def drain(self):
        _load_drain(self)


def _load_drain(stream):
    _, dst, stage, _, chunk, convert = stream.args
    copy = stream._copy
    n = dst.shape[0] // chunk
    ahead = STAGE_SLOTS - 1

    def body(c, carry):
        s = lax.rem(c, STAGE_SLOTS)

        @pl.when(c + ahead < n)
        def _():
            copy(c + ahead, lax.rem(c + ahead, STAGE_SLOTS)).start()

        copy(c, s).wait()
        src = pl.multiple_of(s * chunk, chunk)
        row = pl.multiple_of(c * chunk, chunk)
        dst[pl.ds(row, chunk), :] = convert(stage[pl.ds(src, chunk), :])
        return carry

    lax.fori_loop(0, n, body, 0)


def _expand_block_diag(src, dst, block_rows, block_lanes, n_blocks):
    lane = lax.broadcasted_iota(jnp.int32, src.shape, 1)
    lane_group = lax.rem(lane, n_blocks * block_lanes) // block_lanes
    for g in range(n_blocks):
        dst[g * block_rows:(g + 1) * block_rows, :] = jnp.where(lane_group == g, src, 0.0).astype(dst.dtype)


def _block_kernel(x_hbm, meta_ref, n1g_ref, a4r_ref, a4i_ref, zc_ref, vc_ref, locc_ref, d_ref, gluc_ref,
                  gb_ref, sng_ref, pw_ref, ps_ref, png_ref, n2g_ref, fg_ref,
                  win_hbm, wout_hbm, wg_hbm, wu_hbm, wd_hbm, o_hbm,
                  xbuf, xsem, obuf, osem, hbuf, n2buf, zr, zi, st_r, st_i, u_last, halo, act,
                  win_ref, wout_ref, wg_ref, wu_ref, wd_ref, wz_ref, v_ref, loc_ref, gw_ref,
                  narrow_stage, wide_stage, nsem, wsem):
    i = pl.program_id(0)
    n_tiles = pl.num_programs(0) - 1
    slot = lax.rem(i, 2)
    oslot = 1 - slot

    def fetch(tile, s):
        return _tile_copies(x_hbm, xbuf, xsem, s, tile, T_TILE, to_vmem=True)

    def writeback(tile, s):
        return _tile_copies(o_hbm, obuf, osem, s, tile, T_TILE, to_vmem=False)

    d_ssm = d_ref.shape[-1]
    half_in = HALF_GROUPS * SSM_GROUP
    quad_in = QUAD_GROUPS * SSM_GROUP
    quad_state = QUAD_GROUPS * SSM_STATE
    n_quads = d_ssm // quad_in
    n_halves = d_ssm // half_in
    halo_rows = halo.shape[0]
    n_ff_chunks = wg_ref.shape[-1] // FF_CHUNK

    def build_weights():
        gd = pw_ref.shape[-1]
        pool_maps = [pw_ref[k] * ps_ref[:, k * gd:(k + 1) * gd] for k in range(pw_ref.shape[0])]

        def fold_pool(rows):
            cols = [rows[:, :d_ssm]]
            for k, pm in enumerate(pool_maps):
                cols.append(jnp.dot(rows[:, d_ssm + k * gd:d_ssm + (k + 1) * gd], pm,
                                    precision=lax.Precision.HIGHEST, preferred_element_type=_F32))
            return jnp.concatenate(cols, axis=-1).astype(_BF16)

        w_in_s = _WeightStream(win_hbm, win_ref, narrow_stage, nsem, NARROW_STAGE_ROWS, convert=fold_pool)
        w_out_s = _WeightStream(wout_hbm, wout_ref, narrow_stage, nsem, NARROW_STAGE_ROWS)
        w_down_s = _WeightStream(wd_hbm, wd_ref, narrow_stage, nsem, NARROW_STAGE_ROWS)
        w_gate_s = _WeightStream(wg_hbm, wg_ref, wide_stage, wsem, WIDE_STAGE_ROWS)
        w_up_s = _WeightStream(wu_hbm, wu_ref, wide_stage, wsem, WIDE_STAGE_ROWS)
        w_in_s.prime()
        w_gate_s.prime()
        expand_group_maps()
        w_in_s.drain()
        w_out_s.prime()
        w_gate_s.drain()
        w_up_s.prime()
        w_out_s.drain()
        w_down_s.prime()
        w_up_s.drain()
        w_down_s.drain()

    def expand_group_maps():
        ph_rows = QUAD_GROUPS * SSM_GROUP
        for q in range(n_quads):
            for ri in range(2):
                for ph in range(PHASES):
                    _expand_block_diag(zc_ref[ri, ph, q], wz_ref.at[ri, q, ph * ph_rows:(ph + 1) * ph_rows, :],
                                       SSM_GROUP, SSM_STATE, QUAD_GROUPS)
                by_phase = jnp.concatenate([vc_ref[ri, ph, q] for ph in range(PHASES)], axis=-1)
                _expand_block_diag(by_phase, v_ref.at[ri, q], SSM_STATE, SSM_GROUP, QUAD_GROUPS)
            for j in range(PHASES):
                by_phase = jnp.concatenate([locc_ref[j, p, q] for p in range(PHASES)], axis=-1)
                _expand_block_diag(by_phase, loc_ref.at[q, j * ph_rows:(j + 1) * ph_rows, :],
                                   SSM_GROUP, SSM_GROUP, QUAD_GROUPS)
        for hh in range(n_halves):
            _expand_block_diag(gluc_ref[hh], gw_ref.at[hh], SSM_GROUP, SSM_GROUP, HALF_GROUPS)

    def project(rows):
        n1 = _rms(rows, n1g_ref[...]).astype(_BF16)
        return _dot(n1, win_ref[...])

    def quad_lanes(parts, q):
        return jnp.concatenate([p[:, q * quad_in:(q + 1) * quad_in] for p in parts], axis=1)

    def s5_states(u_in, n_chunks, qq):
        rows = n_chunks * BATCH_ROWS
        for q in range(qq * SCAN_QUADS, (qq + 1) * SCAN_QUADS):
            qs = slice(q * quad_state, (q + 1) * quad_state)
            lhs = quad_lanes(u_in, q)
            zr[0:rows, qs] = _dot(lhs, wz_ref[0, q])
            zi[0:rows, qs] = _dot(lhs, wz_ref[1, q])
        cs = slice(qq * SCAN_QUADS * quad_state, (qq + 1) * SCAN_QUADS * quad_state)
        width = SCAN_QUADS * quad_state
        ar = jnp.broadcast_to(a4r_ref[:, cs], (BATCH_ROWS, width))
        ai = jnp.broadcast_to(a4i_ref[:, cs], (BATCH_ROWS, width))
        sr = st_r[:, cs]
        si = st_i[:, cs]
        for k in range(n_chunks):
            rs = slice(k * BATCH_ROWS, (k + 1) * BATCH_ROWS)
            sr, si = (ar * sr - ai * si + zr[rs, cs], ar * si + ai * sr + zi[rs, cs])
            zr[rs, cs] = sr
            zi[rs, cs] = si
        st_r[:, cs] = sr
        st_i[:, cs] = si

    def s5_inputs(u, n_chunks):
        parts = _split_phases(u, n_chunks)
        state_in = []
        for p in range(1, PHASES):
            carry = slice((p - 1) * BATCH_ROWS, p * BATCH_ROWS)
            state_in.append(jnp.concatenate([u_last[carry, :], parts[p][:-BATCH_ROWS]], axis=0).astype(_BF16))
            u_last[carry, :] = parts[p][-BATCH_ROWS:]
        local_in = [p.astype(_BF16) for p in parts]
        return state_in + [local_in[0]], local_in

    def ffn_chunk(n2, c):
        cs = slice(c * FF_CHUNK, (c + 1) * FF_CHUNK)
        gate = _dot(n2, wg_ref[:, cs])
        up = _dot(n2, wu_ref[:, cs])
        act[:, cs] = (gate * jax.nn.sigmoid(gate) * up).astype(_BF16)

    def step(with_prev, with_mixer):
        rows = T_TILE * BATCH_ROWS
        n_chunks = T_TILE // PHASES
        crows = n_chunks * BATCH_ROWS
        ff = iter(range(FFN_HEAD_CHUNKS, n_ff_chunks) if with_prev else ())

        def ffn_chunks(n):
            for _ in range(n):
                c = next(ff, None)
                if c is not None:
                    ffn_chunk(n2_prev, c)

        def finish_prev():
            ffn_chunks(n_ff_chunks)
            h2 = obuf[oslot].reshape(rows, obuf.shape[-1]) + _dot(act[...], wd_ref[...])
            obuf[oslot] = _rms(h2, fg_ref[...]).reshape(obuf.shape[1:])

        if with_prev:
            n2_prev = n2buf[...]
            obuf[oslot] = hbuf[...].reshape(obuf.shape[1:])
            ffn_chunks(1)
        if not with_mixer:
            finish_prev()
            return

        x = xbuf[slot].reshape(rows, xbuf.shape[-1])
        n1 = _rms(x, n1g_ref[...]).astype(_BF16)
        proj = _dot(n1, win_ref[...])
        u = proj[:, :d_ssm]
        v = proj[:, d_ssm:]

        state_in, local_in = s5_inputs(u, n_chunks)
        for qq in range(n_quads // SCAN_QUADS):
            s5_states(state_in, n_chunks, qq)
            ffn_chunks(1)
        chunk_out = []
        for q in range(n_quads):
            qs = slice(q * quad_state, (q + 1) * quad_state)
            chunk_out.append(_dot(zr[0:crows, qs].astype(_BF16), v_ref[0, q])
                             + _dot(zi[0:crows, qs].astype(_BF16), v_ref[1, q])
                             + _dot(quad_lanes(local_in, q), loc_ref[q]))
            if q % 4 == 3:
                ffn_chunks(1)
        y_parts = [jnp.concatenate([c[:, p * quad_in:(p + 1) * quad_in] for c in chunk_out], axis=-1)
                   for p in range(PHASES)]
        y = _merge_phases(y_parts, n_chunks) + d_ref[...] * u
        g = jax.nn.gelu(y)
        gb = g.astype(_BF16)
        ffn_chunks(1)
        gate = jnp.concatenate([_dot(gb[:, hh * half_in:(hh + 1) * half_in], gw_ref[hh])
                                for hh in range(n_halves)], axis=-1) + gb_ref[...]
        y_ssm = _rms(g * jax.nn.sigmoid(gate), sng_ref[...])
        ffn_chunks(1)

        vext = jnp.concatenate([halo[...], v], axis=0)
        halo[...] = v[rows - halo_rows:, :]
        gd = v.shape[-1] // len(POOL_WINDOWS)
        yps = []
        for k, w in enumerate(POOL_WINDOWS):
            acc = vext[:, k * gd:(k + 1) * gd]
            span = 1
            while span < w:
                shift = span * BATCH_ROWS
                acc = acc[shift:, :] + acc[:-shift, :]
                span *= 2
            yps.append(acc[acc.shape[0] - rows:, :] * (1.0 / w) - v[:, k * gd:(k + 1) * gd])
        y_pool = _rms(jnp.concatenate(yps, axis=-1), png_ref[...])

        mixed = jnp.concatenate([y_ssm, y_pool], axis=-1).astype(_BF16)
        h_next = x + _dot(mixed, wout_ref[...])
        hbuf[...] = h_next
        n2 = _rms(h_next, n2g_ref[...]).astype(_BF16)
        n2buf[...] = n2
        if with_prev:
            finish_prev()
        for c in range(FFN_HEAD_CHUNKS):
            ffn_chunk(n2, c)

    @pl.when(i == 0)
    def _():
        for c in fetch(0, 0):
            c.start()

    @pl.when(i + 1 < n_tiles)
    def _():
        for c in fetch(i + 1, oslot):
            c.start()

    @pl.when(i >= 3)
    def _():
        for c in writeback(i - 3, oslot):
            c.wait()

    @pl.when(i == 0)
    def _():
        build_weights()
        st_r[...] = jnp.zeros_like(st_r)
        st_i[...] = jnp.zeros_like(st_i)
        u_last[...] = jnp.zeros_like(u_last)
        meta_rows = jnp.concatenate(
            [jnp.broadcast_to(meta_ref[t:t + 1, :], (BATCH_ROWS, meta_ref.shape[-1])) for t in range(N_META)], axis=0)
        pm = project(meta_rows)
        state_in, _ = s5_inputs(pm[:, :d_ssm], N_META // PHASES)
        for qq in range(n_quads // SCAN_QUADS):
            s5_states(state_in, N_META // PHASES, qq)
        halo[...] = pm[:, d_ssm:]

    @pl.when(i < n_tiles)
    def _():
        for c in fetch(i, slot):
            c.wait()

    @pl.when(i == 0)
    def _():
        step(with_prev=False, with_mixer=True)

    @pl.when(jnp.logical_and(i > 0, i < n_tiles))
    def _():
        step(with_prev=True, with_mixer=True)

    @pl.when(i == n_tiles)
    def _():
        step(with_prev=True, with_mixer=False)

    @pl.when(i >= 1)
    def _():
        for c in writeback(i - 1, oslot):
            c.start()

    @pl.when(i == n_tiles)
    def _():
        @pl.when(n_tiles >= 2)
        def _():
            for c in writeback(i - 2, slot):
                c.wait()
        for c in writeback(i - 1, oslot):
            c.wait()


def _const_spec(a):
    nd = a.ndim
    return pl.BlockSpec(a.shape, lambda i: (0,) * nd, pipeline_mode=pl.Buffered(1))


def _block(x, consts, hbm_weights, n_state, d_ssm, d_pool, d_ff):
    B, S, D = x.shape
    rows = T_TILE * B
    any_spec = pl.BlockSpec(memory_space=pl.ANY)
    n_quad = d_ssm // (QUAD_GROUPS * SSM_GROUP)
    n_half = d_ssm // (HALF_GROUPS * SSM_GROUP)
    tile_k = PHASES * QUAD_GROUPS * SSM_GROUP
    quad_state = QUAD_GROUPS * SSM_STATE
    return pl.pallas_call(
        _block_kernel,
        grid=(S // T_TILE + 1,),
        in_specs=[any_spec] + [_const_spec(c) for c in consts] + [any_spec] * len(hbm_weights),
        out_specs=any_spec,
        out_shape=jax.ShapeDtypeStruct((B, S, D), _F32),
        scratch_shapes=[
            pltpu.VMEM((2, T_TILE, B, D), _F32),
            pltpu.SemaphoreType.DMA((2, BATCH_ROWS)),
            pltpu.VMEM((2, T_TILE, B, D), _F32),
            pltpu.SemaphoreType.DMA((2, BATCH_ROWS)),
            pltpu.VMEM((rows, D), _F32),
            pltpu.VMEM((rows, D), _BF16),
            pltpu.VMEM((rows // PHASES, n_state), _F32),
            pltpu.VMEM((rows // PHASES, n_state), _F32),
            pltpu.VMEM((B, n_state), _F32),
            pltpu.VMEM((B, n_state), _F32),
            pltpu.VMEM(((PHASES - 1) * B, d_ssm), _F32),
            pltpu.VMEM((N_META * B, d_pool), _F32),
            pltpu.VMEM((rows, d_ff), _BF16),
            pltpu.VMEM((D, D), _BF16),
            pltpu.VMEM((D, D), _BF16),
            pltpu.VMEM((D, d_ff), _BF16),
            pltpu.VMEM((D, d_ff), _BF16),
            pltpu.VMEM((d_ff, D), _BF16),
            pltpu.VMEM((2, n_quad, tile_k, quad_state), _BF16),
            pltpu.VMEM((2, n_quad, quad_state, tile_k), _BF16),
            pltpu.VMEM((n_quad, tile_k, tile_k), _BF16),
            pltpu.VMEM((n_half, HALF_GROUPS * SSM_GROUP, HALF_GROUPS * SSM_GROUP), _BF16),
            pltpu.VMEM((STAGE_SLOTS * NARROW_STAGE_ROWS, D), _F32),
            pltpu.VMEM((STAGE_SLOTS * WIDE_STAGE_ROWS, d_ff), _F32),
            pltpu.SemaphoreType.DMA((STAGE_SLOTS,)),
            pltpu.SemaphoreType.DMA((STAGE_SLOTS,)),
        ],
        compiler_params=pltpu.CompilerParams(dimension_semantics=("arbitrary",),
                                             vmem_limit_bytes=VMEM_LIMIT),
        name="block",
    )(x, *consts, *hbm_weights)


def _lanes_by_group(blocks, per):
    G, r, c = blocks.shape
    return jnp.transpose(blocks.reshape(G // per, per, r, c), (0, 2, 1, 3)).reshape(G // per, r, per * c)


def _cmul(ar, ai, br, bi):
    return ar * br - ai * bi, ar * bi + ai * br


def _s5_params(lam_re, lam_im, log_step, b_re, b_im, c_re, c_im):
    lr = jnp.minimum(lam_re, -1e-4)
    li = lam_im
    step = jnp.exp(log_step)[:, None]
    mag = jnp.exp(lr * step)
    ang = li * step
    abr = mag * jnp.cos(ang)
    abi = mag * jnp.sin(ang)
    nr = abr - 1.0
    ni = abi
    den = lr * lr + li * li
    cr = ((nr * lr + ni * li) / den)[..., None]
    ci = ((ni * lr - nr * li) / den)[..., None]
    bbr = cr * b_re - ci * b_im
    bbi = cr * b_im + ci * b_re
    pw = [(jnp.ones_like(abr), jnp.zeros_like(abi))]
    for _ in range(PHASES):
        pw.append(_cmul(pw[-1][0], pw[-1][1], abr, abi))
    G = bbr.shape[0]
    Q = G // QUAD_GROUPS
    t = lambda a: jnp.swapaxes(a, 1, 2)

    def by_quad(blocks, lead):
        packed = _lanes_by_group(jnp.concatenate(blocks, axis=0), QUAD_GROUPS)
        return packed.reshape(lead + (Q,) + packed.shape[1:])

    ab = [_cmul(pr[..., None], pi[..., None], bbr, bbi) for pr, pi in pw[:PHASES]]
    ca = [_cmul(c_re, c_im, pr[:, None, :], pi[:, None, :]) for pr, pi in pw[:PHASES]]
    zc = by_quad([t(ab[PHASES - 1 - ph][ri]) for ri in range(2) for ph in range(PHASES)], (2, PHASES))
    sign = (1.0, -1.0)
    vc = by_quad([sign[ri] * t(ca[ph][ri]) for ri in range(2) for ph in range(PHASES)], (2, PHASES))
    loc = [jnp.einsum('ghp,gpk->ghk', ca[d][0], bbr) - jnp.einsum('ghp,gpk->ghk', ca[d][1], bbi)
           for d in range(PHASES - 1)]
    zero = jnp.zeros_like(loc[0])
    locc = by_quad([t(loc[p - j]) if 1 <= j <= p else zero for j in range(PHASES) for p in range(PHASES)],
                   (PHASES, PHASES))
    return pw[PHASES], zc, vc, locc


def kernel(x, meta_tokens, norm1_g, w_in, ssm_lambda_re, ssm_lambda_im, ssm_log_step, ssm_b_re, ssm_b_im, ssm_c_re, ssm_c_im, ssm_d, ssm_glu_w, ssm_glu_b, ssm_norm_g, pool_w, pool_scale, pool_norm_g, w_out, norm2_g, w_gate, w_up, w_down, final_norm_g):
    B, S, D = x.shape
    assert B == BATCH_ROWS and norm1_g.shape[0] == 1
    assert S % T_TILE == 0 and T_TILE % PHASES == 0 and N_META % PHASES == 0
    f32 = _F32
    (a4r, a4i), zc, vc, locc = _s5_params(
        ssm_lambda_re[0].astype(f32), ssm_lambda_im[0].astype(f32), ssm_log_step[0].astype(f32),
        ssm_b_re[0].astype(f32), ssm_b_im[0].astype(f32), ssm_c_re[0].astype(f32), ssm_c_im[0].astype(f32))
    row = lambda a: a.astype(f32).reshape(1, -1)
    consts = (
        meta_tokens.astype(f32),
        row(norm1_g[0]),
        row(a4r), row(a4i),
        zc,
        vc,
        locc,
        row(ssm_d[0]),
        _lanes_by_group(ssm_glu_w[0].astype(f32), HALF_GROUPS),
        row(ssm_glu_b[0]),
        row(ssm_norm_g[0]),
        pool_w[0].astype(f32),
        row(pool_scale[0]),
        row(pool_norm_g[0]),
        row(norm2_g[0]),
        row(final_norm_g),
    )
    hbm_weights = tuple(w.astype(f32) for w in (w_in, w_out, w_gate, w_up, w_down))
    out = _block(x.astype(f32), consts, hbm_weights, n_state=a4r.size, d_ssm=ssm_d[0].size,
                 d_pool=pool_scale[0].size, d_ff=w_gate.shape[-1])
    return out.astype(x.dtype)
```
